```python
import math
import jax, jax.numpy as jnp
from jax import lax
import numpy as np

D_MODEL = 1024
BATCH = 8
SEQ = 4096
DEPTH = 2

HEAD_DIM = 64
NSA_HEADS = 6
NSA_KV_HEADS = 2
NSA_CMP_BLOCK = 32
NSA_CMP_STRIDE = 16
NSA_SEL_BLOCK = 64
NSA_N_SEL = 16
NSA_WINDOW = 512
NSA_CMP_HIDDEN = 256
NSA_Q_BLOCK = 64
MOBA_HEADS = 4
MOBA_BLOCK = 256
MOBA_TOPK = 3
MOBA_Q_BLOCK = 32
DIL_PATTERNS = ((128, 1), (512, 4), (2048, 16))
DIL_HEADS_PER_GROUP = 2
DIL_HEADS = DIL_HEADS_PER_GROUP * len(DIL_PATTERNS)
N_HEADS_TOTAL = NSA_HEADS + MOBA_HEADS + DIL_HEADS
N_BRANCHES = 3
REL_BUCKETS = 32
REL_MAX_EXACT = 16
REL_MAX_DIST = 2048
D_FF = 2816
NORM_EPS = 1e-6
NEG_INF = -1e30
FORCE_SCORE = 1e4

KV_W = NSA_KV_HEADS * HEAD_DIM
IN_LAYOUT = (
    ('nsa_q', NSA_HEADS * HEAD_DIM),
    ('nsa_k_cmp', KV_W), ('nsa_v_cmp', KV_W),
    ('nsa_k_sel', KV_W), ('nsa_v_sel', KV_W),
    ('nsa_k_win', KV_W), ('nsa_v_win', KV_W),
    ('nsa_gate', NSA_HEADS * 3),
    ('moba_q', MOBA_HEADS * HEAD_DIM), ('moba_k', MOBA_HEADS * HEAD_DIM), ('moba_v', MOBA_HEADS * HEAD_DIM),
    ('dil_q', DIL_HEADS * HEAD_DIM), ('dil_k', DIL_HEADS * HEAD_DIM), ('dil_v', DIL_HEADS * HEAD_DIM),
    ('merge_gate', N_BRANCHES * D_MODEL),
)
D_IN = sum(w for _, w in IN_LAYOUT)

kernel_name = 'hybrid_nsa_moba_dilated_macaron'


def rms_norm(x, gain):
    xf = x.astype(jnp.float32)
    y = xf * lax.rsqrt(jnp.mean(xf * xf, axis=-1, keepdims=True) + NORM_EPS)
    return (y * gain.astype(jnp.float32)).astype(x.dtype)


def swiglu(x, w_gate, w_up, w_down):
    return (jax.nn.silu(x @ w_gate) * (x @ w_up)) @ w_down


def rel_bucket(dist):
    n = jnp.maximum(dist, 0)
    nf = jnp.maximum(n, REL_MAX_EXACT).astype(jnp.float32)
    large = REL_MAX_EXACT + (jnp.log(nf / REL_MAX_EXACT) / math.log(REL_MAX_DIST / REL_MAX_EXACT)
                             * (REL_BUCKETS - REL_MAX_EXACT)).astype(jnp.int32)
    large = jnp.minimum(large, REL_BUCKETS - 1)
    return jnp.where(n < REL_MAX_EXACT, n, large)


def masked_softmax(logits, mask):
    l = jnp.where(mask, logits.astype(jnp.float32), NEG_INF)
    m = jnp.max(l, axis=-1, keepdims=True)
    e = jnp.where(mask, jnp.exp(l - m), 0.0)
    s = jnp.maximum(jnp.sum(e, axis=-1, keepdims=True), 1e-30)
    return e / s, (m + jnp.log(s))[..., 0]


def split_columns(proj):
    offs = np.cumsum([w for _, w in IN_LAYOUT])[:-1]
    parts = jnp.split(proj, offs, axis=-1)
    return {name: p for (name, _), p in zip(IN_LAYOUT, parts)}


def nsa_compress(kv, pe, w1, w2):
    B, S, G, hd = kv.shape
    n_cmp = (S - NSA_CMP_BLOCK) // NSA_CMP_STRIDE + 1
    idx = np.arange(n_cmp)[:, None] * NSA_CMP_STRIDE + np.arange(NSA_CMP_BLOCK)[None, :]
    blocks = kv[:, idx] + pe[None, None, :, None, :]
    blocks = blocks.transpose(0, 3, 1, 2, 4).reshape(B, G, n_cmp, NSA_CMP_BLOCK * hd)
    return jax.nn.gelu(blocks @ w1) @ w2


def nsa_mixer(q, k_cmp, v_cmp, k_sel, v_sel, k_win, v_win, gate_logits, bias_tbl,
              pe_k, pe_v, phi_k1, phi_k2, phi_v1, phi_v2):
    B, S, _ = q.shape
    G, R, hd = NSA_KV_HEADS, NSA_HEADS // NSA_KV_HEADS, HEAD_DIM
    scale = hd ** -0.5
    Qb = NSA_Q_BLOCK
    qh = q.reshape(B, S, G, R, hd).transpose(0, 2, 3, 1, 4)
    gates = jax.nn.sigmoid(gate_logits.reshape(B, S, G, R, 3)).transpose(0, 2, 3, 1, 4)
    tbl = bias_tbl.T.reshape(G, R, REL_BUCKETS)
    heads = lambda t: t.reshape(B, S, G, hd)
    kc = nsa_compress(heads(k_cmp), pe_k, phi_k1, phi_k2)
    vc = nsa_compress(heads(v_cmp), pe_v, phi_v1, phi_v2)
    n_cmp = kc.shape[2]
    cmp_end = jnp.arange(n_cmp) * NSA_CMP_STRIDE + NSA_CMP_BLOCK - 1
    n_slc = S // NSA_SEL_BLOCK
    n_sel = min(NSA_N_SEL, n_slc)
    c_start = np.arange(n_cmp) * NSA_CMP_STRIDE
    s_start = np.arange(n_slc) * NSA_SEL_BLOCK
    overlap = (c_start[:, None] < s_start[None, :] + NSA_SEL_BLOCK) & (c_start[:, None] + NSA_CMP_BLOCK > s_start[None, :])
    cmp_to_slc = jnp.asarray(overlap, dtype=jnp.float32)
    ks_blocks = heads(k_sel).transpose(0, 2, 1, 3).reshape(B, G, n_slc, NSA_SEL_BLOCK, hd)
    vs_blocks = heads(v_sel).transpose(0, 2, 1, 3).reshape(B, G, n_slc, NSA_SEL_BLOCK, hd)
    wpad = ((0, 0), (0, 0), (NSA_WINDOW, 0), (0, 0))
    kw_pad = jnp.pad(heads(k_win).transpose(0, 2, 1, 3), wpad)
    vw_pad = jnp.pad(heads(v_win).transpose(0, 2, 1, 3), wpad)
    bi = jnp.arange(B)[:, None, None, None]
    gi = jnp.arange(G)[None, :, None, None]
    ri = jnp.arange(R)[None, None, :, None, None]
    blk = jnp.arange(n_slc)

    def block_step(j):
        q0 = j * Qb
        t = q0 + jnp.arange(Qb)
        qb = lax.dynamic_slice_in_dim(qh, q0, Qb, axis=3)
        gb = lax.dynamic_slice_in_dim(gates, q0, Qb, axis=3)
        dist_c = t[:, None] - cmp_end[None, :]
        logit_c = jnp.einsum('bgrqd,bgkd->bgrqk', qb, kc).astype(jnp.float32) * scale + tbl[:, :, rel_bucket(dist_c)]
        p_c, _ = masked_softmax(logit_c, dist_c >= 0)
        o_c = jnp.einsum('bgrqk,bgkd->bgrqd', p_c.astype(vc.dtype), vc)
        imp = jnp.einsum('bgrqk,ks->bgqs', p_c, cmp_to_slc)
        cur = t // NSA_SEL_BLOCK
        forced = (blk[None, :] == 0) | (blk[None, :] == cur[:, None]) | (blk[None, :] == cur[:, None] - 1)
        future = blk[None, :] * NSA_SEL_BLOCK > t[:, None]
        imp = jnp.where(future, NEG_INF, jnp.where(forced, FORCE_SCORE, imp))
        _, idx = lax.top_k(imp, n_sel)
        ks = ks_blocks[bi, gi, idx].reshape(B, G, Qb, n_sel * NSA_SEL_BLOCK, hd)
        vs = vs_blocks[bi, gi, idx].reshape(B, G, Qb, n_sel * NSA_SEL_BLOCK, hd)
        pos_s = (idx[..., None] * NSA_SEL_BLOCK + jnp.arange(NSA_SEL_BLOCK)).reshape(B, G, Qb, n_sel * NSA_SEL_BLOCK)
        dist_s = t[None, None, :, None] - pos_s
        bias_s = tbl[gi[..., None], ri, rel_bucket(dist_s)[:, :, None]]
        logit_s = jnp.einsum('bgrqd,bgqkd->bgrqk', qb, ks).astype(jnp.float32) * scale + bias_s
        p_s, _ = masked_softmax(logit_s, (dist_s >= 0)[:, :, None])
        o_s = jnp.einsum('bgrqk,bgqkd->bgrqd', p_s.astype(vs.dtype), vs)
        kw = lax.dynamic_slice_in_dim(kw_pad, q0, NSA_WINDOW + Qb, axis=2)
        vw = lax.dynamic_slice_in_dim(vw_pad, q0, NSA_WINDOW + Qb, axis=2)
        pos_w = q0 - NSA_WINDOW + jnp.arange(NSA_WINDOW + Qb)
        dist_w = t[:, None] - pos_w[None, :]
        mask_w = (pos_w[None, :] >= 0) & (dist_w >= 0) & (dist_w < NSA_WINDOW)
        logit_w = jnp.einsum('bgrqd,bgkd->bgrqk', qb, kw).astype(jnp.float32) * scale + tbl[:, :, rel_bucket(dist_w)]
        p_w, _ = masked_softmax(logit_w, mask_w)
        o_w = jnp.einsum('bgrqk,bgkd->bgrqd', p_w.astype(vw.dtype), vw)
        return gb[..., 0:1] * o_c + gb[..., 1:2] * o_s + gb[..., 2:3] * o_w

    out = lax.map(block_step, jnp.arange(S // Qb))
    return out.transpose(1, 0, 4, 2, 3, 5).reshape(B, S, NSA_HEADS * hd)


def moba_mixer(q, k, v, bias_tbl):
    B, S, _ = q.shape
    H, hd, Qb = MOBA_HEADS, HEAD_DIM, MOBA_Q_BLOCK
    scale = hd ** -0.5
    to_heads = lambda t: t.reshape(B, S, H, hd).transpose(0, 2, 1, 3)
    qh, kh, vh = to_heads(q), to_heads(k), to_heads(v)
    nb = -(-S // MOBA_BLOCK)
    pad = ((0, 0), (0, 0), (0, nb * MOBA_BLOCK - S), (0, 0))
    kp, vp = jnp.pad(kh, pad), jnp.pad(vh, pad)
    k_blocks = kp.reshape(B, H, nb, MOBA_BLOCK, hd)
    v_blocks = vp.reshape(B, H, nb, MOBA_BLOCK, hd)
    k_mean = jnp.mean(k_blocks.astype(jnp.float32), axis=3)
    n_top = min(MOBA_TOPK, nb - 1)
    tbl = bias_tbl.T
    bi = jnp.arange(B)[:, None, None, None]
    hi = jnp.arange(H)[None, :, None, None]

    def block_step(j):
        q0 = j * Qb
        c = q0 // MOBA_BLOCK
        t = q0 + jnp.arange(Qb)
        qb = lax.dynamic_slice_in_dim(qh, q0, Qb, axis=2)
        k_own = lax.dynamic_slice_in_dim(kp, c * MOBA_BLOCK, MOBA_BLOCK, axis=2)
        v_own = lax.dynamic_slice_in_dim(vp, c * MOBA_BLOCK, MOBA_BLOCK, axis=2)
        dist_own = t[:, None] - (c * MOBA_BLOCK + jnp.arange(MOBA_BLOCK))[None, :]
        logit_own = jnp.einsum('bhqd,bhkd->bhqk', qb, k_own).astype(jnp.float32) * scale + tbl[:, rel_bucket(dist_own)]
        mask_own = jnp.broadcast_to(dist_own >= 0, logit_own.shape)
        if n_top > 0:
            gate = jnp.einsum('bhqd,bhnd->bhqn', qb.astype(jnp.float32), k_mean)
            gate = jnp.where(jnp.arange(nb) < c, gate, NEG_INF)
            _, idx = lax.top_k(gate, n_top)
            sel_ok = idx < c
            n_k = n_top * MOBA_BLOCK
            ks = k_blocks[bi, hi, idx].reshape(B, H, Qb, n_k, hd)
            vs = v_blocks[bi, hi, idx].reshape(B, H, Qb, n_k, hd)
            pos = (idx[..., None] * MOBA_BLOCK + jnp.arange(MOBA_BLOCK)).reshape(B, H, Qb, n_k)
            dist_sel = t[None, None, :, None] - pos
            logit_sel = jnp.einsum('bhqd,bhqkd->bhqk', qb, ks).astype(jnp.float32) * scale + tbl[hi, rel_bucket(dist_sel)]
            mask_sel = jnp.repeat(sel_ok, MOBA_BLOCK, axis=-1)
            p, _ = masked_softmax(jnp.concatenate([logit_sel, logit_own], axis=-1),
                                  jnp.concatenate([mask_sel, mask_own], axis=-1))
            p = p.astype(v.dtype)
            return (jnp.einsum('bhqk,bhqkd->bhqd', p[..., :n_k], vs)
                    + jnp.einsum('bhqk,bhkd->bhqd', p[..., n_k:], v_own))
        p, _ = masked_softmax(logit_own, mask_own)
        return jnp.einsum('bhqk,bhkd->bhqd', p.astype(v.dtype), v_own)

    out = lax.map(block_step, jnp.arange(S // Qb))
    return out.transpose(1, 0, 3, 2, 4).reshape(B, S, H * hd)


def dilated_group(q, k, v, tbl, window, dilation):
    B, S, h, hd = q.shape
    scale = hd ** -0.5
    L = S // dilation
    wb = window // dilation
    n_blk = -(-L // wb)
    Lp = n_blk * wb

    def to_sub(t, front):
        t = t.reshape(B, L, dilation, h, hd).transpose(0, 2, 3, 1, 4)
        return jnp.pad(t, ((0, 0), (0, 0), (0, 0), (front, Lp - L), (0, 0)))

    def band(t):
        tp = to_sub(t, wb)
        prev = tp[:, :, :, :Lp].reshape(B, dilation, h, n_blk, wb, hd)
        cur = tp[:, :, :, wb:].reshape(B, dilation, h, n_blk, wb, hd)
        return jnp.concatenate([prev, cur], axis=4)

    qs = to_sub(q, 0).reshape(B, dilation, h, n_blk, wb, hd)
    kb, vb = band(k), band(v)
    qa, ka = np.arange(wb), np.arange(2 * wb)
    delta = wb + qa[:, None] - ka[None, :]
    key_idx = np.arange(n_blk)[:, None] * wb - wb + ka[None, :]
    mask = jnp.asarray(((delta >= 0) & (delta <= wb))[None] & (key_idx >= 0)[:, None, :])
    bias = tbl[:, rel_bucket(jnp.asarray(delta * dilation))]
    logits = jnp.einsum('bdhnqc,bdhnkc->bdhnqk', qs, kb).astype(jnp.float32) * scale + bias[None, None, :, None]
    p, lse = masked_softmax(logits, mask)
    o = jnp.einsum('bdhnqk,bdhnkc->bdhnqc', p.astype(v.dtype), vb)
    o = o.reshape(B, dilation, h, Lp, hd)[:, :, :, :L].transpose(0, 3, 1, 2, 4).reshape(B, S, h, hd)
    lse = lse.reshape(B, dilation, h, Lp)[:, :, :, :L].transpose(0, 3, 1, 2).reshape(B, S, h)
    return o, lse


def dilated_mixer(q, k, v, bias_tbl):
    B, S, _ = q.shape
    h, hd = DIL_HEADS_PER_GROUP, HEAD_DIM
    outs, lses = [], []
    for g, (window, dilation) in enumerate(DIL_PATTERNS):
        sl = slice(g * h * hd, (g + 1) * h * hd)
        o, lse = dilated_group(q[..., sl].reshape(B, S, h, hd), k[..., sl].reshape(B, S, h, hd),
                               v[..., sl].reshape(B, S, h, hd), bias_tbl[:, g * h:(g + 1) * h].T,
                               window, dilation)
        outs.append(o)
        lses.append(lse)
    alpha = jax.nn.softmax(jnp.stack(lses, axis=0), axis=0)
    o = jnp.sum(alpha[..., None].astype(q.dtype) * jnp.stack(outs, axis=0), axis=0)
    return o.reshape(B, S, h * hd)


def hybrid_mixer(h, w_in, rel_bias, pe_k, pe_v, phi_k1, phi_k2, phi_v1, phi_v2, w_up_a, w_up_b, w_up_c, w_o):
    B, S, D = h.shape
    cols = split_columns(h @ w_in)
    bias_a = rel_bias[:, :NSA_HEADS]
    bias_b = rel_bias[:, NSA_HEADS:NSA_HEADS + MOBA_HEADS]
    bias_c = rel_bias[:, NSA_HEADS + MOBA_HEADS:]
    y_a = nsa_mixer(cols['nsa_q'], cols['nsa_k_cmp'], cols['nsa_v_cmp'], cols['nsa_k_sel'], cols['nsa_v_sel'],
                    cols['nsa_k_win'], cols['nsa_v_win'], cols['nsa_gate'], bias_a,
                    pe_k, pe_v, phi_k1, phi_k2, phi_v1, phi_v2)
    y_b = moba_mixer(cols['moba_q'], cols['moba_k'], cols['moba_v'], bias_b)
    y_c = dilated_mixer(cols['dil_q'], cols['dil_k'], cols['dil_v'], bias_c)
    gates = jax.nn.sigmoid(cols['merge_gate']).reshape(B, S, N_BRANCHES, D)
    merged = gates[:, :, 0] * (y_a @ w_up_a) + gates[:, :, 1] * (y_b @ w_up_b) + gates[:, :, 2] * (y_c @ w_up_c)
    return merged @ w_o


def setup_inputs(seed: int = 0) -> dict:
    key = jax.random.key(seed)
    ks = jax.random.split(key, 24)
    f32 = jnp.float32
    nrm = lambda k, shape, fan_in: jax.random.normal(k, shape, f32) * fan_in ** -0.5
    gain = lambda k, shape: 1.0 + 0.01 * jax.random.normal(k, shape, f32)
    L_HD = NSA_CMP_BLOCK * HEAD_DIM
    return {
        'x': jax.random.normal(ks[0], (BATCH, SEQ, D_MODEL), f32),
        'rel_bias': 0.1 * jax.random.normal(ks[1], (REL_BUCKETS, N_HEADS_TOTAL), f32),
        'ffn1_norm': gain(ks[2], (DEPTH, D_MODEL)),
        'ffn1_w_gate': nrm(ks[3], (DEPTH, D_MODEL, D_FF), D_MODEL),
        'ffn1_w_up': nrm(ks[4], (DEPTH, D_MODEL, D_FF), D_MODEL),
        'ffn1_w_down': nrm(ks[5], (DEPTH, D_FF, D_MODEL), D_FF),
        'mix_norm': gain(ks[6], (DEPTH, D_MODEL)),
        'w_in': nrm(ks[7], (DEPTH, D_MODEL, D_IN), D_MODEL),
        'nsa_pe_k': 0.02 * jax.random.normal(ks[8], (DEPTH, NSA_CMP_BLOCK, HEAD_DIM), f32),
        'nsa_pe_v': 0.02 * jax.random.normal(ks[9], (DEPTH, NSA_CMP_BLOCK, HEAD_DIM), f32),
        'nsa_phi_k1': nrm(ks[10], (DEPTH, L_HD, NSA_CMP_HIDDEN), L_HD),
        'nsa_phi_k2': nrm(ks[11], (DEPTH, NSA_CMP_HIDDEN, HEAD_DIM), NSA_CMP_HIDDEN),
        'nsa_phi_v1': nrm(ks[12], (DEPTH, L_HD, NSA_CMP_HIDDEN), L_HD),
        'nsa_phi_v2': nrm(ks[13], (DEPTH, NSA_CMP_HIDDEN, HEAD_DIM), NSA_CMP_HIDDEN),
        'w_up_a': nrm(ks[14], (DEPTH, NSA_HEADS * HEAD_DIM, D_MODEL), NSA_HEADS * HEAD_DIM),
        'w_up_b': nrm(ks[15], (DEPTH, MOBA_HEADS * HEAD_DIM, D_MODEL), MOBA_HEADS * HEAD_DIM),
        'w_up_c': nrm(ks[16], (DEPTH, DIL_HEADS_PER_GROUP * HEAD_DIM, D_MODEL), DIL_HEADS_PER_GROUP * HEAD_DIM),
        'w_o': nrm(ks[17], (DEPTH, D_MODEL, D_MODEL), D_MODEL),
        'ffn2_norm': gain(ks[18], (DEPTH, D_MODEL)),
        'ffn2_w_gate': nrm(ks[19], (DEPTH, D_MODEL, D_FF), D_MODEL),
        'ffn2_w_up': nrm(ks[20], (DEPTH, D_MODEL, D_FF), D_MODEL),
        'ffn2_w_down': nrm(ks[21], (DEPTH, D_FF, D_MODEL), D_FF),
        'final_norm': gain(ks[22], (D_MODEL,)),
    }


def reference(x, rel_bias, ffn1_norm, ffn1_w_gate, ffn1_w_up, ffn1_w_down, mix_norm, w_in,
              nsa_pe_k, nsa_pe_v, nsa_phi_k1, nsa_phi_k2, nsa_phi_v1, nsa_phi_v2,
              w_up_a, w_up_b, w_up_c, w_o, ffn2_norm, ffn2_w_gate, ffn2_w_up, ffn2_w_down, final_norm):
    for l in range(DEPTH):
        x = x + 0.5 * swiglu(rms_norm(x, ffn1_norm[l]), ffn1_w_gate[l], ffn1_w_up[l], ffn1_w_down[l])
        x = x + hybrid_mixer(rms_norm(x, mix_norm[l]), w_in[l], rel_bias,
                             nsa_pe_k[l], nsa_pe_v[l], nsa_phi_k1[l], nsa_phi_k2[l], nsa_phi_v1[l], nsa_phi_v2[l],
                             w_up_a[l], w_up_b[l], w_up_c[l], w_o[l])
        x = x + 0.5 * swiglu(rms_norm(x, ffn2_norm[l]), ffn2_w_gate[l], ffn2_w_up[l], ffn2_w_down[l])
    return rms_norm(x, final_norm)
```

```python
import functools
import math

import jax
import jax.numpy as jnp
import numpy as np
from jax import lax
from jax.experimental import pallas as pl
from jax.experimental.pallas import tpu as pltpu

HEAD_DIM = 64
NSA_HEADS = 6
NSA_KV_HEADS = 2
NSA_REP = NSA_HEADS // NSA_KV_HEADS
NSA_CMP_BLOCK = 32
NSA_CMP_STRIDE = 16
NSA_SEL_BLOCK = 64
NSA_N_SEL = 16
NSA_WINDOW = 512
NSA_CMP_HIDDEN = 256
MOBA_HEADS = 4
MOBA_BLOCK = 256
MOBA_TOPK = 3
DIL_PATTERNS = ((128, 1), (512, 4), (2048, 16))
DIL_HEADS_PER_GROUP = 2
DIL_WB = 128
N_BRANCHES = 3
REL_BUCKETS = 32
REL_MAX_EXACT = 16
REL_MAX_DIST = 2048
NORM_EPS = 1e-6
FORCE_SCORE = 1e4
N_HEADS_TOTAL = NSA_HEADS + MOBA_HEADS + DIL_HEADS_PER_GROUP * len(DIL_PATTERNS)

LANES = 128
SLOT = LANES
TQ = 128
TK = 128
NEG = -1e30
M_INIT = -1e29
MXU_DTYPE = jnp.bfloat16
VMEM_LIMIT = 56 * 1024 * 1024
F32 = jnp.float32


def _cparams(n_grid, vmem=VMEM_LIMIT):
    return pltpu.CompilerParams(dimension_semantics=("arbitrary",) * n_grid, vmem_limit_bytes=vmem)


def _const_spec(shape):
    nd = len(shape)
    return pl.BlockSpec(shape, lambda *_: (0,) * nd)


def _dot(a, b):
    return jnp.dot(a, b, preferred_element_type=F32)


def _dot_nt(a, b, precision=None):
    return lax.dot_general(a, b, (((1,), (1,)), ((), ())), preferred_element_type=F32, precision=precision)


def _rms(x, gain):
    return x * lax.rsqrt(jnp.mean(x * x, axis=-1, keepdims=True) + NORM_EPS) * gain


def _ffn_kernel(x_ref, g_ref, wg_ref, wu_ref, wd_ref, *rest, ff_chunk, final_norm):
    if final_norm:
        fg_ref, o_ref, acc_ref = rest
    else:
        o_ref, acc_ref = rest
    x = x_ref[...]
    h = _rms(x, g_ref[...]).astype(MXU_DTYPE)
    d_ff = wg_ref.shape[1]
    for c in range(d_ff // ff_chunk):
        sl = slice(c * ff_chunk, (c + 1) * ff_chunk)
        a = _dot(h, wg_ref[:, sl])
        u = _dot(h, wu_ref[:, sl])
        z = (a * jax.nn.sigmoid(a) * u).astype(MXU_DTYPE)
        part = _dot(z, wd_ref[sl, :])
        if c == 0:
            acc_ref[...] = part
        else:
            acc_ref[...] += part
    y = x + 0.5 * acc_ref[...]
    if final_norm:
        y = _rms(y, fg_ref[...])
    o_ref[...] = y


def _ffn(x2, gain, wg, wu, wd, final_gain=None, tm=512):
    t, d = x2.shape
    d_ff = wg.shape[1]
    ff_chunk = 256 if d_ff % 256 == 0 else d_ff
    tm = min(tm, t)
    final = final_gain is not None
    in_specs = [pl.BlockSpec((tm, d), lambda i: (i, 0)), _const_spec((1, d)),
                _const_spec((d, d_ff)), _const_spec((d, d_ff)), _const_spec((d_ff, d))]
    args = [x2, gain.reshape(1, d), wg, wu, wd]
    if final:
        in_specs.append(_const_spec((1, d)))
        args.append(final_gain.reshape(1, d))
    return pl.pallas_call(
        functools.partial(_ffn_kernel, ff_chunk=ff_chunk, final_norm=final),
        out_shape=jax.ShapeDtypeStruct((t, d), F32),
        grid=(t // tm,),
        in_specs=in_specs,
        out_specs=pl.BlockSpec((tm, d), lambda i: (i, 0)),
        scratch_shapes=[pltpu.VMEM((tm, d), F32)],
        compiler_params=_cparams(1),
        name="ffn_swiglu",
    )(*args)


def _bias_tile_kernel(tbl_ref, o_ref, *, heads, width, row_step, col_mult, offset, dist_mult, lo, hi, n_cols):
    d = pl.program_id(0)
    i = lax.broadcasted_iota(jnp.int32, (TQ, width), 0)
    j = lax.broadcasted_iota(jnp.int32, (TQ, width), 1)
    raw = row_step * d + i - col_mult * j + offset
    valid = (raw >= lo) & (raw <= hi) & (j < n_cols)
    n = jnp.maximum(raw * dist_mult, 0)
    nf = jnp.maximum(n, REL_MAX_EXACT).astype(F32)
    large = REL_MAX_EXACT + (jnp.log(nf / REL_MAX_EXACT) / math.log(REL_MAX_DIST / REL_MAX_EXACT)
                             * (REL_BUCKETS - REL_MAX_EXACT)).astype(jnp.int32)
    large = jnp.minimum(large, REL_BUCKETS - 1)
    bucket = jnp.where(n < REL_MAX_EXACT, n, large)
    for hh, head in enumerate(heads):
        val = jnp.zeros((TQ, width), F32)
        for k in range(REL_BUCKETS):
            val = jnp.where(bucket == k, tbl_ref[k, head], val)
        o_ref[0, hh] = jnp.where(valid, val, NEG)


def _bias_tiles(rel_bias, *, n_tiles, heads, width=TK, row_step=TQ, col_mult=1, offset=0, dist_mult=1,
                lo=0, hi=2 ** 30, n_cols=None, name):
    n_cols = width if n_cols is None else n_cols
    nh = len(heads)
    return pl.pallas_call(
        functools.partial(_bias_tile_kernel, heads=tuple(heads), width=width, row_step=row_step,
                          col_mult=col_mult, offset=offset, dist_mult=dist_mult, lo=lo, hi=hi, n_cols=n_cols),
        out_shape=jax.ShapeDtypeStruct((n_tiles, nh, TQ, width), F32),
        grid=(n_tiles,),
        in_specs=[pl.BlockSpec(memory_space=pltpu.SMEM)],
        out_specs=pl.BlockSpec((1, nh, TQ, width), lambda d: (d, 0, 0, 0)),
        compiler_params=_cparams(1),
        name=name,
    )(rel_bias)


_PROJ_SEGS = (
    ("nsa_q", NSA_HEADS * SLOT, MXU_DTYPE),
    ("nsa_kc", LANES, F32),
    ("nsa_vc", LANES, F32),
    ("nsa_ks", NSA_KV_HEADS * SLOT, MXU_DTYPE),
    ("nsa_vs", NSA_KV_HEADS * SLOT, MXU_DTYPE),
    ("nsa_kw", NSA_KV_HEADS * SLOT, MXU_DTYPE),
    ("nsa_vw", NSA_KV_HEADS * SLOT, MXU_DTYPE),
    ("nsa_gate", NSA_KV_HEADS * SLOT, F32),
    ("moba_q", MOBA_HEADS * SLOT, MXU_DTYPE),
    ("moba_k", MOBA_HEADS * SLOT, MXU_DTYPE),
    ("moba_v", MOBA_HEADS * SLOT, MXU_DTYPE),
    ("dil_q0", LANES, MXU_DTYPE), ("dil_q1", LANES, MXU_DTYPE), ("dil_q2", LANES, MXU_DTYPE),
    ("dil_k0", LANES, MXU_DTYPE), ("dil_k1", LANES, MXU_DTYPE), ("dil_k2", LANES, MXU_DTYPE),
    ("dil_v0", LANES, MXU_DTYPE), ("dil_v1", LANES, MXU_DTYPE), ("dil_v2", LANES, MXU_DTYPE),
)
_PROJ_OFFS = np.concatenate([[0], np.cumsum([w for _, w, _ in _PROJ_SEGS])])
_PROJ_COLS = int(_PROJ_OFFS[-1])


def _proj_column_map():
    hd = HEAD_DIM
    kvw = NSA_KV_HEADS * hd
    off = {}
    o = 0
    for name, w in (("nsa_q", NSA_HEADS * hd), ("nsa_k_cmp", kvw), ("nsa_v_cmp", kvw), ("nsa_k_sel", kvw),
                    ("nsa_v_sel", kvw), ("nsa_k_win", kvw), ("nsa_v_win", kvw), ("nsa_gate", NSA_HEADS * 3),
                    ("moba_q", MOBA_HEADS * hd), ("moba_k", MOBA_HEADS * hd), ("moba_v", MOBA_HEADS * hd),
                    ("dil_q", 6 * hd), ("dil_k", 6 * hd), ("dil_v", 6 * hd)):
        off[name] = o
        o += w
    merge_off = o
    src = np.full((_PROJ_COLS,), -1, np.int64)
    scale = np.ones((_PROJ_COLS,), np.float32)
    seg_off = {name: int(_PROJ_OFFS[i]) for i, (name, _, _) in enumerate(_PROJ_SEGS)}
    qk_scale = hd ** -0.5

    def put(seg, slot, src_start, n, s=1.0, lane0=0):
        base = seg_off[seg] + slot * SLOT + lane0
        src[base:base + n] = np.arange(src_start, src_start + n)
        scale[base:base + n] = s

    for h in range(NSA_HEADS):
        put("nsa_q", h, off["nsa_q"] + h * hd, hd, qk_scale)
    put("nsa_kc", 0, off["nsa_k_cmp"], kvw)
    put("nsa_vc", 0, off["nsa_v_cmp"], kvw)
    for g in range(NSA_KV_HEADS):
        put("nsa_ks", g, off["nsa_k_sel"] + g * hd, hd)
        put("nsa_vs", g, off["nsa_v_sel"] + g * hd, hd)
        put("nsa_kw", g, off["nsa_k_win"] + g * hd, hd)
        put("nsa_vw", g, off["nsa_v_win"] + g * hd, hd)
        put("nsa_gate", g, off["nsa_gate"] + g * NSA_REP * 3, NSA_REP * 3)
    for h in range(MOBA_HEADS):
        put("moba_q", h, off["moba_q"] + h * hd, hd, qk_scale)
        put("moba_k", h, off["moba_k"] + h * hd, hd)
        put("moba_v", h, off["moba_v"] + h * hd, hd)
    for g in range(len(DIL_PATTERNS)):
        put(f"dil_q{g}", 0, off["dil_q"] + g * 2 * hd, 2 * hd, qk_scale)
        put(f"dil_k{g}", 0, off["dil_k"] + g * 2 * hd, 2 * hd)
        put(f"dil_v{g}", 0, off["dil_v"] + g * 2 * hd, 2 * hd)
    return src, scale, merge_off


def _proj_kernel(x_ref, g_ref, w_ref, *out_refs, seq, tm):
    outs = dict(zip([n for n, _, _ in _PROJ_SEGS] + ["moba_kmean"], out_refs))
    h = _rms(x_ref[...], g_ref[...]).astype(MXU_DTYPE)
    pos0 = (pl.program_id(0) * tm) % seq
    for si, (name, width, dtype) in enumerate(_PROJ_SEGS):
        c0 = int(_PROJ_OFFS[si])
        y = _dot(h, w_ref[:, c0:c0 + width])
        if name == "moba_k":
            nblk = tm // MOBA_BLOCK
            outs["moba_kmean"][0] = jnp.mean(y.reshape(nblk, MOBA_BLOCK, width), axis=1)
        if name in ("nsa_ks", "moba_k"):
            blk = NSA_SEL_BLOCK if name == "nsa_ks" else MOBA_BLOCK
            row = lax.broadcasted_iota(jnp.int32, (tm, width), 0)
            lane = lax.broadcasted_iota(jnp.int32, (tm, width), 1) & (SLOT - 1)
            hit = (lane - HEAD_DIM) == jnp.right_shift(pos0 + row, int(math.log2(blk)))
            y = jnp.where(hit, 1.0, y)
        if name == "nsa_gate":
            y = jax.nn.sigmoid(y)
        outs[name][...] = y.astype(dtype)


def _proj(x2, gain, w_attn, seq, tm=512):
    t, d = x2.shape
    tm = min(tm, seq)
    assert tm % MOBA_BLOCK == 0 and seq % tm == 0
    nblk = tm // MOBA_BLOCK
    out_shape = [jax.ShapeDtypeStruct((t, w), dt) for _, w, dt in _PROJ_SEGS]
    out_specs = [pl.BlockSpec((tm, w), lambda i: (i, 0)) for _, w, _ in _PROJ_SEGS]
    kw = MOBA_HEADS * SLOT
    out_shape.append(jax.ShapeDtypeStruct((t // tm, nblk, kw), F32))
    out_specs.append(pl.BlockSpec((1, nblk, kw), lambda i: (i, 0, 0)))
    outs = pl.pallas_call(
        functools.partial(_proj_kernel, seq=seq, tm=tm),
        out_shape=out_shape,
        grid=(t // tm,),
        in_specs=[pl.BlockSpec((tm, d), lambda i: (i, 0)), _const_spec((1, d)), _const_spec((d, _PROJ_COLS))],
        out_specs=out_specs,
        compiler_params=_cparams(1),
        name="mixer_in_proj",
    )(x2, gain.reshape(1, d), w_attn)
    res = {name: o for (name, _, _), o in zip(_PROJ_SEGS, outs)}
    res["moba_kmean"] = outs[-1]
    return res


def _gelu_tanh(x):
    return 0.5 * x * (1.0 + jnp.tanh(math.sqrt(2.0 / math.pi) * (x + 0.044715 * (x * x * x))))


def _compress_kernel(k_ref, v_ref, pek_ref, pev_ref, wk1_ref, wv1_ref, wk2_ref, wv2_ref, kc_ref, vc_ref):
    nr = k_ref.shape[1]
    hid_w = NSA_CMP_HIDDEN
    for x_ref, pe_ref, w1_ref, w2_ref, o_ref in ((k_ref, pek_ref, wk1_ref, wk2_ref, kc_ref),
                                                 (v_ref, pev_ref, wv1_ref, wv2_ref, vc_ref)):
        r = x_ref[0]
        lo = _dot((r + pe_ref[0:1, :]).astype(MXU_DTYPE), w1_ref[0])
        hi = _dot((r + pe_ref[1:2, :]).astype(MXU_DTYPE), w1_ref[1])
        hid = lo + pltpu.roll(hi, nr - 1, 0)
        act = _gelu_tanh(hid).astype(MXU_DTYPE)
        for g in range(NSA_KV_HEADS):
            o_ref[0, g] = _dot(act[:, g * hid_w:(g + 1) * hid_w], w2_ref[...]).astype(o_ref.dtype)


def _compress(kc_in, vc_in, pek, pev, wk1, wv1, wk2, wv2, batch, seq):
    nr = seq // NSA_CMP_STRIDE
    rw = NSA_CMP_STRIDE * LANES
    kin = kc_in.reshape(batch, nr, rw)
    vin = vc_in.reshape(batch, nr, rw)
    hw = NSA_KV_HEADS * NSA_CMP_HIDDEN
    out = jax.ShapeDtypeStruct((batch, NSA_KV_HEADS, nr, SLOT), MXU_DTYPE)
    in_blk = pl.BlockSpec((1, nr, rw), lambda b: (b, 0, 0))
    out_blk = pl.BlockSpec((1, NSA_KV_HEADS, nr, SLOT), lambda b: (b, 0, 0, 0))
    return pl.pallas_call(
        _compress_kernel,
        out_shape=[out, out],
        grid=(batch,),
        in_specs=[in_blk, in_blk, _const_spec((2, rw)), _const_spec((2, rw)),
                  _const_spec((2, rw, hw)), _const_spec((2, rw, hw)),
                  _const_spec((NSA_CMP_HIDDEN, SLOT)), _const_spec((NSA_CMP_HIDDEN, SLOT))],
        out_specs=[out_blk, out_blk],
        compiler_params=_cparams(1),
        name="nsa_compress",
    )(kin, vin, pek, pev, wk1, wv1, wk2, wv2)


def _rank_before(score):
    n = score.shape[0]
    idx = lax.broadcasted_iota(jnp.int32, score.shape, 0)
    rank = jnp.zeros(score.shape, jnp.int32)
    for j in range(n):
        row = score[j:j + 1, :]
        ahead = (row > score) | ((row == score) & (idx > j))
        rank = rank + jnp.where(ahead, 1, 0)
    return rank


def _flash_tiles(q, k_ref, v_ref, t_ref, qt, lo, hi, n_stack):
    rows = q.shape[0]

    def body(kt, carry):
        m, l, acc = carry
        ks = pl.multiple_of(kt * TK, TK)
        k = k_ref[pl.ds(ks, TK), :]
        v = v_ref[pl.ds(ks, TK), :]
        s = _dot_nt(q, k) + t_ref[qt - kt].reshape(rows, TK)
        m_new = jnp.maximum(m, jnp.max(s, axis=-1, keepdims=True))
        alpha = jnp.exp(m - m_new)
        p = jnp.exp(s - m_new)
        l = alpha * l + jnp.sum(p, axis=-1, keepdims=True)
        acc = alpha * acc + _dot(p.astype(MXU_DTYPE), v)
        return m_new, l, acc

    init = (jnp.full((rows, 1), M_INIT, F32), jnp.zeros((rows, 1), F32), jnp.zeros((rows, SLOT), F32))
    m, l, acc = lax.fori_loop(lo, hi, body, init)
    return acc / jnp.maximum(l, 1e-30)


def _block_mask_lanes(z_ref, sel_t, tq):
    n = sel_t.shape[0]
    z_ref[...] = jnp.zeros(z_ref.shape, F32)
    z_ref[HEAD_DIM:HEAD_DIM + n, :] = jnp.where(sel_t, 0.0, NEG)
    return z_ref[...].T


def _nsa_kernel(q_ref, gate_ref, kc_ref, vc_ref, ks_ref, vs_ref, kw_ref, vw_ref, c2s_ref,
                tcmp_ref, tsel_ref, twin_ref, o_ref, z_ref, *, n_slc):
    qt = pl.program_id(2)
    rep = NSA_REP
    q_all = q_ref[0]
    q = jnp.concatenate([q_all[:, r * SLOT:(r + 1) * SLOT] for r in range(rep)], axis=0)
    rows = rep * TQ

    n_cmp_pad = kc_ref.shape[2]
    sc = _dot_nt(q, kc_ref[0, 0]) + tcmp_ref[0].reshape(rows, n_cmp_pad)
    mc = jnp.maximum(jnp.max(sc, axis=-1, keepdims=True), M_INIT)
    pc = jnp.exp(sc - mc)
    pc = pc / jnp.maximum(jnp.sum(pc, axis=-1, keepdims=True), 1e-30)
    o_c = _dot(pc.astype(MXU_DTYPE), vc_ref[0, 0])

    p_sum = pc[0:TQ]
    for r in range(1, rep):
        p_sum = p_sum + pc[r * TQ:(r + 1) * TQ]
    imp = jnp.dot(p_sum, c2s_ref[...], preferred_element_type=F32, precision=lax.Precision.HIGHEST)
    imp_t = imp.T[0:n_slc]
    blk = lax.broadcasted_iota(jnp.int32, (n_slc, TQ), 0)
    t_pos = qt * TQ + lax.broadcasted_iota(jnp.int32, (n_slc, TQ), 1)
    cur = jnp.right_shift(t_pos, int(math.log2(NSA_SEL_BLOCK)))
    forced = (blk == 0) | (blk == cur) | (blk == cur - 1)
    score = jnp.where(blk > cur, NEG, jnp.where(forced, FORCE_SCORE, imp_t))
    sel_t = _rank_before(score) < min(NSA_N_SEL, n_slc)
    zt = _block_mask_lanes(z_ref, sel_t, TQ)
    q_sel = (q.astype(F32) + jnp.concatenate([zt] * rep, axis=0)).astype(MXU_DTYPE)

    o_s = _flash_tiles(q_sel, ks_ref.at[0], vs_ref.at[0], tsel_ref, qt, 0, qt + 1, rep)
    n_win = twin_ref.shape[0]
    o_w = _flash_tiles(q, kw_ref.at[0], vw_ref.at[0], twin_ref, qt, jnp.maximum(qt - (n_win - 1), 0), qt + 1, rep)

    gates = gate_ref[0]
    for r in range(rep):
        sl = slice(r * TQ, (r + 1) * TQ)
        o_r = (gates[:, 3 * r:3 * r + 1] * o_c[sl] + gates[:, 3 * r + 1:3 * r + 2] * o_s[sl]
               + gates[:, 3 * r + 2:3 * r + 3] * o_w[sl])
        o_ref[0, :, r * SLOT:(r + 1) * SLOT] = o_r.astype(o_ref.dtype)


def _nsa(p, kc, vc, c2s, tcmp, tmain, twin, batch, seq):
    nq = seq // TQ
    n_slc = seq // NSA_SEL_BLOCK
    assert n_slc <= HEAD_DIM
    g_n, rep = NSA_KV_HEADS, NSA_REP
    n_cmp_pad = kc.shape[2]
    r3 = lambda a: a.reshape(batch, seq, a.shape[-1])
    kv_spec = pl.BlockSpec((1, seq, SLOT), lambda b, g, i: (b, 0, g))
    cmp_spec = pl.BlockSpec((1, 1, n_cmp_pad, SLOT), lambda b, g, i: (b, g, 0, 0))
    n_win = twin.shape[0]
    return pl.pallas_call(
        functools.partial(_nsa_kernel, n_slc=n_slc),
        out_shape=jax.ShapeDtypeStruct((batch, seq, NSA_HEADS * SLOT), MXU_DTYPE),
        grid=(batch, g_n, nq),
        in_specs=[
            pl.BlockSpec((1, TQ, rep * SLOT), lambda b, g, i: (b, i, g)),
            pl.BlockSpec((1, TQ, SLOT), lambda b, g, i: (b, i, g)),
            cmp_spec, cmp_spec, kv_spec, kv_spec, kv_spec, kv_spec,
            _const_spec(c2s.shape),
            pl.BlockSpec((1, rep, TQ, n_cmp_pad), lambda b, g, i: (i, g, 0, 0)),
            pl.BlockSpec((nq, rep, TQ, TK), lambda b, g, i: (0, g, 0, 0)),
            pl.BlockSpec((n_win, rep, TQ, TK), lambda b, g, i: (0, g, 0, 0)),
        ],
        out_specs=pl.BlockSpec((1, TQ, rep * SLOT), lambda b, g, i: (b, i, g)),
        scratch_shapes=[pltpu.VMEM((LANES, TQ), F32)],
        compiler_params=_cparams(3),
        name="nsa_attention",
    )(r3(p["nsa_q"]), r3(p["nsa_gate"]), kc, vc, r3(p["nsa_ks"]), r3(p["nsa_vs"]), r3(p["nsa_kw"]),
      r3(p["nsa_vw"]), c2s, tcmp, tmain, twin)


def _moba_kernel(q_ref, k_ref, v_ref, km_ref, t_ref, o_ref, z_ref, *, nb):
    qt = pl.program_id(2)
    q = q_ref[0]
    own = (qt * TQ) // MOBA_BLOCK
    gate_t = _dot_nt(km_ref[0], q.astype(F32), precision=lax.Precision.HIGHEST)
    n_i = lax.broadcasted_iota(jnp.int32, (nb, TQ), 0)
    past = n_i < own
    score = jnp.where(past, gate_t, NEG)
    sel_t = ((_rank_before(score) < min(MOBA_TOPK, nb - 1)) & past) | (n_i == own)
    zt = _block_mask_lanes(z_ref, sel_t, TQ)
    q_sel = (q.astype(F32) + zt).astype(MXU_DTYPE)
    o = _flash_tiles(q_sel, k_ref.at[0], v_ref.at[0], t_ref, qt, 0, qt + 1, 1)
    o_ref[0] = o.astype(o_ref.dtype)


def _moba(p, kmean, tmain, batch, seq):
    nq = seq // TQ
    nb = seq // MOBA_BLOCK
    assert seq % MOBA_BLOCK == 0 and nb <= HEAD_DIM and MOBA_BLOCK % TQ == 0
    r3 = lambda a: a.reshape(batch, seq, a.shape[-1])
    kv_spec = pl.BlockSpec((1, seq, SLOT), lambda b, h, i: (b, 0, h))
    q_spec = pl.BlockSpec((1, TQ, SLOT), lambda b, h, i: (b, i, h))
    return pl.pallas_call(
        functools.partial(_moba_kernel, nb=nb),
        out_shape=jax.ShapeDtypeStruct((batch, seq, MOBA_HEADS * SLOT), MXU_DTYPE),
        grid=(batch, MOBA_HEADS, nq),
        in_specs=[q_spec, kv_spec, kv_spec,
                  pl.BlockSpec((1, nb, SLOT), lambda b, h, i: (b, 0, h)),
                  pl.BlockSpec((nq, 1, TQ, TK), lambda b, h, i: (0, NSA_HEADS + h, 0, 0))],
        out_specs=q_spec,
        scratch_shapes=[pltpu.VMEM((LANES, TQ), F32)],
        compiler_params=_cparams(3),
        name="moba_attention",
    )(r3(p["moba_q"]), r3(p["moba_k"]), r3(p["moba_v"]), kmean.reshape(batch, nb, MOBA_HEADS * SLOT), tmain)


def _dilated_kernel(q_ref, k_ref, v_ref, t_ref, o_ref, lse_ref):
    n = pl.program_id(2)
    q = q_ref[0]
    cur = pl.multiple_of(n * DIL_WB, DIL_WB)
    prev = pl.multiple_of(jnp.maximum(n - 1, 0) * DIL_WB, DIL_WB)
    k_cur, v_cur = k_ref[0, pl.ds(cur, DIL_WB), :], v_ref[0, pl.ds(cur, DIL_WB), :]
    k_prev, v_prev = k_ref[0, pl.ds(prev, DIL_WB), :], v_ref[0, pl.ds(prev, DIL_WB), :]
    no_prev = jnp.where(n == 0, NEG, 0.0)
    lane = lax.broadcasted_iota(jnp.int32, (DIL_WB, LANES), 1)
    o_heads, lse_heads = [], []
    for h in range(DIL_HEADS_PER_GROUP):
        in_head = (lane >= h * HEAD_DIM) & (lane < (h + 1) * HEAD_DIM)
        qh = jnp.where(in_head, q, jnp.zeros_like(q))
        s_cur = _dot_nt(qh, k_cur) + t_ref[0, h]
        s_prev = _dot_nt(qh, k_prev) + t_ref[1, h] + no_prev
        m = jnp.maximum(jnp.max(s_cur, axis=-1, keepdims=True), jnp.max(s_prev, axis=-1, keepdims=True))
        p_cur = jnp.exp(s_cur - m)
        p_prev = jnp.exp(s_prev - m)
        l = jnp.maximum(jnp.sum(p_cur, axis=-1, keepdims=True) + jnp.sum(p_prev, axis=-1, keepdims=True), 1e-30)
        o_heads.append((_dot(p_cur.astype(MXU_DTYPE), v_cur) + _dot(p_prev.astype(MXU_DTYPE), v_prev)) / l)
        lse_heads.append(m + jnp.log(l))
    first = lane < HEAD_DIM
    o_ref[0] = jnp.where(first, o_heads[0], o_heads[1])
    lse_ref[0] = jnp.where(first, lse_heads[0], lse_heads[1])


def _dilated(q, k, v, tdil, dilation, batch, seq):
    ln = seq // dilation
    assert ln % DIL_WB == 0
    view = lambda a: a.reshape(batch, ln, dilation * LANES)
    q_spec = pl.BlockSpec((1, DIL_WB, LANES), lambda b, r, n: (b, n, r))
    kv_spec = pl.BlockSpec((1, ln, LANES), lambda b, r, n: (b, 0, r))
    out = jax.ShapeDtypeStruct((batch, ln, dilation * LANES), F32)
    o, lse = pl.pallas_call(
        _dilated_kernel,
        out_shape=[out, out],
        grid=(batch, dilation, ln // DIL_WB),
        in_specs=[q_spec, kv_spec, kv_spec, _const_spec(tdil.shape)],
        out_specs=[q_spec, q_spec],
        compiler_params=_cparams(3),
        name=f"dilated_attention_d{dilation}",
    )(view(q), view(k), view(v), tdil)
    return o.reshape(batch * seq, LANES), lse.reshape(batch * seq, LANES)


def _mixer_out_kernel(x_ref, g_ref, ya_ref, yb_ref, o0_ref, o1_ref, o2_ref, l0_ref, l1_ref, l2_ref,
                      wmg_ref, wa_ref, wb_ref, wc_ref, wo_ref, out_ref):
    x = x_ref[...]
    d = x.shape[1]
    h = _rms(x, g_ref[...]).astype(MXU_DTYPE)
    l0, l1, l2 = l0_ref[...], l1_ref[...], l2_ref[...]
    mx = jnp.maximum(jnp.maximum(l0, l1), l2)
    e0, e1, e2 = jnp.exp(l0 - mx), jnp.exp(l1 - mx), jnp.exp(l2 - mx)
    y_c = (e0 * o0_ref[...] + e1 * o1_ref[...] + e2 * o2_ref[...]) / (e0 + e1 + e2)
    merged = jax.nn.sigmoid(_dot(h, wmg_ref[:, 0:d])) * _dot(ya_ref[...], wa_ref[...])
    merged += jax.nn.sigmoid(_dot(h, wmg_ref[:, d:2 * d])) * _dot(yb_ref[...], wb_ref[...])
    merged += jax.nn.sigmoid(_dot(h, wmg_ref[:, 2 * d:3 * d])) * _dot(y_c.astype(MXU_DTYPE), wc_ref[...])
    out_ref[...] = x + _dot(merged.astype(MXU_DTYPE), wo_ref[...])


def _mixer_out(x2, gain, ya, yb, dil, wmg, wa, wb, wc, wo, tm=512):
    t, d = x2.shape
    tm = min(tm, t)
    row = lambda w: pl.BlockSpec((tm, w), lambda i: (i, 0))
    (o0, l0), (o1, l1), (o2, l2) = dil
    return pl.pallas_call(
        _mixer_out_kernel,
        out_shape=jax.ShapeDtypeStruct((t, d), F32),
        grid=(t // tm,),
        in_specs=[row(d), _const_spec((1, d)), row(ya.shape[1]), row(yb.shape[1])] + [row(LANES)] * 6
                 + [_const_spec(w.shape) for w in (wmg, wa, wb, wc, wo)],
        out_specs=row(d),
        compiler_params=_cparams(1),
        name="mixer_out_proj",
    )(x2, gain.reshape(1, d), ya, yb, o0, o1, o2, l0, l1, l2, wmg, wa, wb, wc, wo)


def _slot_rows(w, n_slots):
    d = w.shape[1]
    w = w.reshape(n_slots, HEAD_DIM, d)
    return jnp.concatenate([w, jnp.zeros_like(w)], axis=1).reshape(n_slots * SLOT, d)


def _compress_weights(pe, w1, w2):
    g_n, hd, half = NSA_KV_HEADS, HEAD_DIM, NSA_CMP_STRIDE
    hid = w1.shape[1]
    w1r = w1.reshape(2, half, hd, hid)
    eye = jnp.eye(g_n, dtype=w1.dtype)
    w1p = jnp.einsum("plch,gk->plgckh", w1r, eye).reshape(2, half * g_n * hd, g_n * hid)
    pep = jnp.broadcast_to(pe.reshape(2, half, 1, hd), (2, half, g_n, hd)).reshape(2, half * g_n * hd)
    w2p = jnp.concatenate([w2, jnp.zeros_like(w2)], axis=1)
    return pep, w1p.astype(MXU_DTYPE), w2p.astype(MXU_DTYPE)


def _cmp_to_slc(seq, n_cmp_pad):
    n_cmp = (seq - NSA_CMP_BLOCK) // NSA_CMP_STRIDE + 1
    n_slc = seq // NSA_SEL_BLOCK
    c_start = np.arange(n_cmp_pad) * NSA_CMP_STRIDE
    s_start = np.arange(LANES) * NSA_SEL_BLOCK
    ov = ((c_start[:, None] < s_start[None, :] + NSA_SEL_BLOCK) & (c_start[:, None] + NSA_CMP_BLOCK > s_start[None, :])
          & (np.arange(n_cmp_pad)[:, None] < n_cmp) & (np.arange(LANES)[None, :] < n_slc))
    return jnp.asarray(ov, F32)


def kernel(x, rel_bias, ffn1_norm, ffn1_w_gate, ffn1_w_up, ffn1_w_down, mix_norm, w_in, nsa_pe_k, nsa_pe_v,
           nsa_phi_k1, nsa_phi_k2, nsa_phi_v1, nsa_phi_v2, w_up_a, w_up_b, w_up_c, w_o, ffn2_norm, ffn2_w_gate,
           ffn2_w_up, ffn2_w_down, final_norm):
    batch, seq, d = x.shape
    depth = w_in.shape[0]
    nq = seq // TQ
    assert seq % (TQ * DIL_PATTERNS[-1][1]) == 0 and all(w // dl == DIL_WB for w, dl in DIL_PATTERNS)
    bf = lambda a: a.astype(MXU_DTYPE)

    n_cmp = (seq - NSA_CMP_BLOCK) // NSA_CMP_STRIDE + 1
    n_cmp_pad = seq // NSA_CMP_STRIDE
    a_heads = list(range(NSA_HEADS))
    ab_heads = list(range(NSA_HEADS + MOBA_HEADS))
    t_main = _bias_tiles(rel_bias, n_tiles=nq, heads=ab_heads, name="bias_causal")
    t_win = _bias_tiles(rel_bias, n_tiles=min(NSA_WINDOW // TK + 1, nq), heads=a_heads, hi=NSA_WINDOW - 1,
                        name="bias_window")
    t_cmp = _bias_tiles(rel_bias, n_tiles=nq, heads=a_heads, width=n_cmp_pad, col_mult=NSA_CMP_STRIDE,
                        offset=-(NSA_CMP_BLOCK - 1), n_cols=n_cmp, name="bias_compressed")
    t_dil = []
    for g, (_, dilation) in enumerate(DIL_PATTERNS):
        h0 = NSA_HEADS + MOBA_HEADS + g * DIL_HEADS_PER_GROUP
        t_dil.append(_bias_tiles(rel_bias, n_tiles=2, heads=[h0, h0 + 1], dist_mult=dilation, hi=DIL_WB,
                                 name=f"bias_dilated_{g}"))
    c2s = _cmp_to_slc(seq, n_cmp_pad)

    src, scale, merge_off = _proj_column_map()
    x2 = x.reshape(batch * seq, d)
    for l in range(depth):
        last = l == depth - 1
        x2 = _ffn(x2, ffn1_norm[l], bf(ffn1_w_gate[l]), bf(ffn1_w_up[l]), bf(ffn1_w_down[l]))

        w_l = w_in[l]
        w_attn = bf(jnp.where(src[None, :] >= 0, w_l[:, np.maximum(src, 0)], 0.0) * scale[None, :])
        p = _proj(x2, mix_norm[l], w_attn, seq)
        pek, wk1, wk2 = _compress_weights(nsa_pe_k[l], nsa_phi_k1[l], nsa_phi_k2[l])
        pev, wv1, wv2 = _compress_weights(nsa_pe_v[l], nsa_phi_v1[l], nsa_phi_v2[l])
        kc, vc = _compress(p["nsa_kc"], p["nsa_vc"], pek, pev, wk1, wv1, wk2, wv2, batch, seq)
        ya = _nsa(p, kc, vc, c2s, t_cmp, t_main, t_win, batch, seq)
        yb = _moba(p, p["moba_kmean"], t_main, batch, seq)
        dil = [_dilated(p[f"dil_q{g}"], p[f"dil_k{g}"], p[f"dil_v{g}"], t_dil[g], dilation, batch, seq)
               for g, (_, dilation) in enumerate(DIL_PATTERNS)]
        x2 = _mixer_out(x2, mix_norm[l], ya.reshape(batch * seq, -1), yb.reshape(batch * seq, -1), dil,
                        bf(w_l[:, merge_off:]), bf(_slot_rows(w_up_a[l], NSA_HEADS)),
                        bf(_slot_rows(w_up_b[l], MOBA_HEADS)), bf(w_up_c[l]), bf(w_o[l]))

        x2 = _ffn(x2, ffn2_norm[l], bf(ffn2_w_gate[l]), bf(ffn2_w_up[l]), bf(ffn2_w_down[l]),
                  final_gain=final_norm if last else None)
    return x2.reshape(batch, seq, d)
```

```python
import functools
import math

import jax
import jax.numpy as jnp
import numpy as np
from jax import lax
from jax.experimental import pallas as pl
from jax.experimental.pallas import tpu as pltpu

HEAD_DIM = 64
NSA_HEADS = 6
NSA_KV_HEADS = 2
NSA_REP = NSA_HEADS // NSA_KV_HEADS
NSA_CMP_BLOCK = 32
NSA_CMP_STRIDE = 16
NSA_SEL_BLOCK = 64
NSA_N_SEL = 16
NSA_WINDOW = 512
NSA_CMP_HIDDEN = 256
MOBA_HEADS = 4
MOBA_BLOCK = 256
MOBA_TOPK = 3
DIL_PATTERNS = ((128, 1), (512, 4), (2048, 16))
DIL_HEADS_PER_GROUP = 2
DIL_WB = 128
N_BRANCHES = 3
REL_BUCKETS = 32
REL_MAX_EXACT = 16
REL_MAX_DIST = 2048
NORM_EPS = 1e-6
FORCE_SCORE = 1e4
N_HEADS_TOTAL = NSA_HEADS + MOBA_HEADS + DIL_HEADS_PER_GROUP * len(DIL_PATTERNS)

LANES = 128
SLOT = LANES
TQ = 128
TK = 128
KV_CHUNK = 4
WIN_TILES = NSA_WINDOW // TK
TILE_PAD = 4
NEG = -1e30
M_INIT = -1e29
MXU_DTYPE = jnp.bfloat16
VMEM_LIMIT = 56 * 1024 * 1024
F32 = jnp.float32


def _cparams(n_grid, vmem=VMEM_LIMIT):
    return pltpu.CompilerParams(dimension_semantics=("arbitrary",) * n_grid, vmem_limit_bytes=vmem)


def _const_spec(shape):
    nd = len(shape)
    return pl.BlockSpec(shape, lambda *_: (0,) * nd)


def _dot(a, b):
    return jnp.dot(a, b, preferred_element_type=F32)


def _dot_nt(a, b, precision=None):
    return lax.dot_general(a, b, (((1,), (1,)), ((), ())), preferred_element_type=F32, precision=precision)


def _rms(x, gain):
    return x * lax.rsqrt(jnp.mean(x * x, axis=-1, keepdims=True) + NORM_EPS) * gain


def _ffn_kernel(x_ref, g_ref, wg_ref, wu_ref, wd_ref, *rest, ff_chunk, final_norm):
    if final_norm:
        fg_ref, o_ref, acc_ref = rest
    else:
        o_ref, acc_ref = rest
    x = x_ref[...]
    h = _rms(x, g_ref[...]).astype(MXU_DTYPE)
    d_ff = wg_ref.shape[1]
    for c in range(d_ff // ff_chunk):
        sl = slice(c * ff_chunk, (c + 1) * ff_chunk)
        a = _dot(h, wg_ref[:, sl])
        u = _dot(h, wu_ref[:, sl])
        z = (a * jax.nn.sigmoid(a) * u).astype(MXU_DTYPE)
        part = _dot(z, wd_ref[sl, :])
        if c == 0:
            acc_ref[...] = part
        else:
            acc_ref[...] += part
    y = x + 0.5 * acc_ref[...]
    if final_norm:
        y = _rms(y, fg_ref[...])
    o_ref[...] = y


def _ffn(x2, gain, wg, wu, wd, final_gain=None, tm=512):
    t, d = x2.shape
    d_ff = wg.shape[1]
    ff_chunk = 256 if d_ff % 256 == 0 else d_ff
    tm = min(tm, t)
    final = final_gain is not None
    in_specs = [pl.BlockSpec((tm, d), lambda i: (i, 0)), _const_spec((1, d)),
                _const_spec((d, d_ff)), _const_spec((d, d_ff)), _const_spec((d_ff, d))]
    args = [x2, gain.reshape(1, d), wg, wu, wd]
    if final:
        in_specs.append(_const_spec((1, d)))
        args.append(final_gain.reshape(1, d))
    return pl.pallas_call(
        functools.partial(_ffn_kernel, ff_chunk=ff_chunk, final_norm=final),
        out_shape=jax.ShapeDtypeStruct((t, d), F32),
        grid=(t // tm,),
        in_specs=in_specs,
        out_specs=pl.BlockSpec((tm, d), lambda i: (i, 0)),
        scratch_shapes=[pltpu.VMEM((tm, d), F32)],
        compiler_params=_cparams(1),
        name="ffn_swiglu",
    )(*args)


def _bias_tile_kernel(tbl_ref, o_ref, *, heads, width, row_step, col_mult, offset, dist_mult, lo, hi, n_cols):
    d = pl.program_id(0)
    i = lax.broadcasted_iota(jnp.int32, (TQ, width), 0)
    j = lax.broadcasted_iota(jnp.int32, (TQ, width), 1)
    raw = row_step * d + i - col_mult * j + offset
    valid = (raw >= lo) & (raw <= hi) & (j < n_cols)
    n = jnp.maximum(raw * dist_mult, 0)
    nf = jnp.maximum(n, REL_MAX_EXACT).astype(F32)
    large = REL_MAX_EXACT + (jnp.log(nf / REL_MAX_EXACT) / math.log(REL_MAX_DIST / REL_MAX_EXACT)
                             * (REL_BUCKETS - REL_MAX_EXACT)).astype(jnp.int32)
    large = jnp.minimum(large, REL_BUCKETS - 1)
    bucket = jnp.where(n < REL_MAX_EXACT, n, large)
    for hh, head in enumerate(heads):
        val = jnp.zeros((TQ, width), F32)
        for k in range(REL_BUCKETS):
            val = jnp.where(bucket == k, tbl_ref[k, head], val)
        o_ref[0, hh] = jnp.where(valid, val, NEG)


def _bias_tiles(rel_bias, *, n_tiles, heads, width=TK, row_step=TQ, col_mult=1, offset=0, dist_mult=1,
                lo=0, hi=2 ** 30, n_cols=None, name):
    n_cols = width if n_cols is None else n_cols
    nh = len(heads)
    return pl.pallas_call(
        functools.partial(_bias_tile_kernel, heads=tuple(heads), width=width, row_step=row_step,
                          col_mult=col_mult, offset=offset, dist_mult=dist_mult, lo=lo, hi=hi, n_cols=n_cols),
        out_shape=jax.ShapeDtypeStruct((n_tiles, nh, TQ, width), F32),
        grid=(n_tiles,),
        in_specs=[pl.BlockSpec(memory_space=pltpu.SMEM)],
        out_specs=pl.BlockSpec((1, nh, TQ, width), lambda d: (d, 0, 0, 0)),
        compiler_params=_cparams(1),
        name=name,
    )(rel_bias)


_PROJ_SEGS = (
    ("nsa_q", NSA_HEADS * SLOT, MXU_DTYPE),
    ("nsa_kc", LANES, F32),
    ("nsa_vc", LANES, F32),
    ("nsa_ks", NSA_KV_HEADS * SLOT, MXU_DTYPE),
    ("nsa_vs", NSA_KV_HEADS * SLOT, MXU_DTYPE),
    ("nsa_kw", NSA_KV_HEADS * SLOT, MXU_DTYPE),
    ("nsa_vw", NSA_KV_HEADS * SLOT, MXU_DTYPE),
    ("nsa_gate", NSA_KV_HEADS * SLOT, F32),
    ("moba_q", MOBA_HEADS * SLOT, MXU_DTYPE),
    ("moba_k", MOBA_HEADS * SLOT, MXU_DTYPE),
    ("moba_v", MOBA_HEADS * SLOT, MXU_DTYPE),
    ("dil_q0", LANES, MXU_DTYPE), ("dil_q1", LANES, MXU_DTYPE), ("dil_q2", LANES, MXU_DTYPE),
    ("dil_k0", LANES, MXU_DTYPE), ("dil_k1", LANES, MXU_DTYPE), ("dil_k2", LANES, MXU_DTYPE),
    ("dil_v0", LANES, MXU_DTYPE), ("dil_v1", LANES, MXU_DTYPE), ("dil_v2", LANES, MXU_DTYPE),
)
_PROJ_OFFS = np.concatenate([[0], np.cumsum([w for _, w, _ in _PROJ_SEGS])])
_PROJ_COLS = int(_PROJ_OFFS[-1])


def _proj_column_map():
    hd = HEAD_DIM
    kvw = NSA_KV_HEADS * hd
    off = {}
    o = 0
    for name, w in (("nsa_q", NSA_HEADS * hd), ("nsa_k_cmp", kvw), ("nsa_v_cmp", kvw), ("nsa_k_sel", kvw),
                    ("nsa_v_sel", kvw), ("nsa_k_win", kvw), ("nsa_v_win", kvw), ("nsa_gate", NSA_HEADS * 3),
                    ("moba_q", MOBA_HEADS * hd), ("moba_k", MOBA_HEADS * hd), ("moba_v", MOBA_HEADS * hd),
                    ("dil_q", 6 * hd), ("dil_k", 6 * hd), ("dil_v", 6 * hd)):
        off[name] = o
        o += w
    merge_off = o
    src = np.full((_PROJ_COLS,), -1, np.int64)
    scale = np.ones((_PROJ_COLS,), np.float32)
    seg_off = {name: int(_PROJ_OFFS[i]) for i, (name, _, _) in enumerate(_PROJ_SEGS)}
    qk_scale = hd ** -0.5

    def put(seg, slot, src_start, n, s=1.0, lane0=0):
        base = seg_off[seg] + slot * SLOT + lane0
        src[base:base + n] = np.arange(src_start, src_start + n)
        scale[base:base + n] = s

    for h in range(NSA_HEADS):
        put("nsa_q", h, off["nsa_q"] + h * hd, hd, qk_scale)
    put("nsa_kc", 0, off["nsa_k_cmp"], kvw)
    put("nsa_vc", 0, off["nsa_v_cmp"], kvw)
    for g in range(NSA_KV_HEADS):
        put("nsa_ks", g, off["nsa_k_sel"] + g * hd, hd)
        put("nsa_vs", g, off["nsa_v_sel"] + g * hd, hd)
        put("nsa_kw", g, off["nsa_k_win"] + g * hd, hd)
        put("nsa_vw", g, off["nsa_v_win"] + g * hd, hd)
        put("nsa_gate", g, off["nsa_gate"] + g * NSA_REP * 3, NSA_REP * 3)
    for h in range(MOBA_HEADS):
        put("moba_q", h, off["moba_q"] + h * hd, hd, qk_scale)
        put("moba_k", h, off["moba_k"] + h * hd, hd)
        put("moba_v", h, off["moba_v"] + h * hd, hd)
    for g in range(len(DIL_PATTERNS)):
        put(f"dil_q{g}", 0, off["dil_q"] + g * 2 * hd, 2 * hd, qk_scale)
        put(f"dil_k{g}", 0, off["dil_k"] + g * 2 * hd, 2 * hd)
        put(f"dil_v{g}", 0, off["dil_v"] + g * 2 * hd, 2 * hd)
    return src, scale, merge_off


def _proj_kernel(x_ref, g_ref, w_ref, *out_refs, seq, tm):
    outs = dict(zip([n for n, _, _ in _PROJ_SEGS] + ["moba_kmean"], out_refs))
    h = _rms(x_ref[...], g_ref[...]).astype(MXU_DTYPE)
    pos0 = (pl.program_id(0) * tm) % seq
    for si, (name, width, dtype) in enumerate(_PROJ_SEGS):
        c0 = int(_PROJ_OFFS[si])
        y = _dot(h, w_ref[:, c0:c0 + width])
        if name == "moba_k":
            nblk = tm // MOBA_BLOCK
            outs["moba_kmean"][0] = jnp.mean(y.reshape(nblk, MOBA_BLOCK, width), axis=1)
        if name in ("nsa_ks", "moba_k"):
            blk = NSA_SEL_BLOCK if name == "nsa_ks" else MOBA_BLOCK
            row = lax.broadcasted_iota(jnp.int32, (tm, width), 0)
            lane = lax.broadcasted_iota(jnp.int32, (tm, width), 1) & (SLOT - 1)
            hit = (lane - HEAD_DIM) == jnp.right_shift(pos0 + row, int(math.log2(blk)))
            y = jnp.where(hit, 1.0, y)
        if name == "nsa_gate":
            y = jax.nn.sigmoid(y)
        outs[name][...] = y.astype(dtype)


def _proj(x2, gain, w_attn, seq, tm=512):
    t, d = x2.shape
    tm = min(tm, seq)
    assert tm % MOBA_BLOCK == 0 and seq % tm == 0
    nblk = tm // MOBA_BLOCK
    out_shape = [jax.ShapeDtypeStruct((t, w), dt) for _, w, dt in _PROJ_SEGS]
    out_specs = [pl.BlockSpec((tm, w), lambda i: (i, 0)) for _, w, _ in _PROJ_SEGS]
    kw = MOBA_HEADS * SLOT
    out_shape.append(jax.ShapeDtypeStruct((t // tm, nblk, kw), F32))
    out_specs.append(pl.BlockSpec((1, nblk, kw), lambda i: (i, 0, 0)))
    outs = pl.pallas_call(
        functools.partial(_proj_kernel, seq=seq, tm=tm),
        out_shape=out_shape,
        grid=(t // tm,),
        in_specs=[pl.BlockSpec((tm, d), lambda i: (i, 0)), _const_spec((1, d)), _const_spec((d, _PROJ_COLS))],
        out_specs=out_specs,
        compiler_params=_cparams(1),
        name="mixer_in_proj",
    )(x2, gain.reshape(1, d), w_attn)
    res = {name: o for (name, _, _), o in zip(_PROJ_SEGS, outs)}
    res["moba_kmean"] = outs[-1]
    return res


def _gelu_tanh(x):
    return 0.5 * x * (1.0 + jnp.tanh(math.sqrt(2.0 / math.pi) * (x + 0.044715 * (x * x * x))))


def _compress_kernel(k_ref, v_ref, pek_ref, pev_ref, wk1_ref, wv1_ref, wk2_ref, wv2_ref, kc_ref, vc_ref):
    nr = k_ref.shape[1]
    hid_w = NSA_CMP_HIDDEN
    for x_ref, pe_ref, w1_ref, w2_ref, o_ref in ((k_ref, pek_ref, wk1_ref, wk2_ref, kc_ref),
                                                 (v_ref, pev_ref, wv1_ref, wv2_ref, vc_ref)):
        r = x_ref[0]
        lo = _dot((r + pe_ref[0:1, :]).astype(MXU_DTYPE), w1_ref[0])
        hi = _dot((r + pe_ref[1:2, :]).astype(MXU_DTYPE), w1_ref[1])
        hid = lo + pltpu.roll(hi, nr - 1, 0)
        act = _gelu_tanh(hid).astype(MXU_DTYPE)
        for g in range(NSA_KV_HEADS):
            o_ref[0, g] = _dot(act[:, g * hid_w:(g + 1) * hid_w], w2_ref[...]).astype(o_ref.dtype)


def _compress(kc_in, vc_in, pek, pev, wk1, wv1, wk2, wv2, batch, seq):
    nr = seq // NSA_CMP_STRIDE
    rw = NSA_CMP_STRIDE * LANES
    kin = kc_in.reshape(batch, nr, rw)
    vin = vc_in.reshape(batch, nr, rw)
    hw = NSA_KV_HEADS * NSA_CMP_HIDDEN
    out = jax.ShapeDtypeStruct((batch, NSA_KV_HEADS, nr, SLOT), MXU_DTYPE)
    in_blk = pl.BlockSpec((1, nr, rw), lambda b: (b, 0, 0))
    out_blk = pl.BlockSpec((1, NSA_KV_HEADS, nr, SLOT), lambda b: (b, 0, 0, 0))
    return pl.pallas_call(
        _compress_kernel,
        out_shape=[out, out],
        grid=(batch,),
        in_specs=[in_blk, in_blk, _const_spec((2, rw)), _const_spec((2, rw)),
                  _const_spec((2, rw, hw)), _const_spec((2, rw, hw)),
                  _const_spec((NSA_CMP_HIDDEN, SLOT)), _const_spec((NSA_CMP_HIDDEN, SLOT))],
        out_specs=[out_blk, out_blk],
        compiler_params=_cparams(1),
        name="nsa_compress",
    )(kin, vin, pek, pev, wk1, wv1, wk2, wv2)


def _rank_before(score):
    n = score.shape[0]
    idx = lax.broadcasted_iota(jnp.int32, score.shape, 0)
    rank = jnp.zeros(score.shape, jnp.int32)
    for j in range(n):
        row = score[j:j + 1, :]
        ahead = (row > score) | ((row == score) & (idx > j))
        rank = rank + jnp.where(ahead, 1, 0)
    return rank


def _bias_block(t_ref, q_tile0, k_tile0, n_stack, n_qs, n_kt):
    rows = []
    for r in range(n_stack):
        for a in range(n_qs):
            base = q_tile0 + a - k_tile0 + TILE_PAD
            rows.append(jnp.concatenate([t_ref[base - j, r] for j in range(n_kt)], axis=1))
    return jnp.concatenate(rows, axis=0)


def _flash_chunks(q, k_ref, v_ref, t_ref, q_tile0, n_stack, n_qs, n_chunks):
    rows = q.shape[0]
    kw = KV_CHUNK * TK

    def body(c, carry):
        m, l, acc = carry
        ks = pl.multiple_of(c * kw, kw)
        k = k_ref[pl.ds(ks, kw), :]
        v = v_ref[pl.ds(ks, kw), :]
        s = _dot_nt(q, k) + _bias_block(t_ref, q_tile0, c * KV_CHUNK, n_stack, n_qs, KV_CHUNK)
        m_new = jnp.maximum(m, jnp.max(s, axis=-1, keepdims=True))
        alpha = jnp.exp(m - m_new)
        p = jnp.exp(s - m_new)
        l = alpha * l + jnp.sum(p, axis=-1, keepdims=True)
        acc = alpha * acc + _dot(p.astype(MXU_DTYPE), v)
        return m_new, l, acc

    init = (jnp.full((rows, 1), M_INIT, F32), jnp.zeros((rows, 1), F32), jnp.zeros((rows, SLOT), F32))
    m, l, acc = lax.fori_loop(0, n_chunks, body, init)
    return acc / jnp.maximum(l, 1e-30)


def _attend_once(q, k, v, bias):
    s = _dot_nt(q, k) + bias
    m = jnp.maximum(jnp.max(s, axis=-1, keepdims=True), M_INIT)
    p = jnp.exp(s - m)
    l = jnp.maximum(jnp.sum(p, axis=-1, keepdims=True), 1e-30)
    return _dot(p.astype(MXU_DTYPE), v) / l


def _block_mask_lanes(z_ref, sel_t, tq):
    n = sel_t.shape[0]
    z_ref[...] = jnp.zeros(z_ref.shape, F32)
    z_ref[HEAD_DIM:HEAD_DIM + n, :] = jnp.where(sel_t, 0.0, NEG)
    return z_ref[...].T


def _nsa_kernel(q_ref, gate_ref, kc_ref, vc_ref, ks_ref, vs_ref, kw_ref, vw_ref, c2s_ref,
                tcmp_ref, tsel_ref, twin_ref, o_ref, z_ref, *, n_slc):
    qt = pl.program_id(2)
    rep = NSA_REP
    q_all = q_ref[0]
    q = jnp.concatenate([q_all[:, r * SLOT:(r + 1) * SLOT] for r in range(rep)], axis=0)
    rows = rep * TQ

    n_cmp_pad = kc_ref.shape[2]
    sc = _dot_nt(q, kc_ref[0, 0]) + tcmp_ref[0].reshape(rows, n_cmp_pad)
    mc = jnp.maximum(jnp.max(sc, axis=-1, keepdims=True), M_INIT)
    pc = jnp.exp(sc - mc)
    pc = pc / jnp.maximum(jnp.sum(pc, axis=-1, keepdims=True), 1e-30)
    o_c = _dot(pc.astype(MXU_DTYPE), vc_ref[0, 0])

    p_sum = pc[0:TQ]
    for r in range(1, rep):
        p_sum = p_sum + pc[r * TQ:(r + 1) * TQ]
    imp = jnp.dot(p_sum, c2s_ref[...], preferred_element_type=F32, precision=lax.Precision.HIGHEST)
    imp_t = imp.T[0:n_slc]
    blk = lax.broadcasted_iota(jnp.int32, (n_slc, TQ), 0)
    t_pos = qt * TQ + lax.broadcasted_iota(jnp.int32, (n_slc, TQ), 1)
    cur = jnp.right_shift(t_pos, int(math.log2(NSA_SEL_BLOCK)))
    forced = (blk == 0) | (blk == cur) | (blk == cur - 1)
    score = jnp.where(blk > cur, NEG, jnp.where(forced, FORCE_SCORE, imp_t))
    sel_t = _rank_before(score) < min(NSA_N_SEL, n_slc)
    zt = _block_mask_lanes(z_ref, sel_t, TQ)
    q_sel = (q.astype(F32) + jnp.concatenate([zt] * rep, axis=0)).astype(MXU_DTYPE)

    o_s = _flash_chunks(q_sel, ks_ref.at[0], vs_ref.at[0], tsel_ref, qt, rep, 1, qt // KV_CHUNK + 1)
    w0 = jnp.maximum(qt - WIN_TILES, 0)
    w_rows = pl.ds(pl.multiple_of(w0 * TK, TK), (WIN_TILES + 1) * TK)
    o_w = _attend_once(q, kw_ref[0, w_rows, :], vw_ref[0, w_rows, :],
                       _bias_block(twin_ref, qt, w0, rep, 1, WIN_TILES + 1))

    gates = gate_ref[0]
    for r in range(rep):
        sl = slice(r * TQ, (r + 1) * TQ)
        o_r = (gates[:, 3 * r:3 * r + 1] * o_c[sl] + gates[:, 3 * r + 1:3 * r + 2] * o_s[sl]
               + gates[:, 3 * r + 2:3 * r + 3] * o_w[sl])
        o_ref[0, :, r * SLOT:(r + 1) * SLOT] = o_r.astype(o_ref.dtype)


def _nsa(p, kc, vc, c2s, tcmp, tmain, twin, batch, seq):
    nq = seq // TQ
    n_slc = seq // NSA_SEL_BLOCK
    assert n_slc <= HEAD_DIM and nq % KV_CHUNK == 0 and nq > WIN_TILES
    g_n, rep = NSA_KV_HEADS, NSA_REP
    n_cmp_pad = kc.shape[2]
    r3 = lambda a: a.reshape(batch, seq, a.shape[-1])
    kv_spec = pl.BlockSpec((1, seq, SLOT), lambda b, g, i: (b, 0, g))
    cmp_spec = pl.BlockSpec((1, 1, n_cmp_pad, SLOT), lambda b, g, i: (b, g, 0, 0))
    n_win = twin.shape[0]
    return pl.pallas_call(
        functools.partial(_nsa_kernel, n_slc=n_slc),
        out_shape=jax.ShapeDtypeStruct((batch, seq, NSA_HEADS * SLOT), MXU_DTYPE),
        grid=(batch, g_n, nq),
        in_specs=[
            pl.BlockSpec((1, TQ, rep * SLOT), lambda b, g, i: (b, i, g)),
            pl.BlockSpec((1, TQ, SLOT), lambda b, g, i: (b, i, g)),
            cmp_spec, cmp_spec, kv_spec, kv_spec, kv_spec, kv_spec,
            _const_spec(c2s.shape),
            pl.BlockSpec((1, rep, TQ, n_cmp_pad), lambda b, g, i: (i, g, 0, 0)),
            pl.BlockSpec((tmain.shape[0], rep, TQ, TK), lambda b, g, i: (0, g, 0, 0)),
            pl.BlockSpec((n_win, rep, TQ, TK), lambda b, g, i: (0, g, 0, 0)),
        ],
        out_specs=pl.BlockSpec((1, TQ, rep * SLOT), lambda b, g, i: (b, i, g)),
        scratch_shapes=[pltpu.VMEM((LANES, TQ), F32)],
        compiler_params=_cparams(3),
        name="nsa_attention",
    )(r3(p["nsa_q"]), r3(p["nsa_gate"]), kc, vc, r3(p["nsa_ks"]), r3(p["nsa_vs"]), r3(p["nsa_kw"]),
      r3(p["nsa_vw"]), c2s, tcmp, tmain, twin)


def _moba_kernel(q_ref, k_ref, v_ref, km_ref, t_ref, o_ref, z_ref, *, nb):
    own = pl.program_id(2)
    q = q_ref[0]
    tq = MOBA_BLOCK
    gate_t = _dot_nt(km_ref[0], q.astype(F32), precision=lax.Precision.HIGHEST)
    n_i = lax.broadcasted_iota(jnp.int32, (nb, tq), 0)
    past = n_i < own
    score = jnp.where(past, gate_t, NEG)
    sel_t = ((_rank_before(score) < min(MOBA_TOPK, nb - 1)) & past) | (n_i == own)
    zt = _block_mask_lanes(z_ref, sel_t, tq)
    q_sel = (q.astype(F32) + zt).astype(MXU_DTYPE)
    n_qs = MOBA_BLOCK // TQ
    n_chunks = (own * n_qs + n_qs - 1) // KV_CHUNK + 1
    o = _flash_chunks(q_sel, k_ref.at[0], v_ref.at[0], t_ref, own * n_qs, 1, n_qs, n_chunks)
    o_ref[0] = o.astype(o_ref.dtype)


def _moba(p, kmean, tmain, batch, seq):
    nq = seq // TQ
    nb = seq // MOBA_BLOCK
    assert seq % MOBA_BLOCK == 0 and nb <= HEAD_DIM and MOBA_BLOCK % TQ == 0 and nq % KV_CHUNK == 0
    assert TILE_PAD >= KV_CHUNK + MOBA_BLOCK // TQ - 2
    r3 = lambda a: a.reshape(batch, seq, a.shape[-1])
    kv_spec = pl.BlockSpec((1, seq, SLOT), lambda b, h, i: (b, 0, h))
    q_spec = pl.BlockSpec((1, MOBA_BLOCK, SLOT), lambda b, h, i: (b, i, h))
    return pl.pallas_call(
        functools.partial(_moba_kernel, nb=nb),
        out_shape=jax.ShapeDtypeStruct((batch, seq, MOBA_HEADS * SLOT), MXU_DTYPE),
        grid=(batch, MOBA_HEADS, nb),
        in_specs=[q_spec, kv_spec, kv_spec,
                  pl.BlockSpec((1, nb, SLOT), lambda b, h, i: (b, 0, h)),
                  pl.BlockSpec((tmain.shape[0], 1, TQ, TK), lambda b, h, i: (0, NSA_HEADS + h, 0, 0))],
        out_specs=q_spec,
        scratch_shapes=[pltpu.VMEM((LANES, MOBA_BLOCK), F32)],
        compiler_params=_cparams(3),
        name="moba_attention",
    )(r3(p["moba_q"]), r3(p["moba_k"]), r3(p["moba_v"]), kmean.reshape(batch, nb, MOBA_HEADS * SLOT), tmain)


def _dilated_kernel(q_ref, k_ref, v_ref, t_ref, o_ref, lse_ref):
    n = pl.program_id(2)
    q = q_ref[0]
    cur = pl.multiple_of(n * DIL_WB, DIL_WB)
    prev = pl.multiple_of(jnp.maximum(n - 1, 0) * DIL_WB, DIL_WB)
    k_cur, v_cur = k_ref[0, pl.ds(cur, DIL_WB), :], v_ref[0, pl.ds(cur, DIL_WB), :]
    k_prev, v_prev = k_ref[0, pl.ds(prev, DIL_WB), :], v_ref[0, pl.ds(prev, DIL_WB), :]
    no_prev = jnp.where(n == 0, NEG, 0.0)
    lane = lax.broadcasted_iota(jnp.int32, (DIL_WB, LANES), 1)
    o_heads, lse_heads = [], []
    for h in range(DIL_HEADS_PER_GROUP):
        in_head = (lane >= h * HEAD_DIM) & (lane < (h + 1) * HEAD_DIM)
        qh = jnp.where(in_head, q, jnp.zeros_like(q))
        s_cur = _dot_nt(qh, k_cur) + t_ref[0, h]
        s_prev = _dot_nt(qh, k_prev) + t_ref[1, h] + no_prev
        m = jnp.maximum(jnp.max(s_cur, axis=-1, keepdims=True), jnp.max(s_prev, axis=-1, keepdims=True))
        p_cur = jnp.exp(s_cur - m)
        p_prev = jnp.exp(s_prev - m)
        l = jnp.maximum(jnp.sum(p_cur, axis=-1, keepdims=True) + jnp.sum(p_prev, axis=-1, keepdims=True), 1e-30)
        o_heads.append((_dot(p_cur.astype(MXU_DTYPE), v_cur) + _dot(p_prev.astype(MXU_DTYPE), v_prev)) / l)
        lse_heads.append(m + jnp.log(l))
    first = lane < HEAD_DIM
    o_ref[0] = jnp.where(first, o_heads[0], o_heads[1])
    lse_ref[0] = jnp.where(first, lse_heads[0], lse_heads[1])


def _dilated(q, k, v, tdil, dilation, batch, seq):
    ln = seq // dilation
    assert ln % DIL_WB == 0
    view = lambda a: a.reshape(batch, ln, dilation * LANES)
    q_spec = pl.BlockSpec((1, DIL_WB, LANES), lambda b, r, n: (b, n, r))
    kv_spec = pl.BlockSpec((1, ln, LANES), lambda b, r, n: (b, 0, r))
    out = jax.ShapeDtypeStruct((batch, ln, dilation * LANES), F32)
    o, lse = pl.pallas_call(
        _dilated_kernel,
        out_shape=[out, out],
        grid=(batch, dilation, ln // DIL_WB),
        in_specs=[q_spec, kv_spec, kv_spec, _const_spec(tdil.shape)],
        out_specs=[q_spec, q_spec],
        compiler_params=_cparams(3),
        name=f"dilated_attention_d{dilation}",
    )(view(q), view(k), view(v), tdil)
    return o.reshape(batch * seq, LANES), lse.reshape(batch * seq, LANES)


def _mixer_out_kernel(x_ref, g_ref, ya_ref, yb_ref, o0_ref, o1_ref, o2_ref, l0_ref, l1_ref, l2_ref,
                      wmg_ref, wa_ref, wb_ref, wc_ref, wo_ref, out_ref):
    x = x_ref[...]
    d = x.shape[1]
    h = _rms(x, g_ref[...]).astype(MXU_DTYPE)
    l0, l1, l2 = l0_ref[...], l1_ref[...], l2_ref[...]
    mx = jnp.maximum(jnp.maximum(l0, l1), l2)
    e0, e1, e2 = jnp.exp(l0 - mx), jnp.exp(l1 - mx), jnp.exp(l2 - mx)
    y_c = (e0 * o0_ref[...] + e1 * o1_ref[...] + e2 * o2_ref[...]) / (e0 + e1 + e2)
    merged = jax.nn.sigmoid(_dot(h, wmg_ref[:, 0:d])) * _dot(ya_ref[...], wa_ref[...])
    merged += jax.nn.sigmoid(_dot(h, wmg_ref[:, d:2 * d])) * _dot(yb_ref[...], wb_ref[...])
    merged += jax.nn.sigmoid(_dot(h, wmg_ref[:, 2 * d:3 * d])) * _dot(y_c.astype(MXU_DTYPE), wc_ref[...])
    out_ref[...] = x + _dot(merged.astype(MXU_DTYPE), wo_ref[...])


def _mixer_out(x2, gain, ya, yb, dil, wmg, wa, wb, wc, wo, tm=512):
    t, d = x2.shape
    tm = min(tm, t)
    row = lambda w: pl.BlockSpec((tm, w), lambda i: (i, 0))
    (o0, l0), (o1, l1), (o2, l2) = dil
    return pl.pallas_call(
        _mixer_out_kernel,
        out_shape=jax.ShapeDtypeStruct((t, d), F32),
        grid=(t // tm,),
        in_specs=[row(d), _const_spec((1, d)), row(ya.shape[1]), row(yb.shape[1])] + [row(LANES)] * 6
                 + [_const_spec(w.shape) for w in (wmg, wa, wb, wc, wo)],
        out_specs=row(d),
        compiler_params=_cparams(1),
        name="mixer_out_proj",
    )(x2, gain.reshape(1, d), ya, yb, o0, o1, o2, l0, l1, l2, wmg, wa, wb, wc, wo)


def _slot_rows(w, n_slots):
    d = w.shape[1]
    w = w.reshape(n_slots, HEAD_DIM, d)
    return jnp.concatenate([w, jnp.zeros_like(w)], axis=1).reshape(n_slots * SLOT, d)


def _compress_weights(pe, w1, w2):
    g_n, hd, half = NSA_KV_HEADS, HEAD_DIM, NSA_CMP_STRIDE
    hid = w1.shape[1]
    w1r = w1.reshape(2, half, hd, hid)
    eye = jnp.eye(g_n, dtype=w1.dtype)
    w1p = jnp.einsum("plch,gk->plgckh", w1r, eye).reshape(2, half * g_n * hd, g_n * hid)
    pep = jnp.broadcast_to(pe.reshape(2, half, 1, hd), (2, half, g_n, hd)).reshape(2, half * g_n * hd)
    w2p = jnp.concatenate([w2, jnp.zeros_like(w2)], axis=1)
    return pep, w1p.astype(MXU_DTYPE), w2p.astype(MXU_DTYPE)


def _cmp_to_slc(seq, n_cmp_pad):
    n_cmp = (seq - NSA_CMP_BLOCK) // NSA_CMP_STRIDE + 1
    n_slc = seq // NSA_SEL_BLOCK
    c_start = np.arange(n_cmp_pad) * NSA_CMP_STRIDE
    s_start = np.arange(LANES) * NSA_SEL_BLOCK
    ov = ((c_start[:, None] < s_start[None, :] + NSA_SEL_BLOCK) & (c_start[:, None] + NSA_CMP_BLOCK > s_start[None, :])
          & (np.arange(n_cmp_pad)[:, None] < n_cmp) & (np.arange(LANES)[None, :] < n_slc))
    return jnp.asarray(ov, F32)


def kernel(x, rel_bias, ffn1_norm, ffn1_w_gate, ffn1_w_up, ffn1_w_down, mix_norm, w_in, nsa_pe_k, nsa_pe_v,
           nsa_phi_k1, nsa_phi_k2, nsa_phi_v1, nsa_phi_v2, w_up_a, w_up_b, w_up_c, w_o, ffn2_norm, ffn2_w_gate,
           ffn2_w_up, ffn2_w_down, final_norm):
    batch, seq, d = x.shape
    depth = w_in.shape[0]
    nq = seq // TQ
    assert seq % (TQ * DIL_PATTERNS[-1][1]) == 0 and all(w // dl == DIL_WB for w, dl in DIL_PATTERNS)
    bf = lambda a: a.astype(MXU_DTYPE)

    n_cmp = (seq - NSA_CMP_BLOCK) // NSA_CMP_STRIDE + 1
    n_cmp_pad = seq // NSA_CMP_STRIDE
    a_heads = list(range(NSA_HEADS))
    ab_heads = list(range(NSA_HEADS + MOBA_HEADS))
    assert TILE_PAD >= max(WIN_TILES, KV_CHUNK - 1)
    t_main = _bias_tiles(rel_bias, n_tiles=nq + TILE_PAD, heads=ab_heads, offset=-TILE_PAD * TQ, name="bias_causal")
    t_win = _bias_tiles(rel_bias, n_tiles=WIN_TILES + 1 + TILE_PAD, heads=a_heads, offset=-TILE_PAD * TQ,
                        hi=NSA_WINDOW - 1, name="bias_window")
    t_cmp = _bias_tiles(rel_bias, n_tiles=nq, heads=a_heads, width=n_cmp_pad, col_mult=NSA_CMP_STRIDE,
                        offset=-(NSA_CMP_BLOCK - 1), n_cols=n_cmp, name="bias_compressed")
    t_dil = []
    for g, (_, dilation) in enumerate(DIL_PATTERNS):
        h0 = NSA_HEADS + MOBA_HEADS + g * DIL_HEADS_PER_GROUP
        t_dil.append(_bias_tiles(rel_bias, n_tiles=2, heads=[h0, h0 + 1], dist_mult=dilation, hi=DIL_WB,
                                 name=f"bias_dilated_{g}"))
    c2s = _cmp_to_slc(seq, n_cmp_pad)

    src, scale, merge_off = _proj_column_map()
    x2 = x.reshape(batch * seq, d)
    for l in range(depth):
        last = l == depth - 1
        x2 = _ffn(x2, ffn1_norm[l], bf(ffn1_w_gate[l]), bf(ffn1_w_up[l]), bf(ffn1_w_down[l]))

        w_l = w_in[l]
        w_attn = bf(jnp.where(src[None, :] >= 0, w_l[:, np.maximum(src, 0)], 0.0) * scale[None, :])
        p = _proj(x2, mix_norm[l], w_attn, seq)
        pek, wk1, wk2 = _compress_weights(nsa_pe_k[l], nsa_phi_k1[l], nsa_phi_k2[l])
        pev, wv1, wv2 = _compress_weights(nsa_pe_v[l], nsa_phi_v1[l], nsa_phi_v2[l])
        kc, vc = _compress(p["nsa_kc"], p["nsa_vc"], pek, pev, wk1, wv1, wk2, wv2, batch, seq)
        ya = _nsa(p, kc, vc, c2s, t_cmp, t_main, t_win, batch, seq)
        yb = _moba(p, p["moba_kmean"], t_main, batch, seq)
        dil = [_dilated(p[f"dil_q{g}"], p[f"dil_k{g}"], p[f"dil_v{g}"], t_dil[g], dilation, batch, seq)
               for g, (_, dilation) in enumerate(DIL_PATTERNS)]
        x2 = _mixer_out(x2, mix_norm[l], ya.reshape(batch * seq, -1), yb.reshape(batch * seq, -1), dil,
                        bf(w_l[:, merge_off:]), bf(_slot_rows(w_up_a[l], NSA_HEADS)),
                        bf(_slot_rows(w_up_b[l], MOBA_HEADS)), bf(w_up_c[l]), bf(w_o[l]))

        x2 = _ffn(x2, ffn2_norm[l], bf(ffn2_w_gate[l]), bf(ffn2_w_up[l]), bf(ffn2_w_down[l]),
                  final_gain=final_norm if last else None)
    return x2.reshape(batch, seq, d)
```

```python
import functools
import math

import jax
import jax.numpy as jnp
import numpy as np
from jax import lax
from jax.experimental import pallas as pl
from jax.experimental.pallas import tpu as pltpu

HEAD_DIM = 64
NSA_HEADS = 6
NSA_KV_HEADS = 2
NSA_REP = NSA_HEADS // NSA_KV_HEADS
NSA_CMP_BLOCK = 32
NSA_CMP_STRIDE = 16
NSA_SEL_BLOCK = 64
NSA_N_SEL = 16
NSA_WINDOW = 512
NSA_CMP_HIDDEN = 256
MOBA_HEADS = 4
MOBA_BLOCK = 256
MOBA_TOPK = 3
DIL_PATTERNS = ((128, 1), (512, 4), (2048, 16))
DIL_HEADS_PER_GROUP = 2
DIL_WB = 128
N_BRANCHES = 3
REL_BUCKETS = 32
REL_MAX_EXACT = 16
REL_MAX_DIST = 2048
NORM_EPS = 1e-6
FORCE_SCORE = 1e4
N_HEADS_TOTAL = NSA_HEADS + MOBA_HEADS + DIL_HEADS_PER_GROUP * len(DIL_PATTERNS)

LANES = 128
SLOT = LANES
TQ = 128
TK = 128
KV_CHUNK = 4
WIN_TILES = NSA_WINDOW // TK
TILE_PAD = 4
NEG = -1e30
M_INIT = -1e29
MXU_DTYPE = jnp.bfloat16
VMEM_LIMIT = 56 * 1024 * 1024
F32 = jnp.float32


def _cparams(n_grid, vmem=VMEM_LIMIT):
    return pltpu.CompilerParams(dimension_semantics=("arbitrary",) * n_grid, vmem_limit_bytes=vmem)


def _const_spec(shape):
    nd = len(shape)
    return pl.BlockSpec(shape, lambda *_: (0,) * nd)


def _dot(a, b):
    return jnp.dot(a, b, preferred_element_type=F32)


def _dot_nt(a, b, precision=None):
    return lax.dot_general(a, b, (((1,), (1,)), ((), ())), preferred_element_type=F32, precision=precision)


def _rms(x, gain):
    return x * lax.rsqrt(jnp.mean(x * x, axis=-1, keepdims=True) + NORM_EPS) * gain


def _ffn_kernel(x_ref, g_ref, wg_ref, wu_ref, wd_ref, *rest, ff_chunk, final_norm):
    if final_norm:
        fg_ref, o_ref, acc_ref = rest
    else:
        o_ref, acc_ref = rest
    x = x_ref[...]
    h = _rms(x, g_ref[...]).astype(MXU_DTYPE)
    d_ff = wg_ref.shape[1]
    for c in range(d_ff // ff_chunk):
        sl = slice(c * ff_chunk, (c + 1) * ff_chunk)
        a = _dot(h, wg_ref[:, sl])
        u = _dot(h, wu_ref[:, sl])
        z = (a * jax.nn.sigmoid(a) * u).astype(MXU_DTYPE)
        part = _dot(z, wd_ref[sl, :])
        if c == 0:
            acc_ref[...] = part
        else:
            acc_ref[...] += part
    y = x + 0.5 * acc_ref[...]
    if final_norm:
        y = _rms(y, fg_ref[...])
    o_ref[...] = y


def _ffn(x2, gain, wg, wu, wd, final_gain=None, tm=512):
    t, d = x2.shape
    d_ff = wg.shape[1]
    ff_chunk = 256 if d_ff % 256 == 0 else d_ff
    tm = min(tm, t)
    final = final_gain is not None
    in_specs = [pl.BlockSpec((tm, d), lambda i: (i, 0)), _const_spec((1, d)),
                _const_spec((d, d_ff)), _const_spec((d, d_ff)), _const_spec((d_ff, d))]
    args = [x2, gain.reshape(1, d), wg, wu, wd]
    if final:
        in_specs.append(_const_spec((1, d)))
        args.append(final_gain.reshape(1, d))
    return pl.pallas_call(
        functools.partial(_ffn_kernel, ff_chunk=ff_chunk, final_norm=final),
        out_shape=jax.ShapeDtypeStruct((t, d), F32),
        grid=(t // tm,),
        in_specs=in_specs,
        out_specs=pl.BlockSpec((tm, d), lambda i: (i, 0)),
        scratch_shapes=[pltpu.VMEM((tm, d), F32)],
        compiler_params=_cparams(1),
        name="ffn_swiglu",
    )(*args)


def _bias_tile_kernel(tbl_ref, o_ref, *, heads, width, row_step, col_mult, offset, dist_mult, lo, hi, n_cols):
    d = pl.program_id(0)
    i = lax.broadcasted_iota(jnp.int32, (TQ, width), 0)
    j = lax.broadcasted_iota(jnp.int32, (TQ, width), 1)
    raw = row_step * d + i - col_mult * j + offset
    valid = (raw >= lo) & (raw <= hi) & (j < n_cols)
    n = jnp.maximum(raw * dist_mult, 0)
    nf = jnp.maximum(n, REL_MAX_EXACT).astype(F32)
    large = REL_MAX_EXACT + (jnp.log(nf / REL_MAX_EXACT) / math.log(REL_MAX_DIST / REL_MAX_EXACT)
                             * (REL_BUCKETS - REL_MAX_EXACT)).astype(jnp.int32)
    large = jnp.minimum(large, REL_BUCKETS - 1)
    bucket = jnp.where(n < REL_MAX_EXACT, n, large)
    for hh, head in enumerate(heads):
        val = jnp.zeros((TQ, width), F32)
        for k in range(REL_BUCKETS):
            val = jnp.where(bucket == k, tbl_ref[k, head], val)
        o_ref[0, hh] = jnp.where(valid, val, NEG)


def _bias_tiles(rel_bias, *, n_tiles, heads, width=TK, row_step=TQ, col_mult=1, offset=0, dist_mult=1,
                lo=0, hi=2 ** 30, n_cols=None, name):
    n_cols = width if n_cols is None else n_cols
    nh = len(heads)
    return pl.pallas_call(
        functools.partial(_bias_tile_kernel, heads=tuple(heads), width=width, row_step=row_step,
                          col_mult=col_mult, offset=offset, dist_mult=dist_mult, lo=lo, hi=hi, n_cols=n_cols),
        out_shape=jax.ShapeDtypeStruct((n_tiles, nh, TQ, width), F32),
        grid=(n_tiles,),
        in_specs=[pl.BlockSpec(memory_space=pltpu.SMEM)],
        out_specs=pl.BlockSpec((1, nh, TQ, width), lambda d: (d, 0, 0, 0)),
        compiler_params=_cparams(1),
        name=name,
    )(rel_bias)


_PROJ_SEGS = (
    ("nsa_q", NSA_HEADS * SLOT, MXU_DTYPE),
    ("nsa_kc", LANES, F32),
    ("nsa_vc", LANES, F32),
    ("nsa_ks", NSA_KV_HEADS * SLOT, MXU_DTYPE),
    ("nsa_vs", NSA_KV_HEADS * SLOT, MXU_DTYPE),
    ("nsa_kw", NSA_KV_HEADS * SLOT, MXU_DTYPE),
    ("nsa_vw", NSA_KV_HEADS * SLOT, MXU_DTYPE),
    ("nsa_gate", NSA_KV_HEADS * SLOT, F32),
    ("moba_q", MOBA_HEADS * SLOT, MXU_DTYPE),
    ("moba_k", MOBA_HEADS * SLOT, MXU_DTYPE),
    ("moba_v", MOBA_HEADS * SLOT, MXU_DTYPE),
    ("dil_q0", LANES, MXU_DTYPE), ("dil_q1", LANES, MXU_DTYPE), ("dil_q2", LANES, MXU_DTYPE),
    ("dil_k0", LANES, MXU_DTYPE), ("dil_k1", LANES, MXU_DTYPE), ("dil_k2", LANES, MXU_DTYPE),
    ("dil_v0", LANES, MXU_DTYPE), ("dil_v1", LANES, MXU_DTYPE), ("dil_v2", LANES, MXU_DTYPE),
)
_PROJ_OFFS = np.concatenate([[0], np.cumsum([w for _, w, _ in _PROJ_SEGS])])
_PROJ_COLS = int(_PROJ_OFFS[-1])


def _proj_column_map():
    hd = HEAD_DIM
    kvw = NSA_KV_HEADS * hd
    off = {}
    o = 0
    for name, w in (("nsa_q", NSA_HEADS * hd), ("nsa_k_cmp", kvw), ("nsa_v_cmp", kvw), ("nsa_k_sel", kvw),
                    ("nsa_v_sel", kvw), ("nsa_k_win", kvw), ("nsa_v_win", kvw), ("nsa_gate", NSA_HEADS * 3),
                    ("moba_q", MOBA_HEADS * hd), ("moba_k", MOBA_HEADS * hd), ("moba_v", MOBA_HEADS * hd),
                    ("dil_q", 6 * hd), ("dil_k", 6 * hd), ("dil_v", 6 * hd)):
        off[name] = o
        o += w
    merge_off = o
    src = np.full((_PROJ_COLS,), -1, np.int64)
    scale = np.ones((_PROJ_COLS,), np.float32)
    seg_off = {name: int(_PROJ_OFFS[i]) for i, (name, _, _) in enumerate(_PROJ_SEGS)}
    qk_scale = hd ** -0.5

    def put(seg, slot, src_start, n, s=1.0, lane0=0):
        base = seg_off[seg] + slot * SLOT + lane0
        src[base:base + n] = np.arange(src_start, src_start + n)
        scale[base:base + n] = s

    for h in range(NSA_HEADS):
        put("nsa_q", h, off["nsa_q"] + h * hd, hd, qk_scale)
    put("nsa_kc", 0, off["nsa_k_cmp"], kvw)
    put("nsa_vc", 0, off["nsa_v_cmp"], kvw)
    for g in range(NSA_KV_HEADS):
        put("nsa_ks", g, off["nsa_k_sel"] + g * hd, hd)
        put("nsa_vs", g, off["nsa_v_sel"] + g * hd, hd)
        put("nsa_kw", g, off["nsa_k_win"] + g * hd, hd)
        put("nsa_vw", g, off["nsa_v_win"] + g * hd, hd)
        put("nsa_gate", g, off["nsa_gate"] + g * NSA_REP * 3, NSA_REP * 3)
    for h in range(MOBA_HEADS):
        put("moba_q", h, off["moba_q"] + h * hd, hd, qk_scale)
        put("moba_k", h, off["moba_k"] + h * hd, hd)
        put("moba_v", h, off["moba_v"] + h * hd, hd)
    for g in range(len(DIL_PATTERNS)):
        put(f"dil_q{g}", 0, off["dil_q"] + g * 2 * hd, 2 * hd, qk_scale)
        put(f"dil_k{g}", 0, off["dil_k"] + g * 2 * hd, 2 * hd)
        put(f"dil_v{g}", 0, off["dil_v"] + g * 2 * hd, 2 * hd)
    return src, scale, merge_off


def _proj_kernel(x_ref, g_ref, w_ref, *out_refs, seq, tm):
    outs = dict(zip([n for n, _, _ in _PROJ_SEGS] + ["moba_kmean"], out_refs))
    h = _rms(x_ref[...], g_ref[...]).astype(MXU_DTYPE)
    pos0 = (pl.program_id(0) * tm) % seq
    for si, (name, width, dtype) in enumerate(_PROJ_SEGS):
        c0 = int(_PROJ_OFFS[si])
        y = _dot(h, w_ref[:, c0:c0 + width])
        if name == "moba_k":
            nblk = tm // MOBA_BLOCK
            outs["moba_kmean"][0] = jnp.mean(y.reshape(nblk, MOBA_BLOCK, width), axis=1)
        if name in ("nsa_ks", "moba_k"):
            blk = NSA_SEL_BLOCK if name == "nsa_ks" else MOBA_BLOCK
            row = lax.broadcasted_iota(jnp.int32, (tm, width), 0)
            lane = lax.broadcasted_iota(jnp.int32, (tm, width), 1) & (SLOT - 1)
            hit = (lane - HEAD_DIM) == jnp.right_shift(pos0 + row, int(math.log2(blk)))
            y = jnp.where(hit, 1.0, y)
        if name == "nsa_gate":
            y = jax.nn.sigmoid(y)
        outs[name][...] = y.astype(dtype)


def _proj(x2, gain, w_attn, seq, tm=512):
    t, d = x2.shape
    tm = min(tm, seq)
    assert tm % MOBA_BLOCK == 0 and seq % tm == 0
    nblk = tm // MOBA_BLOCK
    out_shape = [jax.ShapeDtypeStruct((t, w), dt) for _, w, dt in _PROJ_SEGS]
    out_specs = [pl.BlockSpec((tm, w), lambda i: (i, 0)) for _, w, _ in _PROJ_SEGS]
    kw = MOBA_HEADS * SLOT
    out_shape.append(jax.ShapeDtypeStruct((t // tm, nblk, kw), F32))
    out_specs.append(pl.BlockSpec((1, nblk, kw), lambda i: (i, 0, 0)))
    outs = pl.pallas_call(
        functools.partial(_proj_kernel, seq=seq, tm=tm),
        out_shape=out_shape,
        grid=(t // tm,),
        in_specs=[pl.BlockSpec((tm, d), lambda i: (i, 0)), _const_spec((1, d)), _const_spec((d, _PROJ_COLS))],
        out_specs=out_specs,
        compiler_params=_cparams(1),
        name="mixer_in_proj",
    )(x2, gain.reshape(1, d), w_attn)
    res = {name: o for (name, _, _), o in zip(_PROJ_SEGS, outs)}
    res["moba_kmean"] = outs[-1]
    return res


def _gelu_tanh(x):
    return 0.5 * x * (1.0 + jnp.tanh(math.sqrt(2.0 / math.pi) * (x + 0.044715 * (x * x * x))))


def _compress_kernel(k_ref, v_ref, pek_ref, pev_ref, wk1_ref, wv1_ref, wk2_ref, wv2_ref, kc_ref, vc_ref):
    nr = k_ref.shape[1]
    hid_w = NSA_CMP_HIDDEN
    for x_ref, pe_ref, w1_ref, w2_ref, o_ref in ((k_ref, pek_ref, wk1_ref, wk2_ref, kc_ref),
                                                 (v_ref, pev_ref, wv1_ref, wv2_ref, vc_ref)):
        r = x_ref[0]
        lo = _dot((r + pe_ref[0:1, :]).astype(MXU_DTYPE), w1_ref[0])
        hi = _dot((r + pe_ref[1:2, :]).astype(MXU_DTYPE), w1_ref[1])
        hid = lo + pltpu.roll(hi, nr - 1, 0)
        act = _gelu_tanh(hid).astype(MXU_DTYPE)
        for g in range(NSA_KV_HEADS):
            o_ref[0, g] = _dot(act[:, g * hid_w:(g + 1) * hid_w], w2_ref[...]).astype(o_ref.dtype)


def _compress(kc_in, vc_in, pek, pev, wk1, wv1, wk2, wv2, batch, seq):
    nr = seq // NSA_CMP_STRIDE
    rw = NSA_CMP_STRIDE * LANES
    kin = kc_in.reshape(batch, nr, rw)
    vin = vc_in.reshape(batch, nr, rw)
    hw = NSA_KV_HEADS * NSA_CMP_HIDDEN
    out = jax.ShapeDtypeStruct((batch, NSA_KV_HEADS, nr, SLOT), MXU_DTYPE)
    in_blk = pl.BlockSpec((1, nr, rw), lambda b: (b, 0, 0))
    out_blk = pl.BlockSpec((1, NSA_KV_HEADS, nr, SLOT), lambda b: (b, 0, 0, 0))
    return pl.pallas_call(
        _compress_kernel,
        out_shape=[out, out],
        grid=(batch,),
        in_specs=[in_blk, in_blk, _const_spec((2, rw)), _const_spec((2, rw)),
                  _const_spec((2, rw, hw)), _const_spec((2, rw, hw)),
                  _const_spec((NSA_CMP_HIDDEN, SLOT)), _const_spec((NSA_CMP_HIDDEN, SLOT))],
        out_specs=[out_blk, out_blk],
        compiler_params=_cparams(1),
        name="nsa_compress",
    )(kin, vin, pek, pev, wk1, wv1, wk2, wv2)


def _rank_before(score):
    n = score.shape[0]
    idx = lax.broadcasted_iota(jnp.int32, score.shape, 0)
    rank = jnp.zeros(score.shape, jnp.int32)
    for j in range(n):
        row = score[j:j + 1, :]
        ahead = (row > score) | ((row == score) & (idx > j))
        rank = rank + jnp.where(ahead, 1, 0)
    return rank


def _bias_block(t_ref, q_tile0, k_tile0, n_stack, n_qs, n_kt):
    rows = []
    for r in range(n_stack):
        for a in range(n_qs):
            base = q_tile0 + a - k_tile0 + TILE_PAD
            rows.append(jnp.concatenate([t_ref[base - j, r] for j in range(n_kt)], axis=1))
    return jnp.concatenate(rows, axis=0)


def _flash_chunks(q, k_ref, v_ref, t_ref, q_tile0, n_stack, n_qs, n_chunks, scratch):
    q_s, s_a, s_b, p_a, p_b, m_s, l_s, a_s, acc_s = scratch
    rows = q.shape[0]
    kw = KV_CHUNK * TK
    stat = (rows, LANES)

    def chunk_rows(c):
        return pl.ds(pl.multiple_of(c * kw, kw), kw)

    def logits(c):
        return (_dot_nt(q_s[...], k_ref[chunk_rows(c), :])
                + _bias_block(t_ref, q_tile0, c * KV_CHUNK, n_stack, n_qs, KV_CHUNK))

    def pv(p_ref, c):
        acc_s[...] = a_s[...] * acc_s[...] + _dot(p_ref[...], v_ref[chunk_rows(c), :])

    def step(c, s_cur, s_nxt, p_cur, p_prev):
        pv(p_prev, jnp.maximum(c - 1, 0))
        s_nxt[...] = logits(jnp.minimum(c + 1, n_chunks - 1))
        tiles = [s_cur[:, j * TK:(j + 1) * TK] for j in range(KV_CHUNK)]
        m_prev = m_s[...]
        m_new = jnp.maximum(m_prev, jnp.broadcast_to(
            jnp.max(functools.reduce(jnp.maximum, tiles), axis=-1, keepdims=True), stat))
        alpha = jnp.exp(m_prev - m_new)
        ps = [jnp.exp(t - m_new) for t in tiles]
        l_s[...] = alpha * l_s[...] + jnp.broadcast_to(
            jnp.sum(functools.reduce(jnp.add, ps), axis=-1, keepdims=True), stat)
        m_s[...] = m_new
        a_s[...] = alpha
        for j in range(KV_CHUNK):
            p_cur[:, j * TK:(j + 1) * TK] = ps[j].astype(MXU_DTYPE)

    q_s[...] = q
    m_s[...] = jnp.full(stat, M_INIT, F32)
    l_s[...] = jnp.zeros(stat, F32)
    a_s[...] = jnp.ones(stat, F32)
    acc_s[...] = jnp.zeros(stat, F32)
    p_b[...] = jnp.zeros(p_b.shape, MXU_DTYPE)
    s_a[...] = logits(0)

    def body(i, carry):
        c = 2 * i
        step(c, s_a, s_b, p_a, p_b)

        @pl.when(c + 1 < n_chunks)
        def _():
            step(c + 1, s_b, s_a, p_b, p_a)
        return carry

    lax.fori_loop(0, (n_chunks + 1) // 2, body, 0)
    last = n_chunks - 1

    @pl.when(last % 2 == 0)
    def _():
        pv(p_a, last)

    @pl.when(last % 2 == 1)
    def _():
        pv(p_b, last)
    return acc_s[...] / jnp.maximum(l_s[...], 1e-30)


def _flash_scratch(rows):
    kw = KV_CHUNK * TK
    return [pltpu.VMEM((rows, LANES), MXU_DTYPE),
            pltpu.VMEM((rows, kw), F32), pltpu.VMEM((rows, kw), F32),
            pltpu.VMEM((rows, kw), MXU_DTYPE), pltpu.VMEM((rows, kw), MXU_DTYPE)] + [pltpu.VMEM((rows, LANES), F32)] * 4


def _attend_once(q, k, v, bias):
    s = _dot_nt(q, k) + bias
    m = jnp.maximum(jnp.max(s, axis=-1, keepdims=True), M_INIT)
    p = jnp.exp(s - m)
    l = jnp.maximum(jnp.sum(p, axis=-1, keepdims=True), 1e-30)
    return _dot(p.astype(MXU_DTYPE), v) / l


def _block_mask_lanes(z_ref, sel_t, tq):
    n = sel_t.shape[0]
    z_ref[...] = jnp.zeros(z_ref.shape, F32)
    z_ref[HEAD_DIM:HEAD_DIM + n, :] = jnp.where(sel_t, 0.0, NEG)
    return z_ref[...].T


def _nsa_kernel(q_ref, gate_ref, kc_ref, vc_ref, ks_ref, vs_ref, kw_ref, vw_ref, c2s_ref,
                tcmp_ref, tsel_ref, twin_ref, o_ref, z_ref, *flash_scratch, n_slc):
    qt = pl.program_id(2)
    rep = NSA_REP
    q_all = q_ref[0]
    q = jnp.concatenate([q_all[:, r * SLOT:(r + 1) * SLOT] for r in range(rep)], axis=0)
    rows = rep * TQ

    w0 = jnp.maximum(qt - WIN_TILES, 0)
    w_rows = pl.ds(pl.multiple_of(w0 * TK, TK), (WIN_TILES + 1) * TK)
    o_w = _attend_once(q, kw_ref[0, w_rows, :], vw_ref[0, w_rows, :],
                       _bias_block(twin_ref, qt, w0, rep, 1, WIN_TILES + 1))

    n_cmp_pad = kc_ref.shape[2]
    sc = _dot_nt(q, kc_ref[0, 0]) + tcmp_ref[0].reshape(rows, n_cmp_pad)
    mc = jnp.maximum(jnp.max(sc, axis=-1, keepdims=True), M_INIT)
    pc = jnp.exp(sc - mc)
    pc = pc / jnp.maximum(jnp.sum(pc, axis=-1, keepdims=True), 1e-30)
    o_c = _dot(pc.astype(MXU_DTYPE), vc_ref[0, 0])

    p_sum = pc[0:TQ]
    for r in range(1, rep):
        p_sum = p_sum + pc[r * TQ:(r + 1) * TQ]
    imp = jnp.dot(p_sum, c2s_ref[...], preferred_element_type=F32, precision=lax.Precision.HIGHEST)
    imp_t = imp.T[0:n_slc]
    blk = lax.broadcasted_iota(jnp.int32, (n_slc, TQ), 0)
    t_pos = qt * TQ + lax.broadcasted_iota(jnp.int32, (n_slc, TQ), 1)
    cur = jnp.right_shift(t_pos, int(math.log2(NSA_SEL_BLOCK)))
    forced = (blk == 0) | (blk == cur) | (blk == cur - 1)
    score = jnp.where(blk > cur, NEG, jnp.where(forced, FORCE_SCORE, imp_t))
    sel_t = _rank_before(score) < min(NSA_N_SEL, n_slc)
    zt = _block_mask_lanes(z_ref, sel_t, TQ)
    q_sel = (q.astype(F32) + jnp.concatenate([zt] * rep, axis=0)).astype(MXU_DTYPE)

    o_s = _flash_chunks(q_sel, ks_ref.at[0], vs_ref.at[0], tsel_ref, qt, rep, 1, qt // KV_CHUNK + 1, flash_scratch)

    gates = gate_ref[0]
    for r in range(rep):
        sl = slice(r * TQ, (r + 1) * TQ)
        o_r = (gates[:, 3 * r:3 * r + 1] * o_c[sl] + gates[:, 3 * r + 1:3 * r + 2] * o_s[sl]
               + gates[:, 3 * r + 2:3 * r + 3] * o_w[sl])
        o_ref[0, :, r * SLOT:(r + 1) * SLOT] = o_r.astype(o_ref.dtype)


def _nsa(p, kc, vc, c2s, tcmp, tmain, twin, batch, seq):
    nq = seq // TQ
    n_slc = seq // NSA_SEL_BLOCK
    assert n_slc <= HEAD_DIM and nq % KV_CHUNK == 0 and nq > WIN_TILES
    g_n, rep = NSA_KV_HEADS, NSA_REP
    n_cmp_pad = kc.shape[2]
    r3 = lambda a: a.reshape(batch, seq, a.shape[-1])
    kv_spec = pl.BlockSpec((1, seq, SLOT), lambda b, g, i: (b, 0, g))
    cmp_spec = pl.BlockSpec((1, 1, n_cmp_pad, SLOT), lambda b, g, i: (b, g, 0, 0))
    n_win = twin.shape[0]
    return pl.pallas_call(
        functools.partial(_nsa_kernel, n_slc=n_slc),
        out_shape=jax.ShapeDtypeStruct((batch, seq, NSA_HEADS * SLOT), MXU_DTYPE),
        grid=(batch, g_n, nq),
        in_specs=[
            pl.BlockSpec((1, TQ, rep * SLOT), lambda b, g, i: (b, i, g)),
            pl.BlockSpec((1, TQ, SLOT), lambda b, g, i: (b, i, g)),
            cmp_spec, cmp_spec, kv_spec, kv_spec, kv_spec, kv_spec,
            _const_spec(c2s.shape),
            pl.BlockSpec((1, rep, TQ, n_cmp_pad), lambda b, g, i: (i, g, 0, 0)),
            pl.BlockSpec((tmain.shape[0], rep, TQ, TK), lambda b, g, i: (0, g, 0, 0)),
            pl.BlockSpec((n_win, rep, TQ, TK), lambda b, g, i: (0, g, 0, 0)),
        ],
        out_specs=pl.BlockSpec((1, TQ, rep * SLOT), lambda b, g, i: (b, i, g)),
        scratch_shapes=[pltpu.VMEM((LANES, TQ), F32)] + _flash_scratch(rep * TQ),
        compiler_params=_cparams(3),
        name="nsa_attention",
    )(r3(p["nsa_q"]), r3(p["nsa_gate"]), kc, vc, r3(p["nsa_ks"]), r3(p["nsa_vs"]), r3(p["nsa_kw"]),
      r3(p["nsa_vw"]), c2s, tcmp, tmain, twin)


def _moba_kernel(q_ref, k_ref, v_ref, km_ref, t_ref, o_ref, z_ref, *flash_scratch, nb):
    own = pl.program_id(2)
    q = q_ref[0]
    tq = MOBA_BLOCK
    gate_t = _dot_nt(km_ref[0], q.astype(F32), precision=lax.Precision.HIGHEST)
    n_i = lax.broadcasted_iota(jnp.int32, (nb, tq), 0)
    past = n_i < own
    score = jnp.where(past, gate_t, NEG)
    sel_t = ((_rank_before(score) < min(MOBA_TOPK, nb - 1)) & past) | (n_i == own)
    zt = _block_mask_lanes(z_ref, sel_t, tq)
    q_sel = (q.astype(F32) + zt).astype(MXU_DTYPE)
    n_qs = MOBA_BLOCK // TQ
    n_chunks = (own * n_qs + n_qs - 1) // KV_CHUNK + 1
    o = _flash_chunks(q_sel, k_ref.at[0], v_ref.at[0], t_ref, own * n_qs, 1, n_qs, n_chunks, flash_scratch)
    o_ref[0] = o.astype(o_ref.dtype)


def _moba(p, kmean, tmain, batch, seq):
    nq = seq // TQ
    nb = seq // MOBA_BLOCK
    assert seq % MOBA_BLOCK == 0 and nb <= HEAD_DIM and MOBA_BLOCK % TQ == 0 and nq % KV_CHUNK == 0
    assert TILE_PAD >= KV_CHUNK + MOBA_BLOCK // TQ - 2
    r3 = lambda a: a.reshape(batch, seq, a.shape[-1])
    kv_spec = pl.BlockSpec((1, seq, SLOT), lambda b, h, i: (b, 0, h))
    q_spec = pl.BlockSpec((1, MOBA_BLOCK, SLOT), lambda b, h, i: (b, i, h))
    return pl.pallas_call(
        functools.partial(_moba_kernel, nb=nb),
        out_shape=jax.ShapeDtypeStruct((batch, seq, MOBA_HEADS * SLOT), MXU_DTYPE),
        grid=(batch, MOBA_HEADS, nb),
        in_specs=[q_spec, kv_spec, kv_spec,
                  pl.BlockSpec((1, nb, SLOT), lambda b, h, i: (b, 0, h)),
                  pl.BlockSpec((tmain.shape[0], 1, TQ, TK), lambda b, h, i: (0, NSA_HEADS + h, 0, 0))],
        out_specs=q_spec,
        scratch_shapes=[pltpu.VMEM((LANES, MOBA_BLOCK), F32)] + _flash_scratch(MOBA_BLOCK),
        compiler_params=_cparams(3),
        name="moba_attention",
    )(r3(p["moba_q"]), r3(p["moba_k"]), r3(p["moba_v"]), kmean.reshape(batch, nb, MOBA_HEADS * SLOT), tmain)


def _dilated_kernel(q_ref, k_ref, v_ref, t_ref, o_ref, lse_ref):
    n = pl.program_id(2)
    q = q_ref[0]
    cur = pl.multiple_of(n * DIL_WB, DIL_WB)
    prev = pl.multiple_of(jnp.maximum(n - 1, 0) * DIL_WB, DIL_WB)
    k_cur, v_cur = k_ref[0, pl.ds(cur, DIL_WB), :], v_ref[0, pl.ds(cur, DIL_WB), :]
    k_prev, v_prev = k_ref[0, pl.ds(prev, DIL_WB), :], v_ref[0, pl.ds(prev, DIL_WB), :]
    no_prev = jnp.where(n == 0, NEG, 0.0)
    lane = lax.broadcasted_iota(jnp.int32, (DIL_WB, LANES), 1)
    o_heads, lse_heads = [], []
    for h in range(DIL_HEADS_PER_GROUP):
        in_head = (lane >= h * HEAD_DIM) & (lane < (h + 1) * HEAD_DIM)
        qh = jnp.where(in_head, q, jnp.zeros_like(q))
        s_cur = _dot_nt(qh, k_cur) + t_ref[0, h]
        s_prev = _dot_nt(qh, k_prev) + t_ref[1, h] + no_prev
        m = jnp.maximum(jnp.max(s_cur, axis=-1, keepdims=True), jnp.max(s_prev, axis=-1, keepdims=True))
        p_cur = jnp.exp(s_cur - m)
        p_prev = jnp.exp(s_prev - m)
        l = jnp.maximum(jnp.sum(p_cur, axis=-1, keepdims=True) + jnp.sum(p_prev, axis=-1, keepdims=True), 1e-30)
        o_heads.append((_dot(p_cur.astype(MXU_DTYPE), v_cur) + _dot(p_prev.astype(MXU_DTYPE), v_prev)) / l)
        lse_heads.append(m + jnp.log(l))
    first = lane < HEAD_DIM
    o_ref[0] = jnp.where(first, o_heads[0], o_heads[1])
    lse_ref[0] = jnp.where(first, lse_heads[0], lse_heads[1])


def _dilated(q, k, v, tdil, dilation, batch, seq):
    ln = seq // dilation
    assert ln % DIL_WB == 0
    view = lambda a: a.reshape(batch, ln, dilation * LANES)
    q_spec = pl.BlockSpec((1, DIL_WB, LANES), lambda b, r, n: (b, n, r))
    kv_spec = pl.BlockSpec((1, ln, LANES), lambda b, r, n: (b, 0, r))
    out = jax.ShapeDtypeStruct((batch, ln, dilation * LANES), F32)
    o, lse = pl.pallas_call(
        _dilated_kernel,
        out_shape=[out, out],
        grid=(batch, dilation, ln // DIL_WB),
        in_specs=[q_spec, kv_spec, kv_spec, _const_spec(tdil.shape)],
        out_specs=[q_spec, q_spec],
        compiler_params=_cparams(3),
        name=f"dilated_attention_d{dilation}",
    )(view(q), view(k), view(v), tdil)
    return o.reshape(batch * seq, LANES), lse.reshape(batch * seq, LANES)


def _mixer_out_kernel(x_ref, g_ref, ya_ref, yb_ref, o0_ref, o1_ref, o2_ref, l0_ref, l1_ref, l2_ref,
                      wmg_ref, wa_ref, wb_ref, wc_ref, wo_ref, out_ref):
    x = x_ref[...]
    d = x.shape[1]
    h = _rms(x, g_ref[...]).astype(MXU_DTYPE)
    l0, l1, l2 = l0_ref[...], l1_ref[...], l2_ref[...]
    mx = jnp.maximum(jnp.maximum(l0, l1), l2)
    e0, e1, e2 = jnp.exp(l0 - mx), jnp.exp(l1 - mx), jnp.exp(l2 - mx)
    y_c = (e0 * o0_ref[...] + e1 * o1_ref[...] + e2 * o2_ref[...]) / (e0 + e1 + e2)
    merged = jax.nn.sigmoid(_dot(h, wmg_ref[:, 0:d])) * _dot(ya_ref[...], wa_ref[...])
    merged += jax.nn.sigmoid(_dot(h, wmg_ref[:, d:2 * d])) * _dot(yb_ref[...], wb_ref[...])
    merged += jax.nn.sigmoid(_dot(h, wmg_ref[:, 2 * d:3 * d])) * _dot(y_c.astype(MXU_DTYPE), wc_ref[...])
    out_ref[...] = x + _dot(merged.astype(MXU_DTYPE), wo_ref[...])


def _mixer_out(x2, gain, ya, yb, dil, wmg, wa, wb, wc, wo, tm=512):
    t, d = x2.shape
    tm = min(tm, t)
    row = lambda w: pl.BlockSpec((tm, w), lambda i: (i, 0))
    (o0, l0), (o1, l1), (o2, l2) = dil
    return pl.pallas_call(
        _mixer_out_kernel,
        out_shape=jax.ShapeDtypeStruct((t, d), F32),
        grid=(t // tm,),
        in_specs=[row(d), _const_spec((1, d)), row(ya.shape[1]), row(yb.shape[1])] + [row(LANES)] * 6
                 + [_const_spec(w.shape) for w in (wmg, wa, wb, wc, wo)],
        out_specs=row(d),
        compiler_params=_cparams(1),
        name="mixer_out_proj",
    )(x2, gain.reshape(1, d), ya, yb, o0, o1, o2, l0, l1, l2, wmg, wa, wb, wc, wo)


def _slot_rows(w, n_slots):
    d = w.shape[1]
    w = w.reshape(n_slots, HEAD_DIM, d)
    return jnp.concatenate([w, jnp.zeros_like(w)], axis=1).reshape(n_slots * SLOT, d)


def _compress_weights(pe, w1, w2):
    g_n, hd, half = NSA_KV_HEADS, HEAD_DIM, NSA_CMP_STRIDE
    hid = w1.shape[1]
    w1r = w1.reshape(2, half, hd, hid)
    eye = jnp.eye(g_n, dtype=w1.dtype)
    w1p = jnp.einsum("plch,gk->plgckh", w1r, eye).reshape(2, half * g_n * hd, g_n * hid)
    pep = jnp.broadcast_to(pe.reshape(2, half, 1, hd), (2, half, g_n, hd)).reshape(2, half * g_n * hd)
    w2p = jnp.concatenate([w2, jnp.zeros_like(w2)], axis=1)
    return pep, w1p.astype(MXU_DTYPE), w2p.astype(MXU_DTYPE)


def _cmp_to_slc(seq, n_cmp_pad):
    n_cmp = (seq - NSA_CMP_BLOCK) // NSA_CMP_STRIDE + 1
    n_slc = seq // NSA_SEL_BLOCK
    c_start = np.arange(n_cmp_pad) * NSA_CMP_STRIDE
    s_start = np.arange(LANES) * NSA_SEL_BLOCK
    ov = ((c_start[:, None] < s_start[None, :] + NSA_SEL_BLOCK) & (c_start[:, None] + NSA_CMP_BLOCK > s_start[None, :])
          & (np.arange(n_cmp_pad)[:, None] < n_cmp) & (np.arange(LANES)[None, :] < n_slc))
    return jnp.asarray(ov, F32)


def kernel(x, rel_bias, ffn1_norm, ffn1_w_gate, ffn1_w_up, ffn1_w_down, mix_norm, w_in, nsa_pe_k, nsa_pe_v,
           nsa_phi_k1, nsa_phi_k2, nsa_phi_v1, nsa_phi_v2, w_up_a, w_up_b, w_up_c, w_o, ffn2_norm, ffn2_w_gate,
           ffn2_w_up, ffn2_w_down, final_norm):
    batch, seq, d = x.shape
    depth = w_in.shape[0]
    nq = seq // TQ
    assert seq % (TQ * DIL_PATTERNS[-1][1]) == 0 and all(w // dl == DIL_WB for w, dl in DIL_PATTERNS)
    bf = lambda a: a.astype(MXU_DTYPE)

    n_cmp = (seq - NSA_CMP_BLOCK) // NSA_CMP_STRIDE + 1
    n_cmp_pad = seq // NSA_CMP_STRIDE
    a_heads = list(range(NSA_HEADS))
    ab_heads = list(range(NSA_HEADS + MOBA_HEADS))
    assert TILE_PAD >= max(WIN_TILES, KV_CHUNK - 1)
    t_main = _bias_tiles(rel_bias, n_tiles=nq + TILE_PAD, heads=ab_heads, offset=-TILE_PAD * TQ, name="bias_causal")
    t_win = _bias_tiles(rel_bias, n_tiles=WIN_TILES + 1 + TILE_PAD, heads=a_heads, offset=-TILE_PAD * TQ,
                        hi=NSA_WINDOW - 1, name="bias_window")
    t_cmp = _bias_tiles(rel_bias, n_tiles=nq, heads=a_heads, width=n_cmp_pad, col_mult=NSA_CMP_STRIDE,
                        offset=-(NSA_CMP_BLOCK - 1), n_cols=n_cmp, name="bias_compressed")
    t_dil = []
    for g, (_, dilation) in enumerate(DIL_PATTERNS):
        h0 = NSA_HEADS + MOBA_HEADS + g * DIL_HEADS_PER_GROUP
        t_dil.append(_bias_tiles(rel_bias, n_tiles=2, heads=[h0, h0 + 1], dist_mult=dilation, hi=DIL_WB,
                                 name=f"bias_dilated_{g}"))
    c2s = _cmp_to_slc(seq, n_cmp_pad)

    src, scale, merge_off = _proj_column_map()
    x2 = x.reshape(batch * seq, d)
    for l in range(depth):
        last = l == depth - 1
        x2 = _ffn(x2, ffn1_norm[l], bf(ffn1_w_gate[l]), bf(ffn1_w_up[l]), bf(ffn1_w_down[l]))

        w_l = w_in[l]
        w_attn = bf(jnp.where(src[None, :] >= 0, w_l[:, np.maximum(src, 0)], 0.0) * scale[None, :])
        p = _proj(x2, mix_norm[l], w_attn, seq)
        pek, wk1, wk2 = _compress_weights(nsa_pe_k[l], nsa_phi_k1[l], nsa_phi_k2[l])
        pev, wv1, wv2 = _compress_weights(nsa_pe_v[l], nsa_phi_v1[l], nsa_phi_v2[l])
        kc, vc = _compress(p["nsa_kc"], p["nsa_vc"], pek, pev, wk1, wv1, wk2, wv2, batch, seq)
        ya = _nsa(p, kc, vc, c2s, t_cmp, t_main, t_win, batch, seq)
        yb = _moba(p, p["moba_kmean"], t_main, batch, seq)
        dil = [_dilated(p[f"dil_q{g}"], p[f"dil_k{g}"], p[f"dil_v{g}"], t_dil[g], dilation, batch, seq)
               for g, (_, dilation) in enumerate(DIL_PATTERNS)]
        x2 = _mixer_out(x2, mix_norm[l], ya.reshape(batch * seq, -1), yb.reshape(batch * seq, -1), dil,
                        bf(w_l[:, merge_off:]), bf(_slot_rows(w_up_a[l], NSA_HEADS)),
                        bf(_slot_rows(w_up_b[l], MOBA_HEADS)), bf(w_up_c[l]), bf(w_o[l]))

        x2 = _ffn(x2, ffn2_norm[l], bf(ffn2_w_gate[l]), bf(ffn2_w_up[l]), bf(ffn2_w_down[l]),
                  final_gain=final_norm if last else None)
    return x2.reshape(batch, seq, d)
```

```python
import functools
import math

import jax
import jax.numpy as jnp
import numpy as np
from jax import lax
from jax.experimental import pallas as pl
from jax.experimental.pallas import tpu as pltpu

HEAD_DIM = 64
NSA_HEADS = 6
NSA_KV_HEADS = 2
NSA_REP = NSA_HEADS // NSA_KV_HEADS
NSA_CMP_BLOCK = 32
NSA_CMP_STRIDE = 16
NSA_SEL_BLOCK = 64
NSA_N_SEL = 16
NSA_WINDOW = 512
NSA_CMP_HIDDEN = 256
MOBA_HEADS = 4
MOBA_BLOCK = 256
MOBA_TOPK = 3
DIL_PATTERNS = ((128, 1), (512, 4), (2048, 16))
DIL_HEADS_PER_GROUP = 2
DIL_WB = 128
N_BRANCHES = 3
REL_BUCKETS = 32
REL_MAX_EXACT = 16
REL_MAX_DIST = 2048
NORM_EPS = 1e-6
FORCE_SCORE = 1e4
N_HEADS_TOTAL = NSA_HEADS + MOBA_HEADS + DIL_HEADS_PER_GROUP * len(DIL_PATTERNS)

LANES = 128
SLOT = LANES
TQ = 128
TK = 128
KV_CHUNK = 4
WIN_TILES = NSA_WINDOW // TK
TILE_PAD = 6
NSA_QS = 2
MOBA_QB = 2
NEG = -1e30
M_INIT = -1e29
MXU_DTYPE = jnp.bfloat16
VMEM_LIMIT = 56 * 1024 * 1024
F32 = jnp.float32


def _cparams(n_grid, vmem=VMEM_LIMIT):
    return pltpu.CompilerParams(dimension_semantics=("arbitrary",) * n_grid, vmem_limit_bytes=vmem)


def _const_spec(shape):
    nd = len(shape)
    return pl.BlockSpec(shape, lambda *_: (0,) * nd)


def _dot(a, b):
    return jnp.dot(a, b, preferred_element_type=F32)


def _dot_nt(a, b, precision=None):
    return lax.dot_general(a, b, (((1,), (1,)), ((), ())), preferred_element_type=F32, precision=precision)


def _rms(x, gain):
    return x * lax.rsqrt(jnp.mean(x * x, axis=-1, keepdims=True) + NORM_EPS) * gain


def _ffn_kernel(x_ref, g_ref, wg_ref, wu_ref, wd_ref, *rest, ff_chunk, final_norm):
    if final_norm:
        fg_ref, o_ref, acc_ref = rest
    else:
        o_ref, acc_ref = rest
    x = x_ref[...]
    h = _rms(x, g_ref[...]).astype(MXU_DTYPE)
    d_ff = wg_ref.shape[1]
    for c in range(d_ff // ff_chunk):
        sl = slice(c * ff_chunk, (c + 1) * ff_chunk)
        a = _dot(h, wg_ref[:, sl])
        u = _dot(h, wu_ref[:, sl])
        z = (a * jax.nn.sigmoid(a) * u).astype(MXU_DTYPE)
        part = _dot(z, wd_ref[sl, :])
        if c == 0:
            acc_ref[...] = part
        else:
            acc_ref[...] += part
    y = x + 0.5 * acc_ref[...]
    if final_norm:
        y = _rms(y, fg_ref[...])
    o_ref[...] = y


def _ffn(x2, gain, wg, wu, wd, final_gain=None, tm=512):
    t, d = x2.shape
    d_ff = wg.shape[1]
    ff_chunk = 256 if d_ff % 256 == 0 else d_ff
    tm = min(tm, t)
    final = final_gain is not None
    in_specs = [pl.BlockSpec((tm, d), lambda i: (i, 0)), _const_spec((1, d)),
                _const_spec((d, d_ff)), _const_spec((d, d_ff)), _const_spec((d_ff, d))]
    args = [x2, gain.reshape(1, d), wg, wu, wd]
    if final:
        in_specs.append(_const_spec((1, d)))
        args.append(final_gain.reshape(1, d))
    return pl.pallas_call(
        functools.partial(_ffn_kernel, ff_chunk=ff_chunk, final_norm=final),
        out_shape=jax.ShapeDtypeStruct((t, d), F32),
        grid=(t // tm,),
        in_specs=in_specs,
        out_specs=pl.BlockSpec((tm, d), lambda i: (i, 0)),
        scratch_shapes=[pltpu.VMEM((tm, d), F32)],
        compiler_params=_cparams(1),
        name="ffn_swiglu",
    )(*args)


def _bias_tile_kernel(tbl_ref, o_ref, *, heads, width, row_step, col_mult, offset, dist_mult, lo, hi, n_cols):
    d = pl.program_id(0)
    i = lax.broadcasted_iota(jnp.int32, (TQ, width), 0)
    j = lax.broadcasted_iota(jnp.int32, (TQ, width), 1)
    raw = row_step * d + i - col_mult * j + offset
    valid = (raw >= lo) & (raw <= hi) & (j < n_cols)
    n = jnp.maximum(raw * dist_mult, 0)
    nf = jnp.maximum(n, REL_MAX_EXACT).astype(F32)
    large = REL_MAX_EXACT + (jnp.log(nf / REL_MAX_EXACT) / math.log(REL_MAX_DIST / REL_MAX_EXACT)
                             * (REL_BUCKETS - REL_MAX_EXACT)).astype(jnp.int32)
    large = jnp.minimum(large, REL_BUCKETS - 1)
    bucket = jnp.where(n < REL_MAX_EXACT, n, large)
    for hh, head in enumerate(heads):
        val = jnp.zeros((TQ, width), F32)
        for k in range(REL_BUCKETS):
            val = jnp.where(bucket == k, tbl_ref[k, head], val)
        o_ref[0, hh] = jnp.where(valid, val, NEG)


def _bias_tiles(rel_bias, *, n_tiles, heads, width=TK, row_step=TQ, col_mult=1, offset=0, dist_mult=1,
                lo=0, hi=2 ** 30, n_cols=None, name):
    n_cols = width if n_cols is None else n_cols
    nh = len(heads)
    return pl.pallas_call(
        functools.partial(_bias_tile_kernel, heads=tuple(heads), width=width, row_step=row_step,
                          col_mult=col_mult, offset=offset, dist_mult=dist_mult, lo=lo, hi=hi, n_cols=n_cols),
        out_shape=jax.ShapeDtypeStruct((n_tiles, nh, TQ, width), F32),
        grid=(n_tiles,),
        in_specs=[pl.BlockSpec(memory_space=pltpu.SMEM)],
        out_specs=pl.BlockSpec((1, nh, TQ, width), lambda d: (d, 0, 0, 0)),
        compiler_params=_cparams(1),
        name=name,
    )(rel_bias)


_PROJ_SEGS = (
    ("nsa_q", NSA_HEADS * SLOT, MXU_DTYPE),
    ("nsa_kc", LANES, F32),
    ("nsa_vc", LANES, F32),
    ("nsa_ks", NSA_KV_HEADS * SLOT, MXU_DTYPE),
    ("nsa_vs", NSA_KV_HEADS * SLOT, MXU_DTYPE),
    ("nsa_kw", NSA_KV_HEADS * SLOT, MXU_DTYPE),
    ("nsa_vw", NSA_KV_HEADS * SLOT, MXU_DTYPE),
    ("nsa_gate", NSA_KV_HEADS * SLOT, F32),
    ("moba_q", MOBA_HEADS * SLOT, MXU_DTYPE),
    ("moba_k", MOBA_HEADS * SLOT, MXU_DTYPE),
    ("moba_v", MOBA_HEADS * SLOT, MXU_DTYPE),
    ("dil_q0", LANES, MXU_DTYPE), ("dil_q1", LANES, MXU_DTYPE), ("dil_q2", LANES, MXU_DTYPE),
    ("dil_k0", LANES, MXU_DTYPE), ("dil_k1", LANES, MXU_DTYPE), ("dil_k2", LANES, MXU_DTYPE),
    ("dil_v0", LANES, MXU_DTYPE), ("dil_v1", LANES, MXU_DTYPE), ("dil_v2", LANES, MXU_DTYPE),
)
_PROJ_OFFS = np.concatenate([[0], np.cumsum([w for _, w, _ in _PROJ_SEGS])])
_PROJ_COLS = int(_PROJ_OFFS[-1])


def _proj_column_map():
    hd = HEAD_DIM
    kvw = NSA_KV_HEADS * hd
    off = {}
    o = 0
    for name, w in (("nsa_q", NSA_HEADS * hd), ("nsa_k_cmp", kvw), ("nsa_v_cmp", kvw), ("nsa_k_sel", kvw),
                    ("nsa_v_sel", kvw), ("nsa_k_win", kvw), ("nsa_v_win", kvw), ("nsa_gate", NSA_HEADS * 3),
                    ("moba_q", MOBA_HEADS * hd), ("moba_k", MOBA_HEADS * hd), ("moba_v", MOBA_HEADS * hd),
                    ("dil_q", 6 * hd), ("dil_k", 6 * hd), ("dil_v", 6 * hd)):
        off[name] = o
        o += w
    merge_off = o
    src = np.full((_PROJ_COLS,), -1, np.int64)
    scale = np.ones((_PROJ_COLS,), np.float32)
    seg_off = {name: int(_PROJ_OFFS[i]) for i, (name, _, _) in enumerate(_PROJ_SEGS)}
    qk_scale = hd ** -0.5

    def put(seg, slot, src_start, n, s=1.0, lane0=0):
        base = seg_off[seg] + slot * SLOT + lane0
        src[base:base + n] = np.arange(src_start, src_start + n)
        scale[base:base + n] = s

    for h in range(NSA_HEADS):
        put("nsa_q", h, off["nsa_q"] + h * hd, hd, qk_scale)
    put("nsa_kc", 0, off["nsa_k_cmp"], kvw)
    put("nsa_vc", 0, off["nsa_v_cmp"], kvw)
    for g in range(NSA_KV_HEADS):
        put("nsa_ks", g, off["nsa_k_sel"] + g * hd, hd)
        put("nsa_vs", g, off["nsa_v_sel"] + g * hd, hd)
        put("nsa_kw", g, off["nsa_k_win"] + g * hd, hd)
        put("nsa_vw", g, off["nsa_v_win"] + g * hd, hd)
        put("nsa_gate", g, off["nsa_gate"] + g * NSA_REP * 3, NSA_REP * 3)
    for h in range(MOBA_HEADS):
        put("moba_q", h, off["moba_q"] + h * hd, hd, qk_scale)
        put("moba_k", h, off["moba_k"] + h * hd, hd)
        put("moba_v", h, off["moba_v"] + h * hd, hd)
    for g in range(len(DIL_PATTERNS)):
        put(f"dil_q{g}", 0, off["dil_q"] + g * 2 * hd, 2 * hd, qk_scale)
        put(f"dil_k{g}", 0, off["dil_k"] + g * 2 * hd, 2 * hd)
        put(f"dil_v{g}", 0, off["dil_v"] + g * 2 * hd, 2 * hd)
    return src, scale, merge_off


def _proj_kernel(x_ref, g_ref, w_ref, *out_refs, seq, tm):
    outs = dict(zip([n for n, _, _ in _PROJ_SEGS] + ["moba_kmean"], out_refs))
    h = _rms(x_ref[...], g_ref[...]).astype(MXU_DTYPE)
    pos0 = (pl.program_id(0) * tm) % seq
    for si, (name, width, dtype) in enumerate(_PROJ_SEGS):
        c0 = int(_PROJ_OFFS[si])
        y = _dot(h, w_ref[:, c0:c0 + width])
        if name == "moba_k":
            nblk = tm // MOBA_BLOCK
            outs["moba_kmean"][0] = jnp.mean(y.reshape(nblk, MOBA_BLOCK, width), axis=1)
        if name in ("nsa_ks", "moba_k"):
            blk = NSA_SEL_BLOCK if name == "nsa_ks" else MOBA_BLOCK
            row = lax.broadcasted_iota(jnp.int32, (tm, width), 0)
            lane = lax.broadcasted_iota(jnp.int32, (tm, width), 1) & (SLOT - 1)
            hit = (lane - HEAD_DIM) == jnp.right_shift(pos0 + row, int(math.log2(blk)))
            y = jnp.where(hit, 1.0, y)
        if name == "nsa_gate":
            y = jax.nn.sigmoid(y)
        outs[name][...] = y.astype(dtype)


def _proj(x2, gain, w_attn, seq, tm=512):
    t, d = x2.shape
    tm = min(tm, seq)
    assert tm % MOBA_BLOCK == 0 and seq % tm == 0
    nblk = tm // MOBA_BLOCK
    out_shape = [jax.ShapeDtypeStruct((t, w), dt) for _, w, dt in _PROJ_SEGS]
    out_specs = [pl.BlockSpec((tm, w), lambda i: (i, 0)) for _, w, _ in _PROJ_SEGS]
    kw = MOBA_HEADS * SLOT
    out_shape.append(jax.ShapeDtypeStruct((t // tm, nblk, kw), F32))
    out_specs.append(pl.BlockSpec((1, nblk, kw), lambda i: (i, 0, 0)))
    outs = pl.pallas_call(
        functools.partial(_proj_kernel, seq=seq, tm=tm),
        out_shape=out_shape,
        grid=(t // tm,),
        in_specs=[pl.BlockSpec((tm, d), lambda i: (i, 0)), _const_spec((1, d)), _const_spec((d, _PROJ_COLS))],
        out_specs=out_specs,
        compiler_params=_cparams(1),
        name="mixer_in_proj",
    )(x2, gain.reshape(1, d), w_attn)
    res = {name: o for (name, _, _), o in zip(_PROJ_SEGS, outs)}
    res["moba_kmean"] = outs[-1]
    return res


def _gelu_tanh(x):
    return 0.5 * x * (1.0 + jnp.tanh(math.sqrt(2.0 / math.pi) * (x + 0.044715 * (x * x * x))))


def _compress_kernel(k_ref, v_ref, pek_ref, pev_ref, wk1_ref, wv1_ref, wk2_ref, wv2_ref, kc_ref, vc_ref):
    nr = k_ref.shape[1]
    hid_w = NSA_CMP_HIDDEN
    for x_ref, pe_ref, w1_ref, w2_ref, o_ref in ((k_ref, pek_ref, wk1_ref, wk2_ref, kc_ref),
                                                 (v_ref, pev_ref, wv1_ref, wv2_ref, vc_ref)):
        r = x_ref[0]
        lo = _dot((r + pe_ref[0:1, :]).astype(MXU_DTYPE), w1_ref[0])
        hi = _dot((r + pe_ref[1:2, :]).astype(MXU_DTYPE), w1_ref[1])
        hid = lo + pltpu.roll(hi, nr - 1, 0)
        act = _gelu_tanh(hid).astype(MXU_DTYPE)
        for g in range(NSA_KV_HEADS):
            o_ref[0, g] = _dot(act[:, g * hid_w:(g + 1) * hid_w], w2_ref[...]).astype(o_ref.dtype)


def _compress(kc_in, vc_in, pek, pev, wk1, wv1, wk2, wv2, batch, seq):
    nr = seq // NSA_CMP_STRIDE
    rw = NSA_CMP_STRIDE * LANES
    kin = kc_in.reshape(batch, nr, rw)
    vin = vc_in.reshape(batch, nr, rw)
    hw = NSA_KV_HEADS * NSA_CMP_HIDDEN
    out = jax.ShapeDtypeStruct((batch, NSA_KV_HEADS, nr, SLOT), MXU_DTYPE)
    in_blk = pl.BlockSpec((1, nr, rw), lambda b: (b, 0, 0))
    out_blk = pl.BlockSpec((1, NSA_KV_HEADS, nr, SLOT), lambda b: (b, 0, 0, 0))
    return pl.pallas_call(
        _compress_kernel,
        out_shape=[out, out],
        grid=(batch,),
        in_specs=[in_blk, in_blk, _const_spec((2, rw)), _const_spec((2, rw)),
                  _const_spec((2, rw, hw)), _const_spec((2, rw, hw)),
                  _const_spec((NSA_CMP_HIDDEN, SLOT)), _const_spec((NSA_CMP_HIDDEN, SLOT))],
        out_specs=[out_blk, out_blk],
        compiler_params=_cparams(1),
        name="nsa_compress",
    )(kin, vin, pek, pev, wk1, wv1, wk2, wv2)


def _rank_before(score):
    n = score.shape[0]
    idx = lax.broadcasted_iota(jnp.int32, score.shape, 0)
    rank = jnp.zeros(score.shape, jnp.int32)
    for j in range(n):
        row = score[j:j + 1, :]
        ahead = (row > score) | ((row == score) & (idx > j))
        rank = rank + jnp.where(ahead, 1, 0)
    return rank


def _bias_block(t_ref, q_tile0, k_tile0, n_stack, n_qs, n_kt):
    rows = []
    for r in range(n_stack):
        for a in range(n_qs):
            base = q_tile0 + a - k_tile0 + TILE_PAD
            rows.append(jnp.concatenate([t_ref[base - j, r] for j in range(n_kt)], axis=1))
    return jnp.concatenate(rows, axis=0)


def _flash_chunks(q, k_ref, v_ref, t_ref, q_tile0, n_stack, n_qs, n_chunks, scratch):
    q_s, s_a, s_b, p_a, p_b, m_s, l_s, a_s, acc_s = scratch
    rows = q.shape[0]
    kw = KV_CHUNK * TK
    stat = (rows, LANES)

    def chunk_rows(c):
        return pl.ds(pl.multiple_of(c * kw, kw), kw)

    def logits(c):
        return (_dot_nt(q_s[...], k_ref[chunk_rows(c), :])
                + _bias_block(t_ref, q_tile0, c * KV_CHUNK, n_stack, n_qs, KV_CHUNK))

    def pv(p_ref, c):
        acc_s[...] = a_s[...] * acc_s[...] + _dot(p_ref[...], v_ref[chunk_rows(c), :])

    def step(c, s_cur, s_nxt, p_cur, p_prev):
        pv(p_prev, jnp.maximum(c - 1, 0))
        s_nxt[...] = logits(jnp.minimum(c + 1, n_chunks - 1))
        tiles = [s_cur[:, j * TK:(j + 1) * TK] for j in range(KV_CHUNK)]
        m_prev = m_s[...]
        m_new = jnp.maximum(m_prev, jnp.broadcast_to(
            jnp.max(functools.reduce(jnp.maximum, tiles), axis=-1, keepdims=True), stat))
        alpha = jnp.exp(m_prev - m_new)
        ps = [jnp.exp(t - m_new) for t in tiles]
        l_s[...] = alpha * l_s[...] + jnp.broadcast_to(
            jnp.sum(functools.reduce(jnp.add, ps), axis=-1, keepdims=True), stat)
        m_s[...] = m_new
        a_s[...] = alpha
        for j in range(KV_CHUNK):
            p_cur[:, j * TK:(j + 1) * TK] = ps[j].astype(MXU_DTYPE)

    q_s[...] = q
    m_s[...] = jnp.full(stat, M_INIT, F32)
    l_s[...] = jnp.zeros(stat, F32)
    a_s[...] = jnp.ones(stat, F32)
    acc_s[...] = jnp.zeros(stat, F32)
    p_b[...] = jnp.zeros(p_b.shape, MXU_DTYPE)
    s_a[...] = logits(0)

    def body(i, carry):
        c = 2 * i
        step(c, s_a, s_b, p_a, p_b)

        @pl.when(c + 1 < n_chunks)
        def _():
            step(c + 1, s_b, s_a, p_b, p_a)
        return carry

    lax.fori_loop(0, (n_chunks + 1) // 2, body, 0)
    last = n_chunks - 1

    @pl.when(last % 2 == 0)
    def _():
        pv(p_a, last)

    @pl.when(last % 2 == 1)
    def _():
        pv(p_b, last)
    return acc_s[...] / jnp.maximum(l_s[...], 1e-30)


def _flash_scratch(rows):
    kw = KV_CHUNK * TK
    return [pltpu.VMEM((rows, LANES), MXU_DTYPE),
            pltpu.VMEM((rows, kw), F32), pltpu.VMEM((rows, kw), F32),
            pltpu.VMEM((rows, kw), MXU_DTYPE), pltpu.VMEM((rows, kw), MXU_DTYPE)] + [pltpu.VMEM((rows, LANES), F32)] * 4


def _attend_once(q, k, v, bias):
    s = _dot_nt(q, k) + bias
    m = jnp.maximum(jnp.max(s, axis=-1, keepdims=True), M_INIT)
    p = jnp.exp(s - m)
    l = jnp.maximum(jnp.sum(p, axis=-1, keepdims=True), 1e-30)
    return _dot(p.astype(MXU_DTYPE), v) / l


def _block_mask_lanes(z_ref, sel_t, tq):
    n = sel_t.shape[0]
    z_ref[...] = jnp.zeros(z_ref.shape, F32)
    z_ref[HEAD_DIM:HEAD_DIM + n, :] = jnp.where(sel_t, 0.0, NEG)
    return z_ref[...].T


def _nsa_kernel(q_ref, gate_ref, kc_ref, vc_ref, ks_ref, vs_ref, kw_ref, vw_ref, c2s_ref,
                tcmp_ref, tsel_ref, twin_ref, o_ref, z_ref, *flash_scratch, n_slc):
    step = pl.program_id(2)
    qt0 = step * NSA_QS
    rep, nqs = NSA_REP, NSA_QS
    tqb = nqs * TQ
    q_all = q_ref[0]
    q = jnp.concatenate([q_all[a * TQ:(a + 1) * TQ, r * SLOT:(r + 1) * SLOT]
                         for r in range(rep) for a in range(nqs)], axis=0)

    n_wt = WIN_TILES + nqs
    w0 = jnp.maximum(qt0 - WIN_TILES, 0)
    w_rows = pl.ds(pl.multiple_of(w0 * TK, TK), n_wt * TK)
    o_w = _attend_once(q, kw_ref[0, w_rows, :], vw_ref[0, w_rows, :],
                       _bias_block(twin_ref, qt0, w0, rep, nqs, n_wt))

    bias_c = jnp.concatenate([tcmp_ref[a, r] for r in range(rep) for a in range(nqs)], axis=0)
    sc = _dot_nt(q, kc_ref[0, 0]) + bias_c
    mc = jnp.maximum(jnp.max(sc, axis=-1, keepdims=True), M_INIT)
    pc = jnp.exp(sc - mc)
    pc = pc / jnp.maximum(jnp.sum(pc, axis=-1, keepdims=True), 1e-30)
    o_c = _dot(pc.astype(MXU_DTYPE), vc_ref[0, 0])

    p_sum = pc[0:tqb]
    for r in range(1, rep):
        p_sum = p_sum + pc[r * tqb:(r + 1) * tqb]
    p_hi = p_sum.astype(MXU_DTYPE)
    p_lo = (p_sum - p_hi.astype(F32)).astype(MXU_DTYPE)
    imp = _dot(p_hi, c2s_ref[...]) + _dot(p_lo, c2s_ref[...])
    imp_t = imp.T[0:n_slc]
    blk = lax.broadcasted_iota(jnp.int32, (n_slc, tqb), 0)
    t_pos = qt0 * TQ + lax.broadcasted_iota(jnp.int32, (n_slc, tqb), 1)
    cur = jnp.right_shift(t_pos, int(math.log2(NSA_SEL_BLOCK)))
    forced = (blk == 0) | (blk == cur) | (blk == cur - 1)
    score = jnp.where(blk > cur, NEG, jnp.where(forced, FORCE_SCORE, imp_t))
    sel_t = _rank_before(score) < min(NSA_N_SEL, n_slc)
    zt = _block_mask_lanes(z_ref, sel_t, tqb)
    q_sel = (q.astype(F32) + jnp.concatenate([zt] * rep, axis=0)).astype(MXU_DTYPE)

    n_chunks = (qt0 + nqs - 1) // KV_CHUNK + 1
    o_s = _flash_chunks(q_sel, ks_ref.at[0], vs_ref.at[0], tsel_ref, qt0, rep, nqs, n_chunks, flash_scratch)

    gates = gate_ref[0]
    for r in range(rep):
        sl = slice(r * tqb, (r + 1) * tqb)
        o_r = (gates[:, 3 * r:3 * r + 1] * o_c[sl] + gates[:, 3 * r + 1:3 * r + 2] * o_s[sl]
               + gates[:, 3 * r + 2:3 * r + 3] * o_w[sl])
        o_ref[0, :, r * SLOT:(r + 1) * SLOT] = o_r.astype(o_ref.dtype)


def _nsa(p, kc, vc, c2s, tcmp, tmain, twin, batch, seq):
    nq = seq // TQ
    n_slc = seq // NSA_SEL_BLOCK
    tqb = NSA_QS * TQ
    assert n_slc <= HEAD_DIM and nq % KV_CHUNK == 0 and nq % NSA_QS == 0 and nq >= WIN_TILES + NSA_QS
    assert TILE_PAD >= KV_CHUNK + NSA_QS - 2
    g_n, rep = NSA_KV_HEADS, NSA_REP
    n_cmp_pad = kc.shape[2]
    r3 = lambda a: a.reshape(batch, seq, a.shape[-1])
    kv_spec = pl.BlockSpec((1, seq, SLOT), lambda b, g, i: (b, 0, g))
    cmp_spec = pl.BlockSpec((1, 1, n_cmp_pad, SLOT), lambda b, g, i: (b, g, 0, 0))
    return pl.pallas_call(
        functools.partial(_nsa_kernel, n_slc=n_slc),
        out_shape=jax.ShapeDtypeStruct((batch, seq, NSA_HEADS * SLOT), MXU_DTYPE),
        grid=(batch, g_n, nq // NSA_QS),
        in_specs=[
            pl.BlockSpec((1, tqb, rep * SLOT), lambda b, g, i: (b, i, g)),
            pl.BlockSpec((1, tqb, SLOT), lambda b, g, i: (b, i, g)),
            cmp_spec, cmp_spec, kv_spec, kv_spec, kv_spec, kv_spec,
            _const_spec(c2s.shape),
            pl.BlockSpec((NSA_QS, rep, TQ, n_cmp_pad), lambda b, g, i: (i, g, 0, 0)),
            pl.BlockSpec((tmain.shape[0], rep, TQ, TK), lambda b, g, i: (0, g, 0, 0)),
            pl.BlockSpec((twin.shape[0], rep, TQ, TK), lambda b, g, i: (0, g, 0, 0)),
        ],
        out_specs=pl.BlockSpec((1, tqb, rep * SLOT), lambda b, g, i: (b, i, g)),
        scratch_shapes=[pltpu.VMEM((LANES, tqb), F32)] + _flash_scratch(rep * tqb),
        compiler_params=_cparams(3),
        name="nsa_attention",
    )(r3(p["nsa_q"]), r3(p["nsa_gate"]), kc, vc, r3(p["nsa_ks"]), r3(p["nsa_vs"]), r3(p["nsa_kw"]),
      r3(p["nsa_vw"]), c2s.astype(MXU_DTYPE), tcmp, tmain, twin)


def _moba_kernel(q_ref, k_ref, v_ref, km_ref, t_ref, o_ref, z_ref, *flash_scratch, nb):
    step = pl.program_id(2)
    tqb = MOBA_QB * MOBA_BLOCK
    n_qs = tqb // TQ
    q = q_ref[0]
    km = km_ref[0]
    km_hi = km.astype(MXU_DTYPE)
    rem = km - km_hi.astype(F32)
    km_mid = rem.astype(MXU_DTYPE)
    km_lo = (rem - km_mid.astype(F32)).astype(MXU_DTYPE)
    gate_t = _dot_nt(km_hi, q) + _dot_nt(km_mid, q) + _dot_nt(km_lo, q)
    n_i = lax.broadcasted_iota(jnp.int32, (nb, tqb), 0)
    own = step * MOBA_QB + jnp.right_shift(lax.broadcasted_iota(jnp.int32, (nb, tqb), 1), int(math.log2(MOBA_BLOCK)))
    past = n_i < own
    score = jnp.where(past, gate_t, NEG)
    sel_t = ((_rank_before(score) < min(MOBA_TOPK, nb - 1)) & past) | (n_i == own)
    zt = _block_mask_lanes(z_ref, sel_t, tqb)
    q_sel = (q.astype(F32) + zt).astype(MXU_DTYPE)
    q_tile0 = step * n_qs
    n_chunks = (q_tile0 + n_qs - 1) // KV_CHUNK + 1
    o = _flash_chunks(q_sel, k_ref.at[0], v_ref.at[0], t_ref, q_tile0, 1, n_qs, n_chunks, flash_scratch)
    o_ref[0] = o.astype(o_ref.dtype)


def _moba(p, kmean, tmain, batch, seq):
    nq = seq // TQ
    nb = seq // MOBA_BLOCK
    tqb = MOBA_QB * MOBA_BLOCK
    assert seq % tqb == 0 and nb <= HEAD_DIM and MOBA_BLOCK % TQ == 0 and nq % KV_CHUNK == 0
    assert TILE_PAD >= KV_CHUNK + tqb // TQ - 2
    r3 = lambda a: a.reshape(batch, seq, a.shape[-1])
    kv_spec = pl.BlockSpec((1, seq, SLOT), lambda b, h, i: (b, 0, h))
    q_spec = pl.BlockSpec((1, tqb, SLOT), lambda b, h, i: (b, i, h))
    return pl.pallas_call(
        functools.partial(_moba_kernel, nb=nb),
        out_shape=jax.ShapeDtypeStruct((batch, seq, MOBA_HEADS * SLOT), MXU_DTYPE),
        grid=(batch, MOBA_HEADS, seq // tqb),
        in_specs=[q_spec, kv_spec, kv_spec,
                  pl.BlockSpec((1, nb, SLOT), lambda b, h, i: (b, 0, h)),
                  pl.BlockSpec((tmain.shape[0], 1, TQ, TK), lambda b, h, i: (0, NSA_HEADS + h, 0, 0))],
        out_specs=q_spec,
        scratch_shapes=[pltpu.VMEM((LANES, tqb), F32)] + _flash_scratch(tqb),
        compiler_params=_cparams(3),
        name="moba_attention",
    )(r3(p["moba_q"]), r3(p["moba_k"]), r3(p["moba_v"]), kmean.reshape(batch, nb, MOBA_HEADS * SLOT), tmain)


def _dilated_kernel(q_ref, k_ref, v_ref, t_ref, o_ref, lse_ref):
    n = pl.program_id(2)
    q = q_ref[0]
    cur = pl.multiple_of(n * DIL_WB, DIL_WB)
    prev = pl.multiple_of(jnp.maximum(n - 1, 0) * DIL_WB, DIL_WB)
    k_cur, v_cur = k_ref[0, pl.ds(cur, DIL_WB), :], v_ref[0, pl.ds(cur, DIL_WB), :]
    k_prev, v_prev = k_ref[0, pl.ds(prev, DIL_WB), :], v_ref[0, pl.ds(prev, DIL_WB), :]
    no_prev = jnp.where(n == 0, NEG, 0.0)
    lane = lax.broadcasted_iota(jnp.int32, (DIL_WB, LANES), 1)
    o_heads, lse_heads = [], []
    for h in range(DIL_HEADS_PER_GROUP):
        in_head = (lane >= h * HEAD_DIM) & (lane < (h + 1) * HEAD_DIM)
        qh = jnp.where(in_head, q, jnp.zeros_like(q))
        s_cur = _dot_nt(qh, k_cur) + t_ref[0, h]
        s_prev = _dot_nt(qh, k_prev) + t_ref[1, h] + no_prev
        m = jnp.maximum(jnp.max(s_cur, axis=-1, keepdims=True), jnp.max(s_prev, axis=-1, keepdims=True))
        p_cur = jnp.exp(s_cur - m)
        p_prev = jnp.exp(s_prev - m)
        l = jnp.maximum(jnp.sum(p_cur, axis=-1, keepdims=True) + jnp.sum(p_prev, axis=-1, keepdims=True), 1e-30)
        o_heads.append((_dot(p_cur.astype(MXU_DTYPE), v_cur) + _dot(p_prev.astype(MXU_DTYPE), v_prev)) / l)
        lse_heads.append(m + jnp.log(l))
    first = lane < HEAD_DIM
    o_ref[0] = jnp.where(first, o_heads[0], o_heads[1])
    lse_ref[0] = jnp.where(first, lse_heads[0], lse_heads[1])


def _dilated(q, k, v, tdil, dilation, batch, seq):
    ln = seq // dilation
    assert ln % DIL_WB == 0
    view = lambda a: a.reshape(batch, ln, dilation * LANES)
    q_spec = pl.BlockSpec((1, DIL_WB, LANES), lambda b, r, n: (b, n, r))
    kv_spec = pl.BlockSpec((1, ln, LANES), lambda b, r, n: (b, 0, r))
    out = jax.ShapeDtypeStruct((batch, ln, dilation * LANES), F32)
    o, lse = pl.pallas_call(
        _dilated_kernel,
        out_shape=[out, out],
        grid=(batch, dilation, ln // DIL_WB),
        in_specs=[q_spec, kv_spec, kv_spec, _const_spec(tdil.shape)],
        out_specs=[q_spec, q_spec],
        compiler_params=_cparams(3),
        name=f"dilated_attention_d{dilation}",
    )(view(q), view(k), view(v), tdil)
    return o.reshape(batch * seq, LANES), lse.reshape(batch * seq, LANES)


def _mixer_out_kernel(x_ref, g_ref, ya_ref, yb_ref, o0_ref, o1_ref, o2_ref, l0_ref, l1_ref, l2_ref,
                      wmg_ref, wa_ref, wb_ref, wc_ref, wo_ref, out_ref):
    x = x_ref[...]
    d = x.shape[1]
    h = _rms(x, g_ref[...]).astype(MXU_DTYPE)
    l0, l1, l2 = l0_ref[...], l1_ref[...], l2_ref[...]
    mx = jnp.maximum(jnp.maximum(l0, l1), l2)
    e0, e1, e2 = jnp.exp(l0 - mx), jnp.exp(l1 - mx), jnp.exp(l2 - mx)
    y_c = (e0 * o0_ref[...] + e1 * o1_ref[...] + e2 * o2_ref[...]) / (e0 + e1 + e2)
    merged = jax.nn.sigmoid(_dot(h, wmg_ref[:, 0:d])) * _dot(ya_ref[...], wa_ref[...])
    merged += jax.nn.sigmoid(_dot(h, wmg_ref[:, d:2 * d])) * _dot(yb_ref[...], wb_ref[...])
    merged += jax.nn.sigmoid(_dot(h, wmg_ref[:, 2 * d:3 * d])) * _dot(y_c.astype(MXU_DTYPE), wc_ref[...])
    out_ref[...] = x + _dot(merged.astype(MXU_DTYPE), wo_ref[...])


def _mixer_out(x2, gain, ya, yb, dil, wmg, wa, wb, wc, wo, tm=512):
    t, d = x2.shape
    tm = min(tm, t)
    row = lambda w: pl.BlockSpec((tm, w), lambda i: (i, 0))
    (o0, l0), (o1, l1), (o2, l2) = dil
    return pl.pallas_call(
        _mixer_out_kernel,
        out_shape=jax.ShapeDtypeStruct((t, d), F32),
        grid=(t // tm,),
        in_specs=[row(d), _const_spec((1, d)), row(ya.shape[1]), row(yb.shape[1])] + [row(LANES)] * 6
                 + [_const_spec(w.shape) for w in (wmg, wa, wb, wc, wo)],
        out_specs=row(d),
        compiler_params=_cparams(1),
        name="mixer_out_proj",
    )(x2, gain.reshape(1, d), ya, yb, o0, o1, o2, l0, l1, l2, wmg, wa, wb, wc, wo)


def _slot_rows(w, n_slots):
    d = w.shape[1]
    w = w.reshape(n_slots, HEAD_DIM, d)
    return jnp.concatenate([w, jnp.zeros_like(w)], axis=1).reshape(n_slots * SLOT, d)


def _compress_weights(pe, w1, w2):
    g_n, hd, half = NSA_KV_HEADS, HEAD_DIM, NSA_CMP_STRIDE
    hid = w1.shape[1]
    w1r = w1.reshape(2, half, hd, hid)
    eye = jnp.eye(g_n, dtype=w1.dtype)
    w1p = jnp.einsum("plch,gk->plgckh", w1r, eye).reshape(2, half * g_n * hd, g_n * hid)
    pep = jnp.broadcast_to(pe.reshape(2, half, 1, hd), (2, half, g_n, hd)).reshape(2, half * g_n * hd)
    w2p = jnp.concatenate([w2, jnp.zeros_like(w2)], axis=1)
    return pep, w1p.astype(MXU_DTYPE), w2p.astype(MXU_DTYPE)


def _cmp_to_slc(seq, n_cmp_pad):
    n_cmp = (seq - NSA_CMP_BLOCK) // NSA_CMP_STRIDE + 1
    n_slc = seq // NSA_SEL_BLOCK
    c_start = np.arange(n_cmp_pad) * NSA_CMP_STRIDE
    s_start = np.arange(LANES) * NSA_SEL_BLOCK
    ov = ((c_start[:, None] < s_start[None, :] + NSA_SEL_BLOCK) & (c_start[:, None] + NSA_CMP_BLOCK > s_start[None, :])
          & (np.arange(n_cmp_pad)[:, None] < n_cmp) & (np.arange(LANES)[None, :] < n_slc))
    return jnp.asarray(ov, F32)


def kernel(x, rel_bias, ffn1_norm, ffn1_w_gate, ffn1_w_up, ffn1_w_down, mix_norm, w_in, nsa_pe_k, nsa_pe_v,
           nsa_phi_k1, nsa_phi_k2, nsa_phi_v1, nsa_phi_v2, w_up_a, w_up_b, w_up_c, w_o, ffn2_norm, ffn2_w_gate,
           ffn2_w_up, ffn2_w_down, final_norm):
    batch, seq, d = x.shape
    depth = w_in.shape[0]
    nq = seq // TQ
    assert seq % (TQ * DIL_PATTERNS[-1][1]) == 0 and all(w // dl == DIL_WB for w, dl in DIL_PATTERNS)
    bf = lambda a: a.astype(MXU_DTYPE)

    n_cmp = (seq - NSA_CMP_BLOCK) // NSA_CMP_STRIDE + 1
    n_cmp_pad = seq // NSA_CMP_STRIDE
    a_heads = list(range(NSA_HEADS))
    ab_heads = list(range(NSA_HEADS + MOBA_HEADS))
    assert TILE_PAD >= max(WIN_TILES, KV_CHUNK - 1)
    t_main = _bias_tiles(rel_bias, n_tiles=nq + TILE_PAD, heads=ab_heads, offset=-TILE_PAD * TQ, name="bias_causal")
    t_win = _bias_tiles(rel_bias, n_tiles=WIN_TILES + NSA_QS + TILE_PAD, heads=a_heads, offset=-TILE_PAD * TQ,
                        hi=NSA_WINDOW - 1, name="bias_window")
    t_cmp = _bias_tiles(rel_bias, n_tiles=nq, heads=a_heads, width=n_cmp_pad, col_mult=NSA_CMP_STRIDE,
                        offset=-(NSA_CMP_BLOCK - 1), n_cols=n_cmp, name="bias_compressed")
    t_dil = []
    for g, (_, dilation) in enumerate(DIL_PATTERNS):
        h0 = NSA_HEADS + MOBA_HEADS + g * DIL_HEADS_PER_GROUP
        t_dil.append(_bias_tiles(rel_bias, n_tiles=2, heads=[h0, h0 + 1], dist_mult=dilation, hi=DIL_WB,
                                 name=f"bias_dilated_{g}"))
    c2s = _cmp_to_slc(seq, n_cmp_pad)

    src, scale, merge_off = _proj_column_map()
    x2 = x.reshape(batch * seq, d)
    for l in range(depth):
        last = l == depth - 1
        x2 = _ffn(x2, ffn1_norm[l], bf(ffn1_w_gate[l]), bf(ffn1_w_up[l]), bf(ffn1_w_down[l]))

        w_l = w_in[l]
        w_attn = bf(jnp.where(src[None, :] >= 0, w_l[:, np.maximum(src, 0)], 0.0) * scale[None, :])
        p = _proj(x2, mix_norm[l], w_attn, seq)
        pek, wk1, wk2 = _compress_weights(nsa_pe_k[l], nsa_phi_k1[l], nsa_phi_k2[l])
        pev, wv1, wv2 = _compress_weights(nsa_pe_v[l], nsa_phi_v1[l], nsa_phi_v2[l])
        kc, vc = _compress(p["nsa_kc"], p["nsa_vc"], pek, pev, wk1, wv1, wk2, wv2, batch, seq)
        ya = _nsa(p, kc, vc, c2s, t_cmp, t_main, t_win, batch, seq)
        yb = _moba(p, p["moba_kmean"], t_main, batch, seq)
        dil = [_dilated(p[f"dil_q{g}"], p[f"dil_k{g}"], p[f"dil_v{g}"], t_dil[g], dilation, batch, seq)
               for g, (_, dilation) in enumerate(DIL_PATTERNS)]
        x2 = _mixer_out(x2, mix_norm[l], ya.reshape(batch * seq, -1), yb.reshape(batch * seq, -1), dil,
                        bf(w_l[:, merge_off:]), bf(_slot_rows(w_up_a[l], NSA_HEADS)),
                        bf(_slot_rows(w_up_b[l], MOBA_HEADS)), bf(w_up_c[l]), bf(w_o[l]))

        x2 = _ffn(x2, ffn2_norm[l], bf(ffn2_w_gate[l]), bf(ffn2_w_up[l]), bf(ffn2_w_down[l]),
                  final_gain=final_norm if last else None)
    return x2.reshape(batch, seq, d)
```

```python
import functools
import math

import jax
import jax.numpy as jnp
import numpy as np
from jax import lax
from jax.experimental import pallas as pl
from jax.experimental.pallas import tpu as pltpu

HEAD_DIM = 64
NSA_HEADS = 6
NSA_KV_HEADS = 2
NSA_REP = NSA_HEADS // NSA_KV_HEADS
NSA_CMP_BLOCK = 32
NSA_CMP_STRIDE = 16
NSA_SEL_BLOCK = 64
NSA_N_SEL = 16
NSA_WINDOW = 512
NSA_CMP_HIDDEN = 256
MOBA_HEADS = 4
MOBA_BLOCK = 256
MOBA_TOPK = 3
DIL_PATTERNS = ((128, 1), (512, 4), (2048, 16))
DIL_HEADS_PER_GROUP = 2
DIL_WB = 128
N_BRANCHES = 3
REL_BUCKETS = 32
REL_MAX_EXACT = 16
REL_MAX_DIST = 2048
NORM_EPS = 1e-6
FORCE_SCORE = 1e4
N_HEADS_TOTAL = NSA_HEADS + MOBA_HEADS + DIL_HEADS_PER_GROUP * len(DIL_PATTERNS)

LANES = 128
SLOT = LANES
TQ = 128
TK = 128
KV_CHUNK = 4
WIN_TILES = NSA_WINDOW // TK
TILE_PAD = 6
NSA_QS = 2
MOBA_QB = 2
DIL_UNROLL = 4
NEG = -1e30
M_INIT = -1e29
MXU_DTYPE = jnp.bfloat16
VMEM_LIMIT = 56 * 1024 * 1024
F32 = jnp.float32


def _cparams(n_grid, vmem=VMEM_LIMIT):
    return pltpu.CompilerParams(dimension_semantics=("arbitrary",) * n_grid, vmem_limit_bytes=vmem)


def _const_spec(shape):
    nd = len(shape)
    return pl.BlockSpec(shape, lambda *_: (0,) * nd)


def _dot(a, b):
    return jnp.dot(a, b, preferred_element_type=F32)


def _dot_nt(a, b, precision=None):
    return lax.dot_general(a, b, (((1,), (1,)), ((), ())), preferred_element_type=F32, precision=precision)


def _rms(x, gain):
    return x * lax.rsqrt(jnp.mean(x * x, axis=-1, keepdims=True) + NORM_EPS) * gain


def _ffn_kernel(x_ref, g_ref, wg_ref, wu_ref, wd_ref, *rest, ff_chunk, final_norm):
    if final_norm:
        fg_ref, o_ref, acc_ref = rest
    else:
        o_ref, acc_ref = rest
    x = x_ref[...]
    h = _rms(x, g_ref[...]).astype(MXU_DTYPE)
    d_ff = wg_ref.shape[1]
    for c in range(d_ff // ff_chunk):
        sl = slice(c * ff_chunk, (c + 1) * ff_chunk)
        a = _dot(h, wg_ref[:, sl])
        u = _dot(h, wu_ref[:, sl])
        z = (a * jax.nn.sigmoid(a) * u).astype(MXU_DTYPE)
        part = _dot(z, wd_ref[sl, :])
        if c == 0:
            acc_ref[...] = part
        else:
            acc_ref[...] += part
    y = x + 0.5 * acc_ref[...]
    if final_norm:
        y = _rms(y, fg_ref[...])
    o_ref[...] = y


def _ffn(x2, gain, wg, wu, wd, final_gain=None, tm=512):
    t, d = x2.shape
    d_ff = wg.shape[1]
    ff_chunk = 256 if d_ff % 256 == 0 else d_ff
    tm = min(tm, t)
    final = final_gain is not None
    in_specs = [pl.BlockSpec((tm, d), lambda i: (i, 0)), _const_spec((1, d)),
                _const_spec((d, d_ff)), _const_spec((d, d_ff)), _const_spec((d_ff, d))]
    args = [x2, gain.reshape(1, d), wg, wu, wd]
    if final:
        in_specs.append(_const_spec((1, d)))
        args.append(final_gain.reshape(1, d))
    return pl.pallas_call(
        functools.partial(_ffn_kernel, ff_chunk=ff_chunk, final_norm=final),
        out_shape=jax.ShapeDtypeStruct((t, d), F32),
        grid=(t // tm,),
        in_specs=in_specs,
        out_specs=pl.BlockSpec((tm, d), lambda i: (i, 0)),
        scratch_shapes=[pltpu.VMEM((tm, d), F32)],
        compiler_params=_cparams(1),
        name="ffn_swiglu",
    )(*args)


def _bias_tile_kernel(tbl_ref, o_ref, *, heads, width, row_step, col_mult, offset, dist_mult, lo, hi, n_cols):
    d = pl.program_id(0)
    i = lax.broadcasted_iota(jnp.int32, (TQ, width), 0)
    j = lax.broadcasted_iota(jnp.int32, (TQ, width), 1)
    raw = row_step * d + i - col_mult * j + offset
    valid = (raw >= lo) & (raw <= hi) & (j < n_cols)
    n = jnp.maximum(raw * dist_mult, 0)
    nf = jnp.maximum(n, REL_MAX_EXACT).astype(F32)
    large = REL_MAX_EXACT + (jnp.log(nf / REL_MAX_EXACT) / math.log(REL_MAX_DIST / REL_MAX_EXACT)
                             * (REL_BUCKETS - REL_MAX_EXACT)).astype(jnp.int32)
    large = jnp.minimum(large, REL_BUCKETS - 1)
    bucket = jnp.where(n < REL_MAX_EXACT, n, large)
    for hh, head in enumerate(heads):
        val = jnp.zeros((TQ, width), F32)
        for k in range(REL_BUCKETS):
            val = jnp.where(bucket == k, tbl_ref[k, head], val)
        o_ref[0, hh] = jnp.where(valid, val, NEG)


def _bias_tiles(rel_bias, *, n_tiles, heads, width=TK, row_step=TQ, col_mult=1, offset=0, dist_mult=1,
                lo=0, hi=2 ** 30, n_cols=None, name):
    n_cols = width if n_cols is None else n_cols
    nh = len(heads)
    return pl.pallas_call(
        functools.partial(_bias_tile_kernel, heads=tuple(heads), width=width, row_step=row_step,
                          col_mult=col_mult, offset=offset, dist_mult=dist_mult, lo=lo, hi=hi, n_cols=n_cols),
        out_shape=jax.ShapeDtypeStruct((n_tiles, nh, TQ, width), F32),
        grid=(n_tiles,),
        in_specs=[pl.BlockSpec(memory_space=pltpu.SMEM)],
        out_specs=pl.BlockSpec((1, nh, TQ, width), lambda d: (d, 0, 0, 0)),
        compiler_params=_cparams(1),
        name=name,
    )(rel_bias)


_DILATIONS = tuple(d for _, d in DIL_PATTERNS)
_PROJ_SEGS = (
    ("nsa_q", NSA_HEADS * SLOT, MXU_DTYPE, 1),
    ("nsa_kc", LANES, F32, NSA_CMP_STRIDE),
    ("nsa_vc", LANES, F32, NSA_CMP_STRIDE),
    ("nsa_ks", NSA_KV_HEADS * SLOT, MXU_DTYPE, 1),
    ("nsa_vs", NSA_KV_HEADS * SLOT, MXU_DTYPE, 1),
    ("nsa_kw", NSA_KV_HEADS * SLOT, MXU_DTYPE, 1),
    ("nsa_vw", NSA_KV_HEADS * SLOT, MXU_DTYPE, 1),
    ("nsa_gate", NSA_KV_HEADS * SLOT, F32, 1),
    ("moba_q", MOBA_HEADS * SLOT, MXU_DTYPE, 1),
    ("moba_k", MOBA_HEADS * SLOT, MXU_DTYPE, 1),
    ("moba_v", MOBA_HEADS * SLOT, MXU_DTYPE, 1),
) + tuple((f"dil_{n}{g}", LANES, MXU_DTYPE, d) for n in "qkv" for g, d in enumerate(_DILATIONS))
_PROJ_OFFS = np.concatenate([[0], np.cumsum([s[1] for s in _PROJ_SEGS])])
_PROJ_COLS = int(_PROJ_OFFS[-1])


def _proj_weights(w_l):
    hd = HEAD_DIM
    d_model = w_l.shape[0]
    kvw = NSA_KV_HEADS * hd
    widths = (("nsa_q", NSA_HEADS * hd), ("nsa_k_cmp", kvw), ("nsa_v_cmp", kvw), ("nsa_k_sel", kvw),
              ("nsa_v_sel", kvw), ("nsa_k_win", kvw), ("nsa_v_win", kvw), ("nsa_gate", NSA_HEADS * 3),
              ("moba_q", MOBA_HEADS * hd), ("moba_k", MOBA_HEADS * hd), ("moba_v", MOBA_HEADS * hd),
              ("dil_q", 6 * hd), ("dil_k", 6 * hd), ("dil_v", 6 * hd))
    cols, o = {}, 0
    for name, w in widths:
        cols[name] = w_l[:, o:o + w]
        o += w
    qk_scale = hd ** -0.5

    def slots(w, n, real):
        w = w.reshape(d_model, n, real)
        return jnp.pad(w, ((0, 0), (0, 0), (0, SLOT - real))).reshape(d_model, n * SLOT)

    parts = {
        "nsa_q": slots(cols["nsa_q"] * qk_scale, NSA_HEADS, hd),
        "nsa_kc": cols["nsa_k_cmp"], "nsa_vc": cols["nsa_v_cmp"],
        "nsa_ks": slots(cols["nsa_k_sel"], NSA_KV_HEADS, hd), "nsa_vs": slots(cols["nsa_v_sel"], NSA_KV_HEADS, hd),
        "nsa_kw": slots(cols["nsa_k_win"], NSA_KV_HEADS, hd), "nsa_vw": slots(cols["nsa_v_win"], NSA_KV_HEADS, hd),
        "nsa_gate": slots(cols["nsa_gate"], NSA_KV_HEADS, NSA_REP * 3),
        "moba_q": slots(cols["moba_q"] * qk_scale, MOBA_HEADS, hd),
        "moba_k": slots(cols["moba_k"], MOBA_HEADS, hd), "moba_v": slots(cols["moba_v"], MOBA_HEADS, hd),
    }
    for g in range(len(DIL_PATTERNS)):
        sl = slice(g * 2 * hd, (g + 1) * 2 * hd)
        parts[f"dil_q{g}"] = cols["dil_q"][:, sl] * qk_scale
        parts[f"dil_k{g}"] = cols["dil_k"][:, sl]
        parts[f"dil_v{g}"] = cols["dil_v"][:, sl]
    w_attn = jnp.concatenate([parts[name] for name, _, _, _ in _PROJ_SEGS], axis=1)
    return w_attn.astype(MXU_DTYPE), w_l[:, o:].astype(MXU_DTYPE)


def _proj_kernel(x_ref, g_ref, w_ref, *refs, seq, tm):
    names = [s[0] for s in _PROJ_SEGS] + ["moba_kmean"]
    outs = dict(zip(names, refs[:len(names)]))
    stage_ref = refs[len(names)]
    h = _rms(x_ref[...], g_ref[...]).astype(MXU_DTYPE)
    pos0 = (pl.program_id(0) * tm) % seq
    for si, (name, width, dtype, inter) in enumerate(_PROJ_SEGS):
        c0 = int(_PROJ_OFFS[si])
        y = _dot(h, w_ref[:, c0:c0 + width])
        if inter > 1:
            stage_ref[...] = y
            for r in range(inter):
                outs[name][:, r * LANES:(r + 1) * LANES] = stage_ref[pl.ds(r, tm // inter, stride=inter), :].astype(dtype)
            continue
        if name == "moba_k":
            nblk = tm // MOBA_BLOCK
            outs["moba_kmean"][0] = jnp.mean(y.reshape(nblk, MOBA_BLOCK, width), axis=1)
        if name in ("nsa_ks", "moba_k"):
            blk = NSA_SEL_BLOCK if name == "nsa_ks" else MOBA_BLOCK
            row = lax.broadcasted_iota(jnp.int32, (tm, width), 0)
            lane = lax.broadcasted_iota(jnp.int32, (tm, width), 1) & (SLOT - 1)
            hit = (lane - HEAD_DIM) == jnp.right_shift(pos0 + row, int(math.log2(blk)))
            y = jnp.where(hit, 1.0, y)
        if name == "nsa_gate":
            y = jax.nn.sigmoid(y)
        outs[name][...] = y.astype(dtype)


def _proj(x2, gain, w_attn, seq, tm=512):
    t, d = x2.shape
    tm = min(tm, seq)
    assert tm % MOBA_BLOCK == 0 and seq % tm == 0
    nblk = tm // MOBA_BLOCK
    out_shape = [jax.ShapeDtypeStruct((t // il, w * il), dt) for _, w, dt, il in _PROJ_SEGS]
    out_specs = [pl.BlockSpec((tm // il, w * il), lambda i: (i, 0)) for _, w, _, il in _PROJ_SEGS]
    kw = MOBA_HEADS * SLOT
    out_shape.append(jax.ShapeDtypeStruct((t // tm, nblk, kw), F32))
    out_specs.append(pl.BlockSpec((1, nblk, kw), lambda i: (i, 0, 0)))
    outs = pl.pallas_call(
        functools.partial(_proj_kernel, seq=seq, tm=tm),
        out_shape=out_shape,
        grid=(t // tm,),
        in_specs=[pl.BlockSpec((tm, d), lambda i: (i, 0)), _const_spec((1, d)), _const_spec((d, _PROJ_COLS))],
        out_specs=out_specs,
        scratch_shapes=[pltpu.VMEM((tm, LANES), F32)],
        compiler_params=_cparams(1),
        name="mixer_in_proj",
    )(x2, gain.reshape(1, d), w_attn)
    res = {s[0]: o for s, o in zip(_PROJ_SEGS, outs)}
    res["moba_kmean"] = outs[-1]
    return res


def _gelu_tanh(x):
    return 0.5 * x * (1.0 + jnp.tanh(math.sqrt(2.0 / math.pi) * (x + 0.044715 * (x * x * x))))


def _compress_kernel(k_ref, v_ref, pek_ref, pev_ref, wk1_ref, wv1_ref, wk2_ref, wv2_ref, kc_ref, vc_ref):
    nr = k_ref.shape[1]
    hid_w = NSA_CMP_HIDDEN
    for x_ref, pe_ref, w1_ref, w2_ref, o_ref in ((k_ref, pek_ref, wk1_ref, wk2_ref, kc_ref),
                                                 (v_ref, pev_ref, wv1_ref, wv2_ref, vc_ref)):
        r = x_ref[0]
        lo = _dot((r + pe_ref[0:1, :]).astype(MXU_DTYPE), w1_ref[0])
        hi = _dot((r + pe_ref[1:2, :]).astype(MXU_DTYPE), w1_ref[1])
        hid = lo + pltpu.roll(hi, nr - 1, 0)
        act = _gelu_tanh(hid).astype(MXU_DTYPE)
        for g in range(NSA_KV_HEADS):
            o_ref[0, g] = _dot(act[:, g * hid_w:(g + 1) * hid_w], w2_ref[...]).astype(o_ref.dtype)


def _compress(kc_in, vc_in, pek, pev, wk1, wv1, wk2, wv2, batch, seq):
    nr = seq // NSA_CMP_STRIDE
    rw = NSA_CMP_STRIDE * LANES
    kin = kc_in.reshape(batch, nr, rw)
    vin = vc_in.reshape(batch, nr, rw)
    hw = NSA_KV_HEADS * NSA_CMP_HIDDEN
    out = jax.ShapeDtypeStruct((batch, NSA_KV_HEADS, nr, SLOT), MXU_DTYPE)
    in_blk = pl.BlockSpec((1, nr, rw), lambda b: (b, 0, 0))
    out_blk = pl.BlockSpec((1, NSA_KV_HEADS, nr, SLOT), lambda b: (b, 0, 0, 0))
    return pl.pallas_call(
        _compress_kernel,
        out_shape=[out, out],
        grid=(batch,),
        in_specs=[in_blk, in_blk, _const_spec((2, rw)), _const_spec((2, rw)),
                  _const_spec((2, rw, hw)), _const_spec((2, rw, hw)),
                  _const_spec((NSA_CMP_HIDDEN, SLOT)), _const_spec((NSA_CMP_HIDDEN, SLOT))],
        out_specs=[out_blk, out_blk],
        compiler_params=_cparams(1),
        name="nsa_compress",
    )(kin, vin, pek, pev, wk1, wv1, wk2, wv2)


def _rank_before(score):
    n = score.shape[0]
    idx = lax.broadcasted_iota(jnp.int32, score.shape, 0)
    rank = jnp.zeros(score.shape, jnp.int32)
    for j in range(n):
        row = score[j:j + 1, :]
        ahead = (row > score) | ((row == score) & (idx > j))
        rank = rank + jnp.where(ahead, 1, 0)
    return rank


def _bias_block(t_ref, q_tile0, k_tile0, n_stack, n_qs, n_kt):
    rows = []
    for r in range(n_stack):
        for a in range(n_qs):
            base = q_tile0 + a - k_tile0 + TILE_PAD
            rows.append(jnp.concatenate([t_ref[base - j, r] for j in range(n_kt)], axis=1))
    return jnp.concatenate(rows, axis=0)


def _flash_chunks(q, k_ref, v_ref, t_ref, q_tile0, n_stack, n_qs, n_chunks, scratch):
    q_s, s_a, s_b, p_a, p_b, m_s, l_s, a_s, acc_s = scratch
    rows = q.shape[0]
    kw = KV_CHUNK * TK
    stat = (rows, LANES)

    def chunk_rows(c):
        return pl.ds(pl.multiple_of(c * kw, kw), kw)

    def logits(c):
        return (_dot_nt(q_s[...], k_ref[chunk_rows(c), :])
                + _bias_block(t_ref, q_tile0, c * KV_CHUNK, n_stack, n_qs, KV_CHUNK))

    def pv(p_ref, c):
        acc_s[...] = a_s[...] * acc_s[...] + _dot(p_ref[...], v_ref[chunk_rows(c), :])

    def step(c, s_cur, s_nxt, p_cur, p_prev):
        pv(p_prev, jnp.maximum(c - 1, 0))
        s_nxt[...] = logits(jnp.minimum(c + 1, n_chunks - 1))
        tiles = [s_cur[:, j * TK:(j + 1) * TK] for j in range(KV_CHUNK)]
        m_prev = m_s[...]
        m_new = jnp.maximum(m_prev, jnp.broadcast_to(
            jnp.max(functools.reduce(jnp.maximum, tiles), axis=-1, keepdims=True), stat))
        alpha = jnp.exp(m_prev - m_new)
        ps = [jnp.exp(t - m_new) for t in tiles]
        l_s[...] = alpha * l_s[...] + jnp.broadcast_to(
            jnp.sum(functools.reduce(jnp.add, ps), axis=-1, keepdims=True), stat)
        m_s[...] = m_new
        a_s[...] = alpha
        for j in range(KV_CHUNK):
            p_cur[:, j * TK:(j + 1) * TK] = ps[j].astype(MXU_DTYPE)

    q_s[...] = q
    m_s[...] = jnp.full(stat, M_INIT, F32)
    l_s[...] = jnp.zeros(stat, F32)
    a_s[...] = jnp.ones(stat, F32)
    acc_s[...] = jnp.zeros(stat, F32)
    p_b[...] = jnp.zeros(p_b.shape, MXU_DTYPE)
    s_a[...] = logits(0)

    def body(i, carry):
        c = 2 * i
        step(c, s_a, s_b, p_a, p_b)

        @pl.when(c + 1 < n_chunks)
        def _():
            step(c + 1, s_b, s_a, p_b, p_a)
        return carry

    lax.fori_loop(0, (n_chunks + 1) // 2, body, 0)
    last = n_chunks - 1

    @pl.when(last % 2 == 0)
    def _():
        pv(p_a, last)

    @pl.when(last % 2 == 1)
    def _():
        pv(p_b, last)
    return acc_s[...] / jnp.maximum(l_s[...], 1e-30)


def _flash_scratch(rows):
    kw = KV_CHUNK * TK
    return [pltpu.VMEM((rows, LANES), MXU_DTYPE),
            pltpu.VMEM((rows, kw), F32), pltpu.VMEM((rows, kw), F32),
            pltpu.VMEM((rows, kw), MXU_DTYPE), pltpu.VMEM((rows, kw), MXU_DTYPE)] + [pltpu.VMEM((rows, LANES), F32)] * 4


def _attend_once(q, k, v, bias):
    s = _dot_nt(q, k) + bias
    m = jnp.maximum(jnp.max(s, axis=-1, keepdims=True), M_INIT)
    p = jnp.exp(s - m)
    l = jnp.maximum(jnp.sum(p, axis=-1, keepdims=True), 1e-30)
    return _dot(p.astype(MXU_DTYPE), v) / l


def _block_mask_lanes(z_ref, sel_t, tq):
    n = sel_t.shape[0]
    z_ref[...] = jnp.zeros(z_ref.shape, F32)
    z_ref[HEAD_DIM:HEAD_DIM + n, :] = jnp.where(sel_t, 0.0, NEG)
    return z_ref[...].T


def _nsa_kernel(q_ref, gate_ref, kc_ref, vc_ref, ks_ref, vs_ref, kw_ref, vw_ref, c2s_ref,
                tcmp_ref, tsel_ref, twin_ref, o_ref, z_ref, *flash_scratch, n_slc):
    step = pl.program_id(2)
    qt0 = step * NSA_QS
    rep, nqs = NSA_REP, NSA_QS
    tqb = nqs * TQ
    q_all = q_ref[0]
    q = jnp.concatenate([q_all[a * TQ:(a + 1) * TQ, r * SLOT:(r + 1) * SLOT]
                         for r in range(rep) for a in range(nqs)], axis=0)

    n_wt = WIN_TILES + nqs
    w0 = jnp.maximum(qt0 - WIN_TILES, 0)
    w_rows = pl.ds(pl.multiple_of(w0 * TK, TK), n_wt * TK)
    o_w = _attend_once(q, kw_ref[0, w_rows, :], vw_ref[0, w_rows, :],
                       _bias_block(twin_ref, qt0, w0, rep, nqs, n_wt))

    bias_c = jnp.concatenate([tcmp_ref[a, r] for r in range(rep) for a in range(nqs)], axis=0)
    sc = _dot_nt(q, kc_ref[0, 0]) + bias_c
    mc = jnp.maximum(jnp.max(sc, axis=-1, keepdims=True), M_INIT)
    pc = jnp.exp(sc - mc)
    pc = pc / jnp.maximum(jnp.sum(pc, axis=-1, keepdims=True), 1e-30)
    o_c = _dot(pc.astype(MXU_DTYPE), vc_ref[0, 0])

    p_sum = pc[0:tqb]
    for r in range(1, rep):
        p_sum = p_sum + pc[r * tqb:(r + 1) * tqb]
    p_hi = p_sum.astype(MXU_DTYPE)
    p_lo = (p_sum - p_hi.astype(F32)).astype(MXU_DTYPE)
    imp = _dot(p_hi, c2s_ref[...]) + _dot(p_lo, c2s_ref[...])
    imp_t = imp.T[0:n_slc]
    blk = lax.broadcasted_iota(jnp.int32, (n_slc, tqb), 0)
    t_pos = qt0 * TQ + lax.broadcasted_iota(jnp.int32, (n_slc, tqb), 1)
    cur = jnp.right_shift(t_pos, int(math.log2(NSA_SEL_BLOCK)))
    forced = (blk == 0) | (blk == cur) | (blk == cur - 1)
    score = jnp.where(blk > cur, NEG, jnp.where(forced, FORCE_SCORE, imp_t))
    sel_t = _rank_before(score) < min(NSA_N_SEL, n_slc)
    zt = _block_mask_lanes(z_ref, sel_t, tqb)
    q_sel = (q.astype(F32) + jnp.concatenate([zt] * rep, axis=0)).astype(MXU_DTYPE)

    n_chunks = (qt0 + nqs - 1) // KV_CHUNK + 1
    o_s = _flash_chunks(q_sel, ks_ref.at[0], vs_ref.at[0], tsel_ref, qt0, rep, nqs, n_chunks, flash_scratch)

    gates = gate_ref[0]
    for r in range(rep):
        sl = slice(r * tqb, (r + 1) * tqb)
        o_r = (gates[:, 3 * r:3 * r + 1] * o_c[sl] + gates[:, 3 * r + 1:3 * r + 2] * o_s[sl]
               + gates[:, 3 * r + 2:3 * r + 3] * o_w[sl])
        o_ref[0, :, r * SLOT:(r + 1) * SLOT] = o_r.astype(o_ref.dtype)


def _nsa(p, kc, vc, c2s, tcmp, tmain, twin, batch, seq):
    nq = seq // TQ
    n_slc = seq // NSA_SEL_BLOCK
    tqb = NSA_QS * TQ
    assert n_slc <= HEAD_DIM and nq % KV_CHUNK == 0 and nq % NSA_QS == 0 and nq >= WIN_TILES + NSA_QS
    assert TILE_PAD >= KV_CHUNK + NSA_QS - 2
    g_n, rep = NSA_KV_HEADS, NSA_REP
    n_cmp_pad = kc.shape[2]
    r3 = lambda a: a.reshape(batch, seq, a.shape[-1])
    kv_spec = pl.BlockSpec((1, seq, SLOT), lambda b, g, i: (b, 0, g))
    cmp_spec = pl.BlockSpec((1, 1, n_cmp_pad, SLOT), lambda b, g, i: (b, g, 0, 0))
    return pl.pallas_call(
        functools.partial(_nsa_kernel, n_slc=n_slc),
        out_shape=jax.ShapeDtypeStruct((batch, seq, NSA_HEADS * SLOT), MXU_DTYPE),
        grid=(batch, g_n, nq // NSA_QS),
        in_specs=[
            pl.BlockSpec((1, tqb, rep * SLOT), lambda b, g, i: (b, i, g)),
            pl.BlockSpec((1, tqb, SLOT), lambda b, g, i: (b, i, g)),
            cmp_spec, cmp_spec, kv_spec, kv_spec, kv_spec, kv_spec,
            _const_spec(c2s.shape),
            pl.BlockSpec((NSA_QS, rep, TQ, n_cmp_pad), lambda b, g, i: (i, g, 0, 0)),
            pl.BlockSpec((tmain.shape[0], rep, TQ, TK), lambda b, g, i: (0, g, 0, 0)),
            pl.BlockSpec((twin.shape[0], rep, TQ, TK), lambda b, g, i: (0, g, 0, 0)),
        ],
        out_specs=pl.BlockSpec((1, tqb, rep * SLOT), lambda b, g, i: (b, i, g)),
        scratch_shapes=[pltpu.VMEM((LANES, tqb), F32)] + _flash_scratch(rep * tqb),
        compiler_params=_cparams(3),
        name="nsa_attention",
    )(r3(p["nsa_q"]), r3(p["nsa_gate"]), kc, vc, r3(p["nsa_ks"]), r3(p["nsa_vs"]), r3(p["nsa_kw"]),
      r3(p["nsa_vw"]), c2s.astype(MXU_DTYPE), tcmp, tmain, twin)


def _moba_kernel(q_ref, k_ref, v_ref, km_ref, t_ref, o_ref, z_ref, *flash_scratch, nb):
    step = pl.program_id(2)
    tqb = MOBA_QB * MOBA_BLOCK
    n_qs = tqb // TQ
    q = q_ref[0]
    km = km_ref[0]
    km_hi = km.astype(MXU_DTYPE)
    rem = km - km_hi.astype(F32)
    km_mid = rem.astype(MXU_DTYPE)
    km_lo = (rem - km_mid.astype(F32)).astype(MXU_DTYPE)
    gate_t = _dot_nt(km_hi, q) + _dot_nt(km_mid, q) + _dot_nt(km_lo, q)
    n_i = lax.broadcasted_iota(jnp.int32, (nb, tqb), 0)
    own = step * MOBA_QB + jnp.right_shift(lax.broadcasted_iota(jnp.int32, (nb, tqb), 1), int(math.log2(MOBA_BLOCK)))
    past = n_i < own
    score = jnp.where(past, gate_t, NEG)
    sel_t = ((_rank_before(score) < min(MOBA_TOPK, nb - 1)) & past) | (n_i == own)
    zt = _block_mask_lanes(z_ref, sel_t, tqb)
    q_sel = (q.astype(F32) + zt).astype(MXU_DTYPE)
    q_tile0 = step * n_qs
    n_chunks = (q_tile0 + n_qs - 1) // KV_CHUNK + 1
    o = _flash_chunks(q_sel, k_ref.at[0], v_ref.at[0], t_ref, q_tile0, 1, n_qs, n_chunks, flash_scratch)
    o_ref[0] = o.astype(o_ref.dtype)


def _moba(p, kmean, tmain, batch, seq):
    nq = seq // TQ
    nb = seq // MOBA_BLOCK
    tqb = MOBA_QB * MOBA_BLOCK
    assert seq % tqb == 0 and nb <= HEAD_DIM and MOBA_BLOCK % TQ == 0 and nq % KV_CHUNK == 0
    assert TILE_PAD >= KV_CHUNK + tqb // TQ - 2
    r3 = lambda a: a.reshape(batch, seq, a.shape[-1])
    kv_spec = pl.BlockSpec((1, seq, SLOT), lambda b, h, i: (b, 0, h))
    q_spec = pl.BlockSpec((1, tqb, SLOT), lambda b, h, i: (b, i, h))
    return pl.pallas_call(
        functools.partial(_moba_kernel, nb=nb),
        out_shape=jax.ShapeDtypeStruct((batch, seq, MOBA_HEADS * SLOT), MXU_DTYPE),
        grid=(batch, MOBA_HEADS, seq // tqb),
        in_specs=[q_spec, kv_spec, kv_spec,
                  pl.BlockSpec((1, nb, SLOT), lambda b, h, i: (b, 0, h)),
                  pl.BlockSpec((tmain.shape[0], 1, TQ, TK), lambda b, h, i: (0, NSA_HEADS + h, 0, 0))],
        out_specs=q_spec,
        scratch_shapes=[pltpu.VMEM((LANES, tqb), F32)] + _flash_scratch(tqb),
        compiler_params=_cparams(3),
        name="moba_attention",
    )(r3(p["moba_q"]), r3(p["moba_k"]), r3(p["moba_v"]), kmean.reshape(batch, nb, MOBA_HEADS * SLOT), tmain)


def _dilated_kernel(q_ref, k_ref, v_ref, t_ref, o_ref, lse_ref, *, n_tiles, n_res):
    lane = lax.broadcasted_iota(jnp.int32, (DIL_WB, LANES), 1)
    first = lane < HEAD_DIM
    n_kt = min(2, n_tiles)

    for res in range(n_res):
        lanes = slice(res * LANES, (res + 1) * LANES)

        def tile(n, carry, lanes=lanes):
            kv0 = jnp.maximum(n - 1, 0)
            q_rows = pl.ds(pl.multiple_of(n * DIL_WB, DIL_WB), DIL_WB)
            kv_rows = pl.ds(pl.multiple_of(kv0 * DIL_WB, DIL_WB), n_kt * DIL_WB)
            q = q_ref[0, q_rows, lanes]
            k = k_ref[0, kv_rows, lanes]
            v = v_ref[0, kv_rows, lanes]
            o_heads, lse_heads = [], []
            for h in range(DIL_HEADS_PER_GROUP):
                qh = jnp.where(first if h == 0 else ~first, q, jnp.zeros_like(q))
                bias = jnp.concatenate([t_ref[(n - kv0) - j + (0 if j == 0 else 3 * (1 - (n - kv0))), h]
                                        for j in range(n_kt)], axis=1)
                s = _dot_nt(qh, k) + bias
                m = jnp.max(s, axis=-1, keepdims=True)
                p = jnp.exp(s - m)
                l = jnp.maximum(jnp.sum(p, axis=-1, keepdims=True), 1e-30)
                o_heads.append(_dot(p.astype(MXU_DTYPE), v) / l)
                lse_heads.append(m + jnp.log(l))
            o_ref[0, q_rows, lanes] = jnp.where(first, o_heads[0], o_heads[1])
            lse_ref[0, q_rows, lanes] = jnp.where(first, lse_heads[0], lse_heads[1])
            return carry

        lax.fori_loop(0, n_tiles, tile, 0, unroll=min(DIL_UNROLL, n_tiles))


def _dilated(q, k, v, tdil, dilation, batch, seq):
    ln = seq // dilation
    assert ln % DIL_WB == 0
    view = lambda a: a.reshape(batch, ln, dilation * LANES)
    spec = pl.BlockSpec((1, ln, dilation * LANES), lambda b: (b, 0, 0))
    out = jax.ShapeDtypeStruct((batch, ln, dilation * LANES), F32)
    o, lse = pl.pallas_call(
        functools.partial(_dilated_kernel, n_tiles=ln // DIL_WB, n_res=dilation),
        out_shape=[out, out],
        grid=(batch,),
        in_specs=[spec, spec, spec, _const_spec(tdil.shape)],
        out_specs=[spec, spec],
        compiler_params=_cparams(1),
        name=f"dilated_attention_d{dilation}",
    )(view(q), view(k), view(v), tdil)
    rows = batch * ln
    return o.reshape(rows, dilation * LANES), lse.reshape(rows, dilation * LANES)


def _mixer_out_kernel(x_ref, g_ref, ya_ref, yb_ref, o0_ref, o1_ref, o2_ref, l0_ref, l1_ref, l2_ref,
                      wmg_ref, wa_ref, wb_ref, wc_ref, wo_ref, out_ref, *stage_refs):
    x = x_ref[...]
    d = x.shape[1]
    tm = x.shape[0]
    h = _rms(x, g_ref[...]).astype(MXU_DTYPE)

    def token_major(ref, stage_ref):
        il = ref.shape[1] // LANES
        if il == 1:
            return ref[...]
        for r in range(il):
            stage_ref[pl.ds(r, tm // il, stride=il), :] = ref[:, r * LANES:(r + 1) * LANES]
        return stage_ref[...]

    o0, o1, o2, l0, l1, l2 = [token_major(r, s) for r, s in
                              zip((o0_ref, o1_ref, o2_ref, l0_ref, l1_ref, l2_ref), stage_refs)]
    mx = jnp.maximum(jnp.maximum(l0, l1), l2)
    e0, e1, e2 = jnp.exp(l0 - mx), jnp.exp(l1 - mx), jnp.exp(l2 - mx)
    y_c = (e0 * o0 + e1 * o1 + e2 * o2) / (e0 + e1 + e2)
    merged = jax.nn.sigmoid(_dot(h, wmg_ref[:, 0:d])) * _dot(ya_ref[...], wa_ref[...])
    merged += jax.nn.sigmoid(_dot(h, wmg_ref[:, d:2 * d])) * _dot(yb_ref[...], wb_ref[...])
    merged += jax.nn.sigmoid(_dot(h, wmg_ref[:, 2 * d:3 * d])) * _dot(y_c.astype(MXU_DTYPE), wc_ref[...])
    out_ref[...] = x + _dot(merged.astype(MXU_DTYPE), wo_ref[...])


def _mixer_out(x2, gain, ya, yb, dil, wmg, wa, wb, wc, wo, tm=512):
    t, d = x2.shape
    tm = min(tm, t)
    row = lambda w: pl.BlockSpec((tm, w), lambda i: (i, 0))
    (o0, l0), (o1, l1), (o2, l2) = dil
    dil_arrays = (o0, o1, o2, l0, l1, l2)
    inter = lambda a: pl.BlockSpec((tm // (a.shape[1] // LANES), a.shape[1]), lambda i: (i, 0))
    return pl.pallas_call(
        _mixer_out_kernel,
        out_shape=jax.ShapeDtypeStruct((t, d), F32),
        grid=(t // tm,),
        in_specs=[row(d), _const_spec((1, d)), row(ya.shape[1]), row(yb.shape[1])] + [inter(a) for a in dil_arrays]
                 + [_const_spec(w.shape) for w in (wmg, wa, wb, wc, wo)],
        out_specs=row(d),
        scratch_shapes=[pltpu.VMEM((tm, LANES), F32)] * len(dil_arrays),
        compiler_params=_cparams(1),
        name="mixer_out_proj",
    )(x2, gain.reshape(1, d), ya, yb, o0, o1, o2, l0, l1, l2, wmg, wa, wb, wc, wo)


def _slot_rows(w, n_slots):
    d = w.shape[1]
    w = w.reshape(n_slots, HEAD_DIM, d)
    return jnp.concatenate([w, jnp.zeros_like(w)], axis=1).reshape(n_slots * SLOT, d)


def _compress_weights(pe, w1, w2):
    g_n, hd, half = NSA_KV_HEADS, HEAD_DIM, NSA_CMP_STRIDE
    hid = w1.shape[1]
    w1r = w1.reshape(2, half, hd, hid)
    eye = jnp.eye(g_n, dtype=w1.dtype)
    w1p = jnp.einsum("plch,gk->plgckh", w1r, eye).reshape(2, half * g_n * hd, g_n * hid)
    pep = jnp.broadcast_to(pe.reshape(2, half, 1, hd), (2, half, g_n, hd)).reshape(2, half * g_n * hd)
    w2p = jnp.concatenate([w2, jnp.zeros_like(w2)], axis=1)
    return pep, w1p.astype(MXU_DTYPE), w2p.astype(MXU_DTYPE)


def _cmp_to_slc(seq, n_cmp_pad):
    n_cmp = (seq - NSA_CMP_BLOCK) // NSA_CMP_STRIDE + 1
    n_slc = seq // NSA_SEL_BLOCK
    c_start = np.arange(n_cmp_pad) * NSA_CMP_STRIDE
    s_start = np.arange(LANES) * NSA_SEL_BLOCK
    ov = ((c_start[:, None] < s_start[None, :] + NSA_SEL_BLOCK) & (c_start[:, None] + NSA_CMP_BLOCK > s_start[None, :])
          & (np.arange(n_cmp_pad)[:, None] < n_cmp) & (np.arange(LANES)[None, :] < n_slc))
    return jnp.asarray(ov, F32)


def kernel(x, rel_bias, ffn1_norm, ffn1_w_gate, ffn1_w_up, ffn1_w_down, mix_norm, w_in, nsa_pe_k, nsa_pe_v,
           nsa_phi_k1, nsa_phi_k2, nsa_phi_v1, nsa_phi_v2, w_up_a, w_up_b, w_up_c, w_o, ffn2_norm, ffn2_w_gate,
           ffn2_w_up, ffn2_w_down, final_norm):
    batch, seq, d = x.shape
    depth = w_in.shape[0]
    nq = seq // TQ
    assert seq % (TQ * DIL_PATTERNS[-1][1]) == 0 and all(w // dl == DIL_WB for w, dl in DIL_PATTERNS)
    bf = lambda a: a.astype(MXU_DTYPE)

    n_cmp = (seq - NSA_CMP_BLOCK) // NSA_CMP_STRIDE + 1
    n_cmp_pad = seq // NSA_CMP_STRIDE
    a_heads = list(range(NSA_HEADS))
    ab_heads = list(range(NSA_HEADS + MOBA_HEADS))
    assert TILE_PAD >= max(WIN_TILES, KV_CHUNK - 1)
    t_main = _bias_tiles(rel_bias, n_tiles=nq + TILE_PAD, heads=ab_heads, offset=-TILE_PAD * TQ, name="bias_causal")
    t_win = _bias_tiles(rel_bias, n_tiles=WIN_TILES + NSA_QS + TILE_PAD, heads=a_heads, offset=-TILE_PAD * TQ,
                        hi=NSA_WINDOW - 1, name="bias_window")
    t_cmp = _bias_tiles(rel_bias, n_tiles=nq, heads=a_heads, width=n_cmp_pad, col_mult=NSA_CMP_STRIDE,
                        offset=-(NSA_CMP_BLOCK - 1), n_cols=n_cmp, name="bias_compressed")
    t_dil = []
    for g, (_, dilation) in enumerate(DIL_PATTERNS):
        h0 = NSA_HEADS + MOBA_HEADS + g * DIL_HEADS_PER_GROUP
        t_dil.append(_bias_tiles(rel_bias, n_tiles=3, heads=[h0, h0 + 1], dist_mult=dilation, hi=DIL_WB,
                                 name=f"bias_dilated_{g}"))
    c2s = _cmp_to_slc(seq, n_cmp_pad)

    x2 = x.reshape(batch * seq, d)
    for l in range(depth):
        last = l == depth - 1
        x2 = _ffn(x2, ffn1_norm[l], bf(ffn1_w_gate[l]), bf(ffn1_w_up[l]), bf(ffn1_w_down[l]))

        w_attn, w_merge = _proj_weights(w_in[l])
        p = _proj(x2, mix_norm[l], w_attn, seq)
        pek, wk1, wk2 = _compress_weights(nsa_pe_k[l], nsa_phi_k1[l], nsa_phi_k2[l])
        pev, wv1, wv2 = _compress_weights(nsa_pe_v[l], nsa_phi_v1[l], nsa_phi_v2[l])
        kc, vc = _compress(p["nsa_kc"], p["nsa_vc"], pek, pev, wk1, wv1, wk2, wv2, batch, seq)
        ya = _nsa(p, kc, vc, c2s, t_cmp, t_main, t_win, batch, seq)
        yb = _moba(p, p["moba_kmean"], t_main, batch, seq)
        dil = [_dilated(p[f"dil_q{g}"], p[f"dil_k{g}"], p[f"dil_v{g}"], t_dil[g], dilation, batch, seq)
               for g, (_, dilation) in enumerate(DIL_PATTERNS)]
        x2 = _mixer_out(x2, mix_norm[l], ya.reshape(batch * seq, -1), yb.reshape(batch * seq, -1), dil,
                        w_merge, bf(_slot_rows(w_up_a[l], NSA_HEADS)),
                        bf(_slot_rows(w_up_b[l], MOBA_HEADS)), bf(w_up_c[l]), bf(w_o[l]))

        x2 = _ffn(x2, ffn2_norm[l], bf(ffn2_w_gate[l]), bf(ffn2_w_up[l]), bf(ffn2_w_down[l]),
                  final_gain=final_norm if last else None)
    return x2.reshape(batch, seq, d)
```

```python
import functools
import math

import jax
import jax.numpy as jnp
import numpy as np
from jax import lax
from jax.experimental import pallas as pl
from jax.experimental.pallas import tpu as pltpu

HEAD_DIM = 64
NSA_HEADS = 6
NSA_KV_HEADS = 2
NSA_REP = NSA_HEADS // NSA_KV_HEADS
NSA_CMP_BLOCK = 32
NSA_CMP_STRIDE = 16
NSA_SEL_BLOCK = 64
NSA_N_SEL = 16
NSA_WINDOW = 512
NSA_CMP_HIDDEN = 256
MOBA_HEADS = 4
MOBA_BLOCK = 256
MOBA_TOPK = 3
DIL_PATTERNS = ((128, 1), (512, 4), (2048, 16))
DIL_HEADS_PER_GROUP = 2
DIL_WB = 128
N_BRANCHES = 3
REL_BUCKETS = 32
REL_MAX_EXACT = 16
REL_MAX_DIST = 2048
NORM_EPS = 1e-6
FORCE_SCORE = 1e4
N_HEADS_TOTAL = NSA_HEADS + MOBA_HEADS + DIL_HEADS_PER_GROUP * len(DIL_PATTERNS)

LANES = 128
SLOT = LANES
TQ = 128
TK = 128
KV_CHUNK = 4
WIN_TILES = NSA_WINDOW // TK
TILE_PAD = 10
NSA_QS = 2
MOBA_QB = 2
MOBA_HS = 2
DIL_UNROLL = 4
NEG = -1e30
M_INIT = -1e29
MXU_DTYPE = jnp.bfloat16
VMEM_LIMIT = 56 * 1024 * 1024
F32 = jnp.float32


def _cparams(n_grid, vmem=VMEM_LIMIT):
    return pltpu.CompilerParams(dimension_semantics=("arbitrary",) * n_grid, vmem_limit_bytes=vmem)


def _const_spec(shape):
    nd = len(shape)
    return pl.BlockSpec(shape, lambda *_: (0,) * nd)


def _dot(a, b):
    return jnp.dot(a, b, preferred_element_type=F32)


def _dot_nt(a, b, precision=None):
    return lax.dot_general(a, b, (((1,), (1,)), ((), ())), preferred_element_type=F32, precision=precision)


def _rms(x, gain):
    return x * lax.rsqrt(jnp.mean(x * x, axis=-1, keepdims=True) + NORM_EPS) * gain


def _ffn_kernel(x_ref, g_ref, wg_ref, wu_ref, wd_ref, *rest, ff_chunk, final_norm):
    if final_norm:
        fg_ref, o_ref, acc_ref = rest
    else:
        o_ref, acc_ref = rest
    x = x_ref[...]
    h = _rms(x, g_ref[...]).astype(MXU_DTYPE)
    d_ff = wg_ref.shape[1]
    for c in range(d_ff // ff_chunk):
        sl = slice(c * ff_chunk, (c + 1) * ff_chunk)
        a = _dot(h, wg_ref[:, sl])
        u = _dot(h, wu_ref[:, sl])
        z = (a * jax.nn.sigmoid(a) * u).astype(MXU_DTYPE)
        part = _dot(z, wd_ref[sl, :])
        if c == 0:
            acc_ref[...] = part
        else:
            acc_ref[...] += part
    y = x + 0.5 * acc_ref[...]
    if final_norm:
        y = _rms(y, fg_ref[...])
    o_ref[...] = y


def _ffn(x2, gain, wg, wu, wd, final_gain=None, tm=512):
    t, d = x2.shape
    d_ff = wg.shape[1]
    ff_chunk = 256 if d_ff % 256 == 0 else d_ff
    tm = min(tm, t)
    final = final_gain is not None
    in_specs = [pl.BlockSpec((tm, d), lambda i: (i, 0)), _const_spec((1, d)),
                _const_spec((d, d_ff)), _const_spec((d, d_ff)), _const_spec((d_ff, d))]
    args = [x2, gain.reshape(1, d), wg, wu, wd]
    if final:
        in_specs.append(_const_spec((1, d)))
        args.append(final_gain.reshape(1, d))
    return pl.pallas_call(
        functools.partial(_ffn_kernel, ff_chunk=ff_chunk, final_norm=final),
        out_shape=jax.ShapeDtypeStruct((t, d), F32),
        grid=(t // tm,),
        in_specs=in_specs,
        out_specs=pl.BlockSpec((tm, d), lambda i: (i, 0)),
        scratch_shapes=[pltpu.VMEM((tm, d), F32)],
        compiler_params=_cparams(1),
        name="ffn_swiglu",
    )(*args)


def _bias_tile_kernel(tbl_ref, o_ref, *, heads, width, row_step, col_mult, offset, dist_mult, lo, hi, n_cols):
    d = pl.program_id(0)
    i = lax.broadcasted_iota(jnp.int32, (TQ, width), 0)
    j = lax.broadcasted_iota(jnp.int32, (TQ, width), 1)
    raw = row_step * d + i - col_mult * j + offset
    valid = (raw >= lo) & (raw <= hi) & (j < n_cols)
    n = jnp.maximum(raw * dist_mult, 0)
    nf = jnp.maximum(n, REL_MAX_EXACT).astype(F32)
    large = REL_MAX_EXACT + (jnp.log(nf / REL_MAX_EXACT) / math.log(REL_MAX_DIST / REL_MAX_EXACT)
                             * (REL_BUCKETS - REL_MAX_EXACT)).astype(jnp.int32)
    large = jnp.minimum(large, REL_BUCKETS - 1)
    bucket = jnp.where(n < REL_MAX_EXACT, n, large)
    for hh, head in enumerate(heads):
        val = jnp.zeros((TQ, width), F32)
        for k in range(REL_BUCKETS):
            val = jnp.where(bucket == k, tbl_ref[k, head], val)
        o_ref[0, hh] = jnp.where(valid, val, NEG)


def _bias_tiles(rel_bias, *, n_tiles, heads, width=TK, row_step=TQ, col_mult=1, offset=0, dist_mult=1,
                lo=0, hi=2 ** 30, n_cols=None, name):
    n_cols = width if n_cols is None else n_cols
    nh = len(heads)
    return pl.pallas_call(
        functools.partial(_bias_tile_kernel, heads=tuple(heads), width=width, row_step=row_step,
                          col_mult=col_mult, offset=offset, dist_mult=dist_mult, lo=lo, hi=hi, n_cols=n_cols),
        out_shape=jax.ShapeDtypeStruct((n_tiles, nh, TQ, width), F32),
        grid=(n_tiles,),
        in_specs=[pl.BlockSpec(memory_space=pltpu.SMEM)],
        out_specs=pl.BlockSpec((1, nh, TQ, width), lambda d: (d, 0, 0, 0)),
        compiler_params=_cparams(1),
        name=name,
    )(rel_bias)


_DILATIONS = tuple(d for _, d in DIL_PATTERNS)
_PROJ_SEGS = (
    ("nsa_q", NSA_HEADS * SLOT, MXU_DTYPE, 1),
    ("nsa_kc", LANES, F32, NSA_CMP_STRIDE),
    ("nsa_vc", LANES, F32, NSA_CMP_STRIDE),
    ("nsa_ks", NSA_KV_HEADS * SLOT, MXU_DTYPE, 1),
    ("nsa_vs", NSA_KV_HEADS * SLOT, MXU_DTYPE, 1),
    ("nsa_kw", NSA_KV_HEADS * SLOT, MXU_DTYPE, 1),
    ("nsa_vw", NSA_KV_HEADS * SLOT, MXU_DTYPE, 1),
    ("nsa_gate", NSA_KV_HEADS * SLOT, F32, 1),
    ("moba_q", MOBA_HEADS * SLOT, MXU_DTYPE, 1),
    ("moba_k", MOBA_HEADS * SLOT, MXU_DTYPE, 1),
    ("moba_v", MOBA_HEADS * SLOT, MXU_DTYPE, 1),
) + tuple((f"dil_{n}{g}", LANES, MXU_DTYPE, d) for n in "qkv" for g, d in enumerate(_DILATIONS))
_PROJ_OFFS = np.concatenate([[0], np.cumsum([s[1] for s in _PROJ_SEGS])])
_PROJ_COLS = int(_PROJ_OFFS[-1])


def _proj_weights(w_l):
    hd = HEAD_DIM
    d_model = w_l.shape[0]
    kvw = NSA_KV_HEADS * hd
    widths = (("nsa_q", NSA_HEADS * hd), ("nsa_k_cmp", kvw), ("nsa_v_cmp", kvw), ("nsa_k_sel", kvw),
              ("nsa_v_sel", kvw), ("nsa_k_win", kvw), ("nsa_v_win", kvw), ("nsa_gate", NSA_HEADS * 3),
              ("moba_q", MOBA_HEADS * hd), ("moba_k", MOBA_HEADS * hd), ("moba_v", MOBA_HEADS * hd),
              ("dil_q", 6 * hd), ("dil_k", 6 * hd), ("dil_v", 6 * hd))
    cols, o = {}, 0
    for name, w in widths:
        cols[name] = w_l[:, o:o + w]
        o += w
    qk_scale = hd ** -0.5

    def slots(w, n, real):
        w = w.reshape(d_model, n, real)
        return jnp.pad(w, ((0, 0), (0, 0), (0, SLOT - real))).reshape(d_model, n * SLOT)

    parts = {
        "nsa_q": slots(cols["nsa_q"] * qk_scale, NSA_HEADS, hd),
        "nsa_kc": cols["nsa_k_cmp"], "nsa_vc": cols["nsa_v_cmp"],
        "nsa_ks": slots(cols["nsa_k_sel"], NSA_KV_HEADS, hd), "nsa_vs": slots(cols["nsa_v_sel"], NSA_KV_HEADS, hd),
        "nsa_kw": slots(cols["nsa_k_win"], NSA_KV_HEADS, hd), "nsa_vw": slots(cols["nsa_v_win"], NSA_KV_HEADS, hd),
        "nsa_gate": slots(cols["nsa_gate"], NSA_KV_HEADS, NSA_REP * 3),
        "moba_q": slots(cols["moba_q"] * qk_scale, MOBA_HEADS, hd),
        "moba_k": slots(cols["moba_k"], MOBA_HEADS, hd), "moba_v": slots(cols["moba_v"], MOBA_HEADS, hd),
    }
    for g in range(len(DIL_PATTERNS)):
        sl = slice(g * 2 * hd, (g + 1) * 2 * hd)
        parts[f"dil_q{g}"] = cols["dil_q"][:, sl] * qk_scale
        parts[f"dil_k{g}"] = cols["dil_k"][:, sl]
        parts[f"dil_v{g}"] = cols["dil_v"][:, sl]
    w_attn = jnp.concatenate([parts[name] for name, _, _, _ in _PROJ_SEGS], axis=1)
    return w_attn.astype(MXU_DTYPE), w_l[:, o:].astype(MXU_DTYPE)


def _proj_kernel(x_ref, g_ref, w_ref, *refs, seq, tm):
    names = [s[0] for s in _PROJ_SEGS] + ["moba_kmean"]
    outs = dict(zip(names, refs[:len(names)]))
    stage_ref = refs[len(names)]
    h = _rms(x_ref[...], g_ref[...]).astype(MXU_DTYPE)
    pos0 = (pl.program_id(0) * tm) % seq
    for si, (name, width, dtype, inter) in enumerate(_PROJ_SEGS):
        c0 = int(_PROJ_OFFS[si])
        y = _dot(h, w_ref[:, c0:c0 + width])
        if inter > 1:
            stage_ref[...] = y
            for r in range(inter):
                outs[name][:, r * LANES:(r + 1) * LANES] = stage_ref[pl.ds(r, tm // inter, stride=inter), :].astype(dtype)
            continue
        if name == "moba_k":
            nblk = tm // MOBA_BLOCK
            outs["moba_kmean"][0] = jnp.mean(y.reshape(nblk, MOBA_BLOCK, width), axis=1)
        if name in ("nsa_ks", "moba_k"):
            blk = NSA_SEL_BLOCK if name == "nsa_ks" else MOBA_BLOCK
            row = lax.broadcasted_iota(jnp.int32, (tm, width), 0)
            lane = lax.broadcasted_iota(jnp.int32, (tm, width), 1) & (SLOT - 1)
            hit = (lane - HEAD_DIM) == jnp.right_shift(pos0 + row, int(math.log2(blk)))
            y = jnp.where(hit, 1.0, y)
        if name in ("nsa_vs", "nsa_vw", "moba_v"):
            lane = lax.broadcasted_iota(jnp.int32, (tm, width), 1) & (SLOT - 1)
            y = jnp.where(lane == HEAD_DIM, 1.0, y)
        if name == "nsa_gate":
            y = jax.nn.sigmoid(y)
        outs[name][...] = y.astype(dtype)


def _proj(x2, gain, w_attn, seq, tm=512):
    t, d = x2.shape
    tm = min(tm, seq)
    assert tm % MOBA_BLOCK == 0 and seq % tm == 0
    nblk = tm // MOBA_BLOCK
    out_shape = [jax.ShapeDtypeStruct((t // il, w * il), dt) for _, w, dt, il in _PROJ_SEGS]
    out_specs = [pl.BlockSpec((tm // il, w * il), lambda i: (i, 0)) for _, w, _, il in _PROJ_SEGS]
    kw = MOBA_HEADS * SLOT
    out_shape.append(jax.ShapeDtypeStruct((t // tm, nblk, kw), F32))
    out_specs.append(pl.BlockSpec((1, nblk, kw), lambda i: (i, 0, 0)))
    outs = pl.pallas_call(
        functools.partial(_proj_kernel, seq=seq, tm=tm),
        out_shape=out_shape,
        grid=(t // tm,),
        in_specs=[pl.BlockSpec((tm, d), lambda i: (i, 0)), _const_spec((1, d)), _const_spec((d, _PROJ_COLS))],
        out_specs=out_specs,
        scratch_shapes=[pltpu.VMEM((tm, LANES), F32)],
        compiler_params=_cparams(1),
        name="mixer_in_proj",
    )(x2, gain.reshape(1, d), w_attn)
    res = {s[0]: o for s, o in zip(_PROJ_SEGS, outs)}
    res["moba_kmean"] = outs[-1]
    return res


def _gelu_tanh(x):
    return 0.5 * x * (1.0 + jnp.tanh(math.sqrt(2.0 / math.pi) * (x + 0.044715 * (x * x * x))))


def _compress_kernel(k_ref, v_ref, pek_ref, pev_ref, wk1_ref, wv1_ref, wk2_ref, wv2_ref, kc_ref, vc_ref):
    nr = k_ref.shape[1]
    hid_w = NSA_CMP_HIDDEN
    for x_ref, pe_ref, w1_ref, w2_ref, o_ref in ((k_ref, pek_ref, wk1_ref, wk2_ref, kc_ref),
                                                 (v_ref, pev_ref, wv1_ref, wv2_ref, vc_ref)):
        r = x_ref[0]
        lo = _dot((r + pe_ref[0:1, :]).astype(MXU_DTYPE), w1_ref[0])
        hi = _dot((r + pe_ref[1:2, :]).astype(MXU_DTYPE), w1_ref[1])
        hid = lo + pltpu.roll(hi, nr - 1, 0)
        act = _gelu_tanh(hid).astype(MXU_DTYPE)
        for g in range(NSA_KV_HEADS):
            o_ref[0, g] = _dot(act[:, g * hid_w:(g + 1) * hid_w], w2_ref[...]).astype(o_ref.dtype)


def _compress(kc_in, vc_in, pek, pev, wk1, wv1, wk2, wv2, batch, seq):
    nr = seq // NSA_CMP_STRIDE
    rw = NSA_CMP_STRIDE * LANES
    kin = kc_in.reshape(batch, nr, rw)
    vin = vc_in.reshape(batch, nr, rw)
    hw = NSA_KV_HEADS * NSA_CMP_HIDDEN
    out = jax.ShapeDtypeStruct((batch, NSA_KV_HEADS, nr, SLOT), MXU_DTYPE)
    in_blk = pl.BlockSpec((1, nr, rw), lambda b: (b, 0, 0))
    out_blk = pl.BlockSpec((1, NSA_KV_HEADS, nr, SLOT), lambda b: (b, 0, 0, 0))
    return pl.pallas_call(
        _compress_kernel,
        out_shape=[out, out],
        grid=(batch,),
        in_specs=[in_blk, in_blk, _const_spec((2, rw)), _const_spec((2, rw)),
                  _const_spec((2, rw, hw)), _const_spec((2, rw, hw)),
                  _const_spec((NSA_CMP_HIDDEN, SLOT)), _const_spec((NSA_CMP_HIDDEN, SLOT))],
        out_specs=[out_blk, out_blk],
        compiler_params=_cparams(1),
        name="nsa_compress",
    )(kin, vin, pek, pev, wk1, wv1, wk2, wv2)


def _rank_before(score):
    n = score.shape[0]
    idx = lax.broadcasted_iota(jnp.int32, score.shape, 0)
    rank = jnp.zeros(score.shape, jnp.int32)
    for j in range(n):
        row = score[j:j + 1, :]
        ahead = (row > score) | ((row == score) & (idx > j))
        rank = rank + jnp.where(ahead, 1, 0)
    return rank


def _bias_block(t_ref, q_tile0, k_tile0, n_stack, n_qs, n_kt):
    rows = []
    for r in range(n_stack):
        for a in range(n_qs):
            base = q_tile0 + a - k_tile0 + TILE_PAD
            rows.append(jnp.concatenate([t_ref[base - j, r] for j in range(n_kt)], axis=1))
    return jnp.concatenate(rows, axis=0)


def _flash_chunks(q, k_ref, v_ref, t_ref, q_tile0, n_stack, n_qs, n_chunks, scratch, max_in_producer=True):
    q_s, s_a, s_b, x_a, x_b, p_a, p_b, m_s, a_s, acc_s = scratch
    rows = q.shape[0]
    kw = KV_CHUNK * TK
    stat = (rows, LANES)

    def chunk_rows(c):
        return pl.ds(pl.multiple_of(c * kw, kw), kw)

    k_refs = k_ref if isinstance(k_ref, (list, tuple)) else [k_ref]
    v_refs = v_ref if isinstance(v_ref, (list, tuple)) else [v_ref]
    grp = rows // len(k_refs)
    groups = [slice(i * grp, (i + 1) * grp) for i in range(len(k_refs))]

    def logits_into(c, s_ref, x_ref):
        bias = _bias_block(t_ref, q_tile0, c * KV_CHUNK, n_stack, n_qs, KV_CHUNK)
        for rows_g, kg in zip(groups, k_refs):
            s = _dot_nt(q_s[rows_g, :], kg[chunk_rows(c), :]) + bias[rows_g]
            s_ref[rows_g, :] = s
            if max_in_producer:
                x_ref[rows_g, :] = row_max(s)

    def row_max(s):
        tiles = [s[:, j * TK:(j + 1) * TK] for j in range(KV_CHUNK)]
        return jnp.broadcast_to(jnp.max(functools.reduce(jnp.maximum, tiles), axis=-1, keepdims=True),
                                (s.shape[0], LANES))

    def pv(p_ref, c):
        for rows_g, vg in zip(groups, v_refs):
            acc_s[rows_g, :] = a_s[rows_g, :] * acc_s[rows_g, :] + _dot(p_ref[rows_g, :], vg[chunk_rows(c), :])

    def step(c, s_cur, x_cur, s_nxt, x_nxt, p_cur, p_prev):
        pv(p_prev, jnp.maximum(c - 1, 0))
        logits_into(jnp.minimum(c + 1, n_chunks - 1), s_nxt, x_nxt)
        m_prev = m_s[...]
        m_new = jnp.maximum(m_prev, x_cur[...] if max_in_producer else row_max(s_cur[...]))
        a_s[...] = jnp.exp(m_prev - m_new)
        m_s[...] = m_new
        for j in range(KV_CHUNK):
            p_cur[:, j * TK:(j + 1) * TK] = jnp.exp(s_cur[:, j * TK:(j + 1) * TK] - m_new).astype(MXU_DTYPE)

    q_s[...] = q
    m_s[...] = jnp.full(stat, M_INIT, F32)
    a_s[...] = jnp.ones(stat, F32)
    acc_s[...] = jnp.zeros(stat, F32)
    p_b[...] = jnp.zeros(p_b.shape, MXU_DTYPE)
    logits_into(0, s_a, x_a)

    def body(i, carry):
        c = 2 * i
        step(c, s_a, x_a, s_b, x_b, p_a, p_b)

        @pl.when(c + 1 < n_chunks)
        def _():
            step(c + 1, s_b, x_b, s_a, x_a, p_b, p_a)
        return carry

    lax.fori_loop(0, (n_chunks + 1) // 2, body, 0)
    last = n_chunks - 1

    @pl.when(last % 2 == 0)
    def _():
        pv(p_a, last)

    @pl.when(last % 2 == 1)
    def _():
        pv(p_b, last)
    return _normalize(acc_s[...])


def _flash_scratch(rows):
    kw = KV_CHUNK * TK
    return ([pltpu.VMEM((rows, LANES), MXU_DTYPE)]
            + [pltpu.VMEM((rows, kw), F32)] * 2 + [pltpu.VMEM((rows, LANES), F32)] * 2
            + [pltpu.VMEM((rows, kw), MXU_DTYPE)] * 2 + [pltpu.VMEM((rows, LANES), F32)] * 3)


def _normalize(acc):
    return acc / jnp.maximum(acc[:, HEAD_DIM:HEAD_DIM + 1], 1e-30)


def _attend_once(q, k, v, bias):
    s = _dot_nt(q, k) + bias
    m = jnp.maximum(jnp.max(s, axis=-1, keepdims=True), M_INIT)
    p = jnp.exp(s - m)
    return _normalize(_dot(p.astype(MXU_DTYPE), v))


def _block_mask_lanes(z_ref, sel_t, tq):
    n = sel_t.shape[0]
    z_ref[...] = jnp.zeros(z_ref.shape, F32)
    z_ref[HEAD_DIM:HEAD_DIM + n, :] = jnp.where(sel_t, 0.0, NEG)
    return z_ref[...].T


def _nsa_kernel(q_ref, gate_ref, kc_ref, vc_ref, ks_ref, vs_ref, kw_ref, vw_ref, c2s_ref,
                tcmp_ref, tsel_ref, twin_ref, o_ref, z_ref, *flash_scratch, n_slc):
    step = pl.program_id(2)
    qt0 = step * NSA_QS
    rep, nqs = NSA_REP, NSA_QS
    tqb = nqs * TQ
    q_all = q_ref[0]
    q = jnp.concatenate([q_all[a * TQ:(a + 1) * TQ, r * SLOT:(r + 1) * SLOT]
                         for r in range(rep) for a in range(nqs)], axis=0)

    n_wt = WIN_TILES + nqs
    w0 = jnp.maximum(qt0 - WIN_TILES, 0)
    w_rows = pl.ds(pl.multiple_of(w0 * TK, TK), n_wt * TK)
    o_w = _attend_once(q, kw_ref[0, w_rows, :], vw_ref[0, w_rows, :],
                       _bias_block(twin_ref, qt0, w0, rep, nqs, n_wt))

    bias_c = jnp.concatenate([tcmp_ref[a, r] for r in range(rep) for a in range(nqs)], axis=0)
    sc = _dot_nt(q, kc_ref[0, 0]) + bias_c
    mc = jnp.maximum(jnp.max(sc, axis=-1, keepdims=True), M_INIT)
    pc = jnp.exp(sc - mc)
    pc = pc / jnp.maximum(jnp.sum(pc, axis=-1, keepdims=True), 1e-30)
    o_c = _dot(pc.astype(MXU_DTYPE), vc_ref[0, 0])

    p_sum = pc[0:tqb]
    for r in range(1, rep):
        p_sum = p_sum + pc[r * tqb:(r + 1) * tqb]
    p_hi = p_sum.astype(MXU_DTYPE)
    p_lo = (p_sum - p_hi.astype(F32)).astype(MXU_DTYPE)
    imp = _dot(p_hi, c2s_ref[...]) + _dot(p_lo, c2s_ref[...])
    imp_t = imp.T[0:n_slc]
    blk = lax.broadcasted_iota(jnp.int32, (n_slc, tqb), 0)
    t_pos = qt0 * TQ + lax.broadcasted_iota(jnp.int32, (n_slc, tqb), 1)
    cur = jnp.right_shift(t_pos, int(math.log2(NSA_SEL_BLOCK)))
    forced = (blk == 0) | (blk == cur) | (blk == cur - 1)
    score = jnp.where(blk > cur, NEG, jnp.where(forced, FORCE_SCORE, imp_t))
    sel_t = _rank_before(score) < min(NSA_N_SEL, n_slc)
    zt = _block_mask_lanes(z_ref, sel_t, tqb)
    q_sel = (q.astype(F32) + jnp.concatenate([zt] * rep, axis=0)).astype(MXU_DTYPE)

    n_chunks = (qt0 + nqs - 1) // KV_CHUNK + 1
    o_s = _flash_chunks(q_sel, ks_ref.at[0], vs_ref.at[0], tsel_ref, qt0, rep, nqs, n_chunks, flash_scratch)

    gates = gate_ref[0]
    for r in range(rep):
        sl = slice(r * tqb, (r + 1) * tqb)
        o_r = (gates[:, 3 * r:3 * r + 1] * o_c[sl] + gates[:, 3 * r + 1:3 * r + 2] * o_s[sl]
               + gates[:, 3 * r + 2:3 * r + 3] * o_w[sl])
        o_ref[0, :, r * SLOT:(r + 1) * SLOT] = o_r.astype(o_ref.dtype)


def _nsa(p, kc, vc, c2s, tcmp, tmain, twin, batch, seq):
    nq = seq // TQ
    n_slc = seq // NSA_SEL_BLOCK
    tqb = NSA_QS * TQ
    assert n_slc <= HEAD_DIM and nq % KV_CHUNK == 0 and nq % NSA_QS == 0 and nq >= WIN_TILES + NSA_QS
    assert TILE_PAD >= KV_CHUNK + NSA_QS - 2
    g_n, rep = NSA_KV_HEADS, NSA_REP
    n_cmp_pad = kc.shape[2]
    r3 = lambda a: a.reshape(batch, seq, a.shape[-1])
    kv_spec = pl.BlockSpec((1, seq, SLOT), lambda b, g, i: (b, 0, g))
    cmp_spec = pl.BlockSpec((1, 1, n_cmp_pad, SLOT), lambda b, g, i: (b, g, 0, 0))
    return pl.pallas_call(
        functools.partial(_nsa_kernel, n_slc=n_slc),
        out_shape=jax.ShapeDtypeStruct((batch, seq, NSA_HEADS * SLOT), MXU_DTYPE),
        grid=(batch, g_n, nq // NSA_QS),
        in_specs=[
            pl.BlockSpec((1, tqb, rep * SLOT), lambda b, g, i: (b, i, g)),
            pl.BlockSpec((1, tqb, SLOT), lambda b, g, i: (b, i, g)),
            cmp_spec, cmp_spec, kv_spec, kv_spec, kv_spec, kv_spec,
            _const_spec(c2s.shape),
            pl.BlockSpec((NSA_QS, rep, TQ, n_cmp_pad), lambda b, g, i: (i, g, 0, 0)),
            pl.BlockSpec((tmain.shape[0], rep, TQ, TK), lambda b, g, i: (0, g, 0, 0),
                         pipeline_mode=pl.Buffered(1)),
            pl.BlockSpec((twin.shape[0], rep, TQ, TK), lambda b, g, i: (0, g, 0, 0),
                         pipeline_mode=pl.Buffered(1)),
        ],
        out_specs=pl.BlockSpec((1, tqb, rep * SLOT), lambda b, g, i: (b, i, g)),
        scratch_shapes=[pltpu.VMEM((LANES, tqb), F32)] + _flash_scratch(rep * tqb),
        compiler_params=_cparams(3),
        name="nsa_attention",
    )(r3(p["nsa_q"]), r3(p["nsa_gate"]), kc, vc, r3(p["nsa_ks"]), r3(p["nsa_vs"]), r3(p["nsa_kw"]),
      r3(p["nsa_vw"]), c2s.astype(MXU_DTYPE), tcmp, tmain, twin)


def _moba_kernel(q_ref, *refs, nb):
    hs = MOBA_HS
    k_refs, v_refs = refs[0:hs], refs[hs:2 * hs]
    km_ref, t_ref, o_ref, z_ref = refs[2 * hs:2 * hs + 4]
    flash_scratch = refs[2 * hs + 4:]
    step = pl.program_id(2)
    tqb = MOBA_QB * MOBA_BLOCK
    n_qs = tqb // TQ
    n_i = lax.broadcasted_iota(jnp.int32, (nb, tqb), 0)
    own = step * MOBA_QB + jnp.right_shift(lax.broadcasted_iota(jnp.int32, (nb, tqb), 1), int(math.log2(MOBA_BLOCK)))
    past = n_i < own
    q_sel = []
    for h in range(hs):
        q = q_ref[0, :, h * SLOT:(h + 1) * SLOT]
        km = km_ref[0, :, h * SLOT:(h + 1) * SLOT]
        km_hi = km.astype(MXU_DTYPE)
        rem = km - km_hi.astype(F32)
        km_mid = rem.astype(MXU_DTYPE)
        km_lo = (rem - km_mid.astype(F32)).astype(MXU_DTYPE)
        gate_t = _dot_nt(km_hi, q) + _dot_nt(km_mid, q) + _dot_nt(km_lo, q)
        score = jnp.where(past, gate_t, NEG)
        sel_t = ((_rank_before(score) < min(MOBA_TOPK, nb - 1)) & past) | (n_i == own)
        zt = _block_mask_lanes(z_ref.at[h], sel_t, tqb)
        q_sel.append((q.astype(F32) + zt).astype(MXU_DTYPE))
    q_tile0 = step * n_qs
    n_chunks = (q_tile0 + n_qs - 1) // KV_CHUNK + 1
    o = _flash_chunks(jnp.concatenate(q_sel, axis=0), [r.at[0] for r in k_refs], [r.at[0] for r in v_refs], t_ref,
                      q_tile0, hs, n_qs, n_chunks, flash_scratch, max_in_producer=False)
    for h in range(hs):
        o_ref[0, :, h * SLOT:(h + 1) * SLOT] = o[h * tqb:(h + 1) * tqb].astype(o_ref.dtype)


def _moba(p, kmean, tmain, batch, seq):
    nq = seq // TQ
    nb = seq // MOBA_BLOCK
    tqb = MOBA_QB * MOBA_BLOCK
    hs = MOBA_HS
    assert seq % tqb == 0 and nb <= HEAD_DIM and MOBA_BLOCK % TQ == 0 and nq % KV_CHUNK == 0
    assert TILE_PAD >= KV_CHUNK + tqb // TQ - 2 and MOBA_HEADS % hs == 0 and NSA_HEADS % hs == 0
    r3 = lambda a: a.reshape(batch, seq, a.shape[-1])
    kv_specs = [pl.BlockSpec((1, seq, SLOT), lambda b, g, i, h=h: (b, 0, g * hs + h)) for h in range(hs)]
    q_spec = pl.BlockSpec((1, tqb, hs * SLOT), lambda b, g, i: (b, i, g))
    return pl.pallas_call(
        functools.partial(_moba_kernel, nb=nb),
        out_shape=jax.ShapeDtypeStruct((batch, seq, MOBA_HEADS * SLOT), MXU_DTYPE),
        grid=(batch, MOBA_HEADS // hs, seq // tqb),
        in_specs=[q_spec] + kv_specs + kv_specs
                 + [pl.BlockSpec((1, nb, hs * SLOT), lambda b, g, i: (b, 0, g)),
                    pl.BlockSpec((tmain.shape[0], hs, TQ, TK), lambda b, g, i: (0, NSA_HEADS // hs + g, 0, 0))],
        out_specs=q_spec,
        scratch_shapes=[pltpu.VMEM((hs, LANES, tqb), F32)] + _flash_scratch(hs * tqb),
        compiler_params=_cparams(3),
        name="moba_attention",
    )(r3(p["moba_q"]), *([r3(p["moba_k"])] * hs), *([r3(p["moba_v"])] * hs),
      kmean.reshape(batch, nb, MOBA_HEADS * SLOT), tmain)


def _dilated_kernel(q_ref, k_ref, v_ref, t_ref, o_ref, lse_ref, *, n_tiles, n_res):
    lane = lax.broadcasted_iota(jnp.int32, (DIL_WB, LANES), 1)
    first = lane < HEAD_DIM
    n_kt = min(2, n_tiles)

    for res in range(n_res):
        lanes = slice(res * LANES, (res + 1) * LANES)

        def tile(n, carry, lanes=lanes):
            kv0 = jnp.maximum(n - 1, 0)
            q_rows = pl.ds(pl.multiple_of(n * DIL_WB, DIL_WB), DIL_WB)
            kv_rows = pl.ds(pl.multiple_of(kv0 * DIL_WB, DIL_WB), n_kt * DIL_WB)
            q = q_ref[0, q_rows, lanes]
            k = k_ref[0, kv_rows, lanes]
            v = v_ref[0, kv_rows, lanes]
            o_heads, lse_heads = [], []
            for h in range(DIL_HEADS_PER_GROUP):
                qh = jnp.where(first if h == 0 else ~first, q, jnp.zeros_like(q))
                bias = jnp.concatenate([t_ref[(n - kv0) - j + (0 if j == 0 else 3 * (1 - (n - kv0))), h]
                                        for j in range(n_kt)], axis=1)
                s = _dot_nt(qh, k) + bias
                m = jnp.max(s, axis=-1, keepdims=True)
                p = jnp.exp(s - m)
                l = jnp.maximum(jnp.sum(p, axis=-1, keepdims=True), 1e-30)
                o_heads.append(_dot(p.astype(MXU_DTYPE), v) / l)
                lse_heads.append(m + jnp.log(l))
            o_ref[0, q_rows, lanes] = jnp.where(first, o_heads[0], o_heads[1])
            lse_ref[0, q_rows, lanes] = jnp.where(first, lse_heads[0], lse_heads[1])
            return carry

        lax.fori_loop(0, n_tiles, tile, 0, unroll=min(DIL_UNROLL, n_tiles))


def _dilated(q, k, v, tdil, dilation, batch, seq):
    ln = seq // dilation
    assert ln % DIL_WB == 0
    view = lambda a: a.reshape(batch, ln, dilation * LANES)
    spec = pl.BlockSpec((1, ln, dilation * LANES), lambda b: (b, 0, 0))
    out = jax.ShapeDtypeStruct((batch, ln, dilation * LANES), F32)
    o, lse = pl.pallas_call(
        functools.partial(_dilated_kernel, n_tiles=ln // DIL_WB, n_res=dilation),
        out_shape=[out, out],
        grid=(batch,),
        in_specs=[spec, spec, spec, _const_spec(tdil.shape)],
        out_specs=[spec, spec],
        compiler_params=_cparams(1),
        name=f"dilated_attention_d{dilation}",
    )(view(q), view(k), view(v), tdil)
    rows = batch * ln
    return o.reshape(rows, dilation * LANES), lse.reshape(rows, dilation * LANES)


def _mixer_out_kernel(x_ref, g_ref, ya_ref, yb_ref, o0_ref, o1_ref, o2_ref, l0_ref, l1_ref, l2_ref,
                      wmg_ref, wa_ref, wb_ref, wc_ref, wo_ref, out_ref, *stage_refs):
    x = x_ref[...]
    d = x.shape[1]
    tm = x.shape[0]
    h = _rms(x, g_ref[...]).astype(MXU_DTYPE)

    def token_major(ref, stage_ref):
        il = ref.shape[1] // LANES
        if il == 1:
            return ref[...]
        for r in range(il):
            stage_ref[pl.ds(r, tm // il, stride=il), :] = ref[:, r * LANES:(r + 1) * LANES]
        return stage_ref[...]

    o0, o1, o2, l0, l1, l2 = [token_major(r, s) for r, s in
                              zip((o0_ref, o1_ref, o2_ref, l0_ref, l1_ref, l2_ref), stage_refs)]
    mx = jnp.maximum(jnp.maximum(l0, l1), l2)
    e0, e1, e2 = jnp.exp(l0 - mx), jnp.exp(l1 - mx), jnp.exp(l2 - mx)
    y_c = (e0 * o0 + e1 * o1 + e2 * o2) / (e0 + e1 + e2)
    merged = jax.nn.sigmoid(_dot(h, wmg_ref[:, 0:d])) * _dot(ya_ref[...], wa_ref[...])
    merged += jax.nn.sigmoid(_dot(h, wmg_ref[:, d:2 * d])) * _dot(yb_ref[...], wb_ref[...])
    merged += jax.nn.sigmoid(_dot(h, wmg_ref[:, 2 * d:3 * d])) * _dot(y_c.astype(MXU_DTYPE), wc_ref[...])
    out_ref[...] = x + _dot(merged.astype(MXU_DTYPE), wo_ref[...])


def _mixer_out(x2, gain, ya, yb, dil, wmg, wa, wb, wc, wo, tm=512):
    t, d = x2.shape
    tm = min(tm, t)
    row = lambda w: pl.BlockSpec((tm, w), lambda i: (i, 0))
    (o0, l0), (o1, l1), (o2, l2) = dil
    dil_arrays = (o0, o1, o2, l0, l1, l2)
    inter = lambda a: pl.BlockSpec((tm // (a.shape[1] // LANES), a.shape[1]), lambda i: (i, 0))
    return pl.pallas_call(
        _mixer_out_kernel,
        out_shape=jax.ShapeDtypeStruct((t, d), F32),
        grid=(t // tm,),
        in_specs=[row(d), _const_spec((1, d)), row(ya.shape[1]), row(yb.shape[1])] + [inter(a) for a in dil_arrays]
                 + [_const_spec(w.shape) for w in (wmg, wa, wb, wc, wo)],
        out_specs=row(d),
        scratch_shapes=[pltpu.VMEM((tm, LANES), F32)] * len(dil_arrays),
        compiler_params=_cparams(1),
        name="mixer_out_proj",
    )(x2, gain.reshape(1, d), ya, yb, o0, o1, o2, l0, l1, l2, wmg, wa, wb, wc, wo)


def _slot_rows(w, n_slots):
    d = w.shape[1]
    w = w.reshape(n_slots, HEAD_DIM, d)
    return jnp.concatenate([w, jnp.zeros_like(w)], axis=1).reshape(n_slots * SLOT, d)


def _compress_weights(pe, w1, w2):
    g_n, hd, half = NSA_KV_HEADS, HEAD_DIM, NSA_CMP_STRIDE
    hid = w1.shape[1]
    w1r = w1.reshape(2, half, hd, hid)
    eye = jnp.eye(g_n, dtype=w1.dtype)
    w1p = jnp.einsum("plch,gk->plgckh", w1r, eye).reshape(2, half * g_n * hd, g_n * hid)
    pep = jnp.broadcast_to(pe.reshape(2, half, 1, hd), (2, half, g_n, hd)).reshape(2, half * g_n * hd)
    w2p = jnp.concatenate([w2, jnp.zeros_like(w2)], axis=1)
    return pep, w1p.astype(MXU_DTYPE), w2p.astype(MXU_DTYPE)


def _cmp_to_slc(seq, n_cmp_pad):
    n_cmp = (seq - NSA_CMP_BLOCK) // NSA_CMP_STRIDE + 1
    n_slc = seq // NSA_SEL_BLOCK
    c_start = np.arange(n_cmp_pad) * NSA_CMP_STRIDE
    s_start = np.arange(LANES) * NSA_SEL_BLOCK
    ov = ((c_start[:, None] < s_start[None, :] + NSA_SEL_BLOCK) & (c_start[:, None] + NSA_CMP_BLOCK > s_start[None, :])
          & (np.arange(n_cmp_pad)[:, None] < n_cmp) & (np.arange(LANES)[None, :] < n_slc))
    return jnp.asarray(ov, F32)


def kernel(x, rel_bias, ffn1_norm, ffn1_w_gate, ffn1_w_up, ffn1_w_down, mix_norm, w_in, nsa_pe_k, nsa_pe_v,
           nsa_phi_k1, nsa_phi_k2, nsa_phi_v1, nsa_phi_v2, w_up_a, w_up_b, w_up_c, w_o, ffn2_norm, ffn2_w_gate,
           ffn2_w_up, ffn2_w_down, final_norm):
    batch, seq, d = x.shape
    depth = w_in.shape[0]
    nq = seq // TQ
    assert seq % (TQ * DIL_PATTERNS[-1][1]) == 0 and all(w // dl == DIL_WB for w, dl in DIL_PATTERNS)
    bf = lambda a: a.astype(MXU_DTYPE)

    n_cmp = (seq - NSA_CMP_BLOCK) // NSA_CMP_STRIDE + 1
    n_cmp_pad = seq // NSA_CMP_STRIDE
    a_heads = list(range(NSA_HEADS))
    ab_heads = list(range(NSA_HEADS + MOBA_HEADS))
    assert TILE_PAD >= max(WIN_TILES, KV_CHUNK - 1)
    t_main = _bias_tiles(rel_bias, n_tiles=nq + TILE_PAD, heads=ab_heads, offset=-TILE_PAD * TQ, name="bias_causal")
    t_win = _bias_tiles(rel_bias, n_tiles=WIN_TILES + NSA_QS + TILE_PAD, heads=a_heads, offset=-TILE_PAD * TQ,
                        hi=NSA_WINDOW - 1, name="bias_window")
    t_cmp = _bias_tiles(rel_bias, n_tiles=nq, heads=a_heads, width=n_cmp_pad, col_mult=NSA_CMP_STRIDE,
                        offset=-(NSA_CMP_BLOCK - 1), n_cols=n_cmp, name="bias_compressed")
    t_dil = []
    for g, (_, dilation) in enumerate(DIL_PATTERNS):
        h0 = NSA_HEADS + MOBA_HEADS + g * DIL_HEADS_PER_GROUP
        t_dil.append(_bias_tiles(rel_bias, n_tiles=3, heads=[h0, h0 + 1], dist_mult=dilation, hi=DIL_WB,
                                 name=f"bias_dilated_{g}"))
    c2s = _cmp_to_slc(seq, n_cmp_pad)

    x2 = x.reshape(batch * seq, d)
    for l in range(depth):
        last = l == depth - 1
        x2 = _ffn(x2, ffn1_norm[l], bf(ffn1_w_gate[l]), bf(ffn1_w_up[l]), bf(ffn1_w_down[l]))

        w_attn, w_merge = _proj_weights(w_in[l])
        p = _proj(x2, mix_norm[l], w_attn, seq)
        pek, wk1, wk2 = _compress_weights(nsa_pe_k[l], nsa_phi_k1[l], nsa_phi_k2[l])
        pev, wv1, wv2 = _compress_weights(nsa_pe_v[l], nsa_phi_v1[l], nsa_phi_v2[l])
        kc, vc = _compress(p["nsa_kc"], p["nsa_vc"], pek, pev, wk1, wv1, wk2, wv2, batch, seq)
        ya = _nsa(p, kc, vc, c2s, t_cmp, t_main, t_win, batch, seq)
        yb = _moba(p, p["moba_kmean"], t_main, batch, seq)
        dil = [_dilated(p[f"dil_q{g}"], p[f"dil_k{g}"], p[f"dil_v{g}"], t_dil[g], dilation, batch, seq)
               for g, (_, dilation) in enumerate(DIL_PATTERNS)]
        x2 = _mixer_out(x2, mix_norm[l], ya.reshape(batch * seq, -1), yb.reshape(batch * seq, -1), dil,
                        w_merge, bf(_slot_rows(w_up_a[l], NSA_HEADS)),
                        bf(_slot_rows(w_up_b[l], MOBA_HEADS)), bf(w_up_c[l]), bf(w_o[l]))

        x2 = _ffn(x2, ffn2_norm[l], bf(ffn2_w_gate[l]), bf(ffn2_w_up[l]), bf(ffn2_w_down[l]),
                  final_gain=final_norm if last else None)
    return x2.reshape(batch, seq, d)
```

```python
import functools
import math

import jax
import jax.numpy as jnp
import numpy as np
from jax import lax
from jax.experimental import pallas as pl
from jax.experimental.pallas import tpu as pltpu

HEAD_DIM = 64
NSA_HEADS = 6
NSA_KV_HEADS = 2
NSA_REP = NSA_HEADS // NSA_KV_HEADS
NSA_CMP_BLOCK = 32
NSA_CMP_STRIDE = 16
NSA_SEL_BLOCK = 64
NSA_N_SEL = 16
NSA_WINDOW = 512
NSA_CMP_HIDDEN = 256
MOBA_HEADS = 4
MOBA_BLOCK = 256
MOBA_TOPK = 3
DIL_PATTERNS = ((128, 1), (512, 4), (2048, 16))
DIL_HEADS_PER_GROUP = 2
DIL_WB = 128
N_BRANCHES = 3
REL_BUCKETS = 32
REL_MAX_EXACT = 16
REL_MAX_DIST = 2048
NORM_EPS = 1e-6
FORCE_SCORE = 1e4
N_HEADS_TOTAL = NSA_HEADS + MOBA_HEADS + DIL_HEADS_PER_GROUP * len(DIL_PATTERNS)

LANES = 128
SLOT = LANES
TQ = 128
TK = 128
KV_CHUNK = 4
WIN_TILES = NSA_WINDOW // TK
TILE_PAD = 10
NSA_QS = 2
MOBA_QB = 2
MOBA_HS = 2
DIL_UNROLL = 4
NEG = -1e30
M_INIT = -1e29
MXU_DTYPE = jnp.bfloat16
VMEM_LIMIT = 56 * 1024 * 1024
F32 = jnp.float32


def _cparams(n_grid, vmem=VMEM_LIMIT):
    return pltpu.CompilerParams(dimension_semantics=("arbitrary",) * n_grid, vmem_limit_bytes=vmem)


def _const_spec(shape):
    nd = len(shape)
    return pl.BlockSpec(shape, lambda *_: (0,) * nd)


def _dot(a, b):
    return jnp.dot(a, b, preferred_element_type=F32)


def _dot_nt(a, b, precision=None):
    return lax.dot_general(a, b, (((1,), (1,)), ((), ())), preferred_element_type=F32, precision=precision)


def _rms(x, gain):
    return x * lax.rsqrt(jnp.mean(x * x, axis=-1, keepdims=True) + NORM_EPS) * gain


def _ffn_kernel(x_ref, g_ref, wg_ref, wu_ref, wd_ref, *rest, ff_chunk, final_norm):
    if final_norm:
        fg_ref, o_ref, acc_ref = rest
    else:
        o_ref, acc_ref = rest
    x = x_ref[...]
    h = _rms(x, g_ref[...]).astype(MXU_DTYPE)
    d_ff = wg_ref.shape[1]
    for c in range(d_ff // ff_chunk):
        sl = slice(c * ff_chunk, (c + 1) * ff_chunk)
        a = _dot(h, wg_ref[:, sl])
        u = _dot(h, wu_ref[:, sl])
        z = (a * jax.nn.sigmoid(a) * u).astype(MXU_DTYPE)
        part = _dot(z, wd_ref[sl, :])
        if c == 0:
            acc_ref[...] = part
        else:
            acc_ref[...] += part
    y = x + 0.5 * acc_ref[...]
    if final_norm:
        y = _rms(y, fg_ref[...])
    o_ref[...] = y


def _ffn(x2, gain, wg, wu, wd, final_gain=None, tm=512):
    t, d = x2.shape
    d_ff = wg.shape[1]
    ff_chunk = 256 if d_ff % 256 == 0 else d_ff
    tm = min(tm, t)
    final = final_gain is not None
    in_specs = [pl.BlockSpec((tm, d), lambda i: (i, 0)), _const_spec((1, d)),
                _const_spec((d, d_ff)), _const_spec((d, d_ff)), _const_spec((d_ff, d))]
    args = [x2, gain.reshape(1, d), wg, wu, wd]
    if final:
        in_specs.append(_const_spec((1, d)))
        args.append(final_gain.reshape(1, d))
    return pl.pallas_call(
        functools.partial(_ffn_kernel, ff_chunk=ff_chunk, final_norm=final),
        out_shape=jax.ShapeDtypeStruct((t, d), F32),
        grid=(t // tm,),
        in_specs=in_specs,
        out_specs=pl.BlockSpec((tm, d), lambda i: (i, 0)),
        scratch_shapes=[pltpu.VMEM((tm, d), F32)],
        compiler_params=_cparams(1),
        name="ffn_swiglu",
    )(*args)


def _bias_tile_kernel(tbl_ref, o_ref, *, heads, width, row_step, col_mult, offset, dist_mult, lo, hi, n_cols):
    d = pl.program_id(0)
    i = lax.broadcasted_iota(jnp.int32, (TQ, width), 0)
    j = lax.broadcasted_iota(jnp.int32, (TQ, width), 1)
    raw = row_step * d + i - col_mult * j + offset
    valid = (raw >= lo) & (raw <= hi) & (j < n_cols)
    n = jnp.maximum(raw * dist_mult, 0)
    nf = jnp.maximum(n, REL_MAX_EXACT).astype(F32)
    large = REL_MAX_EXACT + (jnp.log(nf / REL_MAX_EXACT) / math.log(REL_MAX_DIST / REL_MAX_EXACT)
                             * (REL_BUCKETS - REL_MAX_EXACT)).astype(jnp.int32)
    large = jnp.minimum(large, REL_BUCKETS - 1)
    bucket = jnp.where(n < REL_MAX_EXACT, n, large)
    for hh, head in enumerate(heads):
        val = jnp.zeros((TQ, width), F32)
        for k in range(REL_BUCKETS):
            val = jnp.where(bucket == k, tbl_ref[k, head], val)
        o_ref[0, hh] = jnp.where(valid, val, NEG)


def _bias_tiles(rel_bias, *, n_tiles, heads, width=TK, row_step=TQ, col_mult=1, offset=0, dist_mult=1,
                lo=0, hi=2 ** 30, n_cols=None, name):
    n_cols = width if n_cols is None else n_cols
    nh = len(heads)
    return pl.pallas_call(
        functools.partial(_bias_tile_kernel, heads=tuple(heads), width=width, row_step=row_step,
                          col_mult=col_mult, offset=offset, dist_mult=dist_mult, lo=lo, hi=hi, n_cols=n_cols),
        out_shape=jax.ShapeDtypeStruct((n_tiles, nh, TQ, width), F32),
        grid=(n_tiles,),
        in_specs=[pl.BlockSpec(memory_space=pltpu.SMEM)],
        out_specs=pl.BlockSpec((1, nh, TQ, width), lambda d: (d, 0, 0, 0)),
        compiler_params=_cparams(1),
        name=name,
    )(rel_bias)


_DILATIONS = tuple(d for _, d in DIL_PATTERNS)
_PROJ_SEGS = (
    ("nsa_q", NSA_HEADS * SLOT, MXU_DTYPE, 1),
    ("nsa_kc", LANES, F32, NSA_CMP_STRIDE),
    ("nsa_vc", LANES, F32, NSA_CMP_STRIDE),
    ("nsa_ks", NSA_KV_HEADS * SLOT, MXU_DTYPE, 1),
    ("nsa_vs", NSA_KV_HEADS * SLOT, MXU_DTYPE, 1),
    ("nsa_kw", NSA_KV_HEADS * SLOT, MXU_DTYPE, 1),
    ("nsa_vw", NSA_KV_HEADS * SLOT, MXU_DTYPE, 1),
    ("nsa_gate", NSA_KV_HEADS * SLOT, F32, 1),
    ("moba_q", MOBA_HEADS * SLOT, MXU_DTYPE, 1),
    ("moba_k", MOBA_HEADS * SLOT, MXU_DTYPE, 1),
    ("moba_v", MOBA_HEADS * SLOT, MXU_DTYPE, 1),
) + tuple((f"dil_{n}{g}", LANES, MXU_DTYPE, d) for n in "qkv" for g, d in enumerate(_DILATIONS))
_PROJ_OFFS = np.concatenate([[0], np.cumsum([s[1] for s in _PROJ_SEGS])])
_PROJ_COLS = int(_PROJ_OFFS[-1])


def _proj_weights(w_l):
    hd = HEAD_DIM
    d_model = w_l.shape[0]
    kvw = NSA_KV_HEADS * hd
    widths = (("nsa_q", NSA_HEADS * hd), ("nsa_k_cmp", kvw), ("nsa_v_cmp", kvw), ("nsa_k_sel", kvw),
              ("nsa_v_sel", kvw), ("nsa_k_win", kvw), ("nsa_v_win", kvw), ("nsa_gate", NSA_HEADS * 3),
              ("moba_q", MOBA_HEADS * hd), ("moba_k", MOBA_HEADS * hd), ("moba_v", MOBA_HEADS * hd),
              ("dil_q", 6 * hd), ("dil_k", 6 * hd), ("dil_v", 6 * hd))
    cols, o = {}, 0
    for name, w in widths:
        cols[name] = w_l[:, o:o + w]
        o += w
    qk_scale = hd ** -0.5

    def slots(w, n, real):
        w = w.reshape(d_model, n, real)
        return jnp.pad(w, ((0, 0), (0, 0), (0, SLOT - real))).reshape(d_model, n * SLOT)

    parts = {
        "nsa_q": slots(cols["nsa_q"] * qk_scale, NSA_HEADS, hd),
        "nsa_kc": cols["nsa_k_cmp"], "nsa_vc": cols["nsa_v_cmp"],
        "nsa_ks": slots(cols["nsa_k_sel"], NSA_KV_HEADS, hd), "nsa_vs": slots(cols["nsa_v_sel"], NSA_KV_HEADS, hd),
        "nsa_kw": slots(cols["nsa_k_win"], NSA_KV_HEADS, hd), "nsa_vw": slots(cols["nsa_v_win"], NSA_KV_HEADS, hd),
        "nsa_gate": slots(cols["nsa_gate"], NSA_KV_HEADS, NSA_REP * 3),
        "moba_q": slots(cols["moba_q"] * qk_scale, MOBA_HEADS, hd),
        "moba_k": slots(cols["moba_k"], MOBA_HEADS, hd), "moba_v": slots(cols["moba_v"], MOBA_HEADS, hd),
    }
    for g in range(len(DIL_PATTERNS)):
        sl = slice(g * 2 * hd, (g + 1) * 2 * hd)
        parts[f"dil_q{g}"] = cols["dil_q"][:, sl] * qk_scale
        parts[f"dil_k{g}"] = cols["dil_k"][:, sl]
        parts[f"dil_v{g}"] = cols["dil_v"][:, sl]
    w_attn = jnp.concatenate([parts[name] for name, _, _, _ in _PROJ_SEGS], axis=1)
    return w_attn.astype(MXU_DTYPE), w_l[:, o:].astype(MXU_DTYPE)


def _proj_kernel(x_ref, g_ref, w_ref, *refs, seq, tm):
    names = [s[0] for s in _PROJ_SEGS] + ["moba_kmean"]
    outs = dict(zip(names, refs[:len(names)]))
    stage_ref = refs[len(names)]
    h = _rms(x_ref[...], g_ref[...]).astype(MXU_DTYPE)
    pos0 = (pl.program_id(0) * tm) % seq
    for si, (name, width, dtype, inter) in enumerate(_PROJ_SEGS):
        c0 = int(_PROJ_OFFS[si])
        y = _dot(h, w_ref[:, c0:c0 + width])
        if inter > 1:
            stage_ref[...] = y
            for r in range(inter):
                outs[name][:, r * LANES:(r + 1) * LANES] = stage_ref[pl.ds(r, tm // inter, stride=inter), :].astype(dtype)
            continue
        if name == "moba_k":
            nblk = tm // MOBA_BLOCK
            outs["moba_kmean"][0] = jnp.mean(y.reshape(nblk, MOBA_BLOCK, width), axis=1)
        if name in ("nsa_ks", "moba_k"):
            blk = NSA_SEL_BLOCK if name == "nsa_ks" else MOBA_BLOCK
            row = lax.broadcasted_iota(jnp.int32, (tm, width), 0)
            lane = lax.broadcasted_iota(jnp.int32, (tm, width), 1) & (SLOT - 1)
            hit = (lane - HEAD_DIM) == jnp.right_shift(pos0 + row, int(math.log2(blk)))
            y = jnp.where(hit, 1.0, y)
        if name in ("nsa_vs", "nsa_vw", "moba_v"):
            lane = lax.broadcasted_iota(jnp.int32, (tm, width), 1) & (SLOT - 1)
            y = jnp.where(lane == HEAD_DIM, 1.0, y)
        if name == "nsa_gate":
            y = jax.nn.sigmoid(y)
        outs[name][...] = y.astype(dtype)


def _proj(x2, gain, w_attn, seq, tm=512):
    t, d = x2.shape
    tm = min(tm, seq)
    assert tm % MOBA_BLOCK == 0 and seq % tm == 0
    nblk = tm // MOBA_BLOCK
    out_shape = [jax.ShapeDtypeStruct((t // il, w * il), dt) for _, w, dt, il in _PROJ_SEGS]
    out_specs = [pl.BlockSpec((tm // il, w * il), lambda i: (i, 0)) for _, w, _, il in _PROJ_SEGS]
    kw = MOBA_HEADS * SLOT
    out_shape.append(jax.ShapeDtypeStruct((t // tm, nblk, kw), F32))
    out_specs.append(pl.BlockSpec((1, nblk, kw), lambda i: (i, 0, 0)))
    outs = pl.pallas_call(
        functools.partial(_proj_kernel, seq=seq, tm=tm),
        out_shape=out_shape,
        grid=(t // tm,),
        in_specs=[pl.BlockSpec((tm, d), lambda i: (i, 0)), _const_spec((1, d)), _const_spec((d, _PROJ_COLS))],
        out_specs=out_specs,
        scratch_shapes=[pltpu.VMEM((tm, LANES), F32)],
        compiler_params=_cparams(1),
        name="mixer_in_proj",
    )(x2, gain.reshape(1, d), w_attn)
    res = {s[0]: o for s, o in zip(_PROJ_SEGS, outs)}
    res["moba_kmean"] = outs[-1]
    return res


def _gelu_tanh(x):
    return 0.5 * x * (1.0 + jnp.tanh(math.sqrt(2.0 / math.pi) * (x + 0.044715 * (x * x * x))))


def _compress_kernel(k_ref, v_ref, pek_ref, pev_ref, wk1_ref, wv1_ref, wk2_ref, wv2_ref, kc_ref, vc_ref):
    nr = k_ref.shape[1]
    hid_w = NSA_CMP_HIDDEN
    for x_ref, pe_ref, w1_ref, w2_ref, o_ref in ((k_ref, pek_ref, wk1_ref, wk2_ref, kc_ref),
                                                 (v_ref, pev_ref, wv1_ref, wv2_ref, vc_ref)):
        r = x_ref[0]
        lo = _dot((r + pe_ref[0:1, :]).astype(MXU_DTYPE), w1_ref[0])
        hi = _dot((r + pe_ref[1:2, :]).astype(MXU_DTYPE), w1_ref[1])
        hid = lo + pltpu.roll(hi, nr - 1, 0)
        act = _gelu_tanh(hid).astype(MXU_DTYPE)
        for g in range(NSA_KV_HEADS):
            o_ref[0, g] = _dot(act[:, g * hid_w:(g + 1) * hid_w], w2_ref[...]).astype(o_ref.dtype)


def _compress(kc_in, vc_in, pek, pev, wk1, wv1, wk2, wv2, batch, seq):
    nr = seq // NSA_CMP_STRIDE
    rw = NSA_CMP_STRIDE * LANES
    kin = kc_in.reshape(batch, nr, rw)
    vin = vc_in.reshape(batch, nr, rw)
    hw = NSA_KV_HEADS * NSA_CMP_HIDDEN
    out = jax.ShapeDtypeStruct((batch, NSA_KV_HEADS, nr, SLOT), MXU_DTYPE)
    in_blk = pl.BlockSpec((1, nr, rw), lambda b: (b, 0, 0))
    out_blk = pl.BlockSpec((1, NSA_KV_HEADS, nr, SLOT), lambda b: (b, 0, 0, 0))
    return pl.pallas_call(
        _compress_kernel,
        out_shape=[out, out],
        grid=(batch,),
        in_specs=[in_blk, in_blk, _const_spec((2, rw)), _const_spec((2, rw)),
                  _const_spec((2, rw, hw)), _const_spec((2, rw, hw)),
                  _const_spec((NSA_CMP_HIDDEN, SLOT)), _const_spec((NSA_CMP_HIDDEN, SLOT))],
        out_specs=[out_blk, out_blk],
        compiler_params=_cparams(1),
        name="nsa_compress",
    )(kin, vin, pek, pev, wk1, wv1, wk2, wv2)


def _rank_before(score):
    n, tq = score.shape
    sub_rows = 8
    assert n % sub_rows == 0
    bits = pltpu.bitcast(score, jnp.int32)
    key = bits ^ (jnp.right_shift(bits, 31) & 0x7FFFFFFF)
    keys = [key[b * sub_rows:(b + 1) * sub_rows] for b in range(n // sub_rows)]
    sub = lax.broadcasted_iota(jnp.int32, (sub_rows, tq), 0)
    adj = [k - 1 for k in keys]
    rank = [jnp.zeros((sub_rows, tq), jnp.int32) for _ in keys]
    for j in range(n):
        b, r = divmod(j, sub_rows)
        adj[b] = adj[b] + jnp.where(sub == r, 1, 0)
        row = keys[b][r:r + 1, :]
        rank = [rk + jnp.where(row > a, 1, 0) for rk, a in zip(rank, adj)]
    return jnp.concatenate(rank, axis=0)


def _bias_block(t_ref, q_tile0, k_tile0, n_stack, n_qs, n_kt):
    rows = []
    for r in range(n_stack):
        for a in range(n_qs):
            base = q_tile0 + a - k_tile0 + TILE_PAD
            rows.append(jnp.concatenate([t_ref[base - j, r] for j in range(n_kt)], axis=1))
    return jnp.concatenate(rows, axis=0)


def _flash_chunks(q, k_ref, v_ref, t_ref, q_tile0, n_stack, n_qs, n_chunks, scratch, max_in_producer=True):
    q_s, s_a, s_b, x_a, x_b, p_a, p_b, m_s, a_s, acc_s = scratch
    rows = q.shape[0]
    kw = KV_CHUNK * TK
    stat = (rows, LANES)

    def chunk_rows(c):
        return pl.ds(pl.multiple_of(c * kw, kw), kw)

    k_refs = k_ref if isinstance(k_ref, (list, tuple)) else [k_ref]
    v_refs = v_ref if isinstance(v_ref, (list, tuple)) else [v_ref]
    grp = rows // len(k_refs)
    groups = [slice(i * grp, (i + 1) * grp) for i in range(len(k_refs))]

    def logits_into(c, s_ref, x_ref):
        bias = _bias_block(t_ref, q_tile0, c * KV_CHUNK, n_stack, n_qs, KV_CHUNK)
        for rows_g, kg in zip(groups, k_refs):
            s = _dot_nt(q_s[rows_g, :], kg[chunk_rows(c), :]) + bias[rows_g]
            s_ref[rows_g, :] = s
            if max_in_producer:
                x_ref[rows_g, :] = row_max(s)

    def row_max(s):
        tiles = [s[:, j * TK:(j + 1) * TK] for j in range(KV_CHUNK)]
        return jnp.broadcast_to(jnp.max(functools.reduce(jnp.maximum, tiles), axis=-1, keepdims=True),
                                (s.shape[0], LANES))

    def pv(p_ref, c):
        for rows_g, vg in zip(groups, v_refs):
            acc_s[rows_g, :] = a_s[rows_g, :] * acc_s[rows_g, :] + _dot(p_ref[rows_g, :], vg[chunk_rows(c), :])

    def step(c, s_cur, x_cur, s_nxt, x_nxt, p_cur, p_prev, prefetch=True):
        pv(p_prev, jnp.maximum(c - 1, 0))
        if prefetch:
            logits_into(jnp.minimum(c + 1, n_chunks - 1), s_nxt, x_nxt)
        m_prev = m_s[...]
        m_new = jnp.maximum(m_prev, x_cur[...] if max_in_producer else row_max(s_cur[...]))
        a_s[...] = jnp.exp(m_prev - m_new)
        m_s[...] = m_new
        for j in range(KV_CHUNK):
            p_cur[:, j * TK:(j + 1) * TK] = jnp.exp(s_cur[:, j * TK:(j + 1) * TK] - m_new).astype(MXU_DTYPE)

    q_s[...] = q
    m_s[...] = jnp.full(stat, M_INIT, F32)
    a_s[...] = jnp.ones(stat, F32)
    acc_s[...] = jnp.zeros(stat, F32)
    p_b[...] = jnp.zeros(p_b.shape, MXU_DTYPE)
    logits_into(0, s_a, x_a)

    def body(i, carry):
        step(2 * i, s_a, x_a, s_b, x_b, p_a, p_b)
        step(2 * i + 1, s_b, x_b, s_a, x_a, p_b, p_a)
        return carry

    lax.fori_loop(0, n_chunks // 2, body, 0)
    last = n_chunks - 1

    @pl.when(last % 2 == 0)
    def _():
        step(last, s_a, x_a, s_b, x_b, p_a, p_b, prefetch=False)
        pv(p_a, last)

    @pl.when(last % 2 == 1)
    def _():
        pv(p_b, last)
    return _normalize(acc_s[...])


def _flash_scratch(rows):
    kw = KV_CHUNK * TK
    return ([pltpu.VMEM((rows, LANES), MXU_DTYPE)]
            + [pltpu.VMEM((rows, kw), F32)] * 2 + [pltpu.VMEM((rows, LANES), F32)] * 2
            + [pltpu.VMEM((rows, kw), MXU_DTYPE)] * 2 + [pltpu.VMEM((rows, LANES), F32)] * 3)


def _normalize(acc):
    return acc / jnp.maximum(acc[:, HEAD_DIM:HEAD_DIM + 1], 1e-30)


def _attend_once(q, k, v, bias):
    s = _dot_nt(q, k) + bias
    m = jnp.maximum(jnp.max(s, axis=-1, keepdims=True), M_INIT)
    p = jnp.exp(s - m)
    return _normalize(_dot(p.astype(MXU_DTYPE), v))


def _block_mask_lanes(z_ref, sel_t, tq):
    n = sel_t.shape[0]
    z_ref[...] = jnp.zeros(z_ref.shape, F32)
    z_ref[HEAD_DIM:HEAD_DIM + n, :] = jnp.where(sel_t, 0.0, NEG)
    return z_ref[...].T


def _nsa_kernel(q_ref, gate_ref, kc_ref, vc_ref, ks_ref, vs_ref, kw_ref, vw_ref, c2s_ref,
                tcmp_ref, tsel_ref, twin_ref, o_ref, z_ref, *flash_scratch, n_slc):
    step = pl.program_id(2)
    qt0 = step * NSA_QS
    rep, nqs = NSA_REP, NSA_QS
    tqb = nqs * TQ
    q_all = q_ref[0]
    q = jnp.concatenate([q_all[a * TQ:(a + 1) * TQ, r * SLOT:(r + 1) * SLOT]
                         for r in range(rep) for a in range(nqs)], axis=0)

    n_wt = WIN_TILES + nqs
    w0 = jnp.maximum(qt0 - WIN_TILES, 0)
    w_rows = pl.ds(pl.multiple_of(w0 * TK, TK), n_wt * TK)
    o_w = _attend_once(q, kw_ref[0, w_rows, :], vw_ref[0, w_rows, :],
                       _bias_block(twin_ref, qt0, w0, rep, nqs, n_wt))

    bias_c = jnp.concatenate([tcmp_ref[a, r] for r in range(rep) for a in range(nqs)], axis=0)
    sc = _dot_nt(q, kc_ref[0, 0]) + bias_c
    mc = jnp.maximum(jnp.max(sc, axis=-1, keepdims=True), M_INIT)
    pc = jnp.exp(sc - mc)
    pc = pc / jnp.maximum(jnp.sum(pc, axis=-1, keepdims=True), 1e-30)
    o_c = _dot(pc.astype(MXU_DTYPE), vc_ref[0, 0])

    p_sum = pc[0:tqb]
    for r in range(1, rep):
        p_sum = p_sum + pc[r * tqb:(r + 1) * tqb]
    p_hi = p_sum.astype(MXU_DTYPE)
    p_lo = (p_sum - p_hi.astype(F32)).astype(MXU_DTYPE)
    imp = _dot(p_hi, c2s_ref[...]) + _dot(p_lo, c2s_ref[...])
    imp_t = imp.T[0:n_slc]
    blk = lax.broadcasted_iota(jnp.int32, (n_slc, tqb), 0)
    t_pos = qt0 * TQ + lax.broadcasted_iota(jnp.int32, (n_slc, tqb), 1)
    cur = jnp.right_shift(t_pos, int(math.log2(NSA_SEL_BLOCK)))
    forced = (blk == 0) | (blk == cur) | (blk == cur - 1)
    score = jnp.where(blk > cur, NEG, jnp.where(forced, FORCE_SCORE, imp_t))
    sel_t = _rank_before(score) < min(NSA_N_SEL, n_slc)
    zt = _block_mask_lanes(z_ref, sel_t, tqb)
    q_sel = (q.astype(F32) + jnp.concatenate([zt] * rep, axis=0)).astype(MXU_DTYPE)

    n_chunks = (qt0 + nqs - 1) // KV_CHUNK + 1
    o_s = _flash_chunks(q_sel, ks_ref.at[0], vs_ref.at[0], tsel_ref, qt0, rep, nqs, n_chunks, flash_scratch)

    gates = gate_ref[0]
    for r in range(rep):
        sl = slice(r * tqb, (r + 1) * tqb)
        o_r = (gates[:, 3 * r:3 * r + 1] * o_c[sl] + gates[:, 3 * r + 1:3 * r + 2] * o_s[sl]
               + gates[:, 3 * r + 2:3 * r + 3] * o_w[sl])
        o_ref[0, :, r * SLOT:(r + 1) * SLOT] = o_r.astype(o_ref.dtype)


def _nsa(p, kc, vc, c2s, tcmp, tmain, twin, batch, seq):
    nq = seq // TQ
    n_slc = seq // NSA_SEL_BLOCK
    tqb = NSA_QS * TQ
    assert n_slc <= HEAD_DIM and nq % KV_CHUNK == 0 and nq % NSA_QS == 0 and nq >= WIN_TILES + NSA_QS
    assert TILE_PAD >= KV_CHUNK + NSA_QS - 2
    g_n, rep = NSA_KV_HEADS, NSA_REP
    n_cmp_pad = kc.shape[2]
    r3 = lambda a: a.reshape(batch, seq, a.shape[-1])
    kv_spec = pl.BlockSpec((1, seq, SLOT), lambda b, g, i: (b, 0, g))
    cmp_spec = pl.BlockSpec((1, 1, n_cmp_pad, SLOT), lambda b, g, i: (b, g, 0, 0))
    return pl.pallas_call(
        functools.partial(_nsa_kernel, n_slc=n_slc),
        out_shape=jax.ShapeDtypeStruct((batch, seq, NSA_HEADS * SLOT), MXU_DTYPE),
        grid=(batch, g_n, nq // NSA_QS),
        in_specs=[
            pl.BlockSpec((1, tqb, rep * SLOT), lambda b, g, i: (b, i, g)),
            pl.BlockSpec((1, tqb, SLOT), lambda b, g, i: (b, i, g)),
            cmp_spec, cmp_spec, kv_spec, kv_spec, kv_spec, kv_spec,
            _const_spec(c2s.shape),
            pl.BlockSpec((NSA_QS, rep, TQ, n_cmp_pad), lambda b, g, i: (i, g, 0, 0)),
            pl.BlockSpec((tmain.shape[0], rep, TQ, TK), lambda b, g, i: (0, g, 0, 0),
                         pipeline_mode=pl.Buffered(1)),
            pl.BlockSpec((twin.shape[0], rep, TQ, TK), lambda b, g, i: (0, g, 0, 0),
                         pipeline_mode=pl.Buffered(1)),
        ],
        out_specs=pl.BlockSpec((1, tqb, rep * SLOT), lambda b, g, i: (b, i, g)),
        scratch_shapes=[pltpu.VMEM((LANES, tqb), F32)] + _flash_scratch(rep * tqb),
        compiler_params=_cparams(3),
        name="nsa_attention",
    )(r3(p["nsa_q"]), r3(p["nsa_gate"]), kc, vc, r3(p["nsa_ks"]), r3(p["nsa_vs"]), r3(p["nsa_kw"]),
      r3(p["nsa_vw"]), c2s.astype(MXU_DTYPE), tcmp, tmain, twin)


def _moba_kernel(q_ref, *refs, nb):
    hs = MOBA_HS
    k_refs, v_refs = refs[0:hs], refs[hs:2 * hs]
    km_ref, t_ref, o_ref, z_ref = refs[2 * hs:2 * hs + 4]
    flash_scratch = refs[2 * hs + 4:]
    step = pl.program_id(2)
    tqb = MOBA_QB * MOBA_BLOCK
    n_qs = tqb // TQ
    n_i = lax.broadcasted_iota(jnp.int32, (nb, tqb), 0)
    own = step * MOBA_QB + jnp.right_shift(lax.broadcasted_iota(jnp.int32, (nb, tqb), 1), int(math.log2(MOBA_BLOCK)))
    past = n_i < own
    q_sel = []
    for h in range(hs):
        q = q_ref[0, :, h * SLOT:(h + 1) * SLOT]
        km = km_ref[0, :, h * SLOT:(h + 1) * SLOT]
        km_hi = km.astype(MXU_DTYPE)
        rem = km - km_hi.astype(F32)
        km_mid = rem.astype(MXU_DTYPE)
        km_lo = (rem - km_mid.astype(F32)).astype(MXU_DTYPE)
        gate_t = _dot_nt(km_hi, q) + _dot_nt(km_mid, q) + _dot_nt(km_lo, q)
        score = jnp.where(past, gate_t, NEG)
        sel_t = ((_rank_before(score) < min(MOBA_TOPK, nb - 1)) & past) | (n_i == own)
        zt = _block_mask_lanes(z_ref.at[h], sel_t, tqb)
        q_sel.append((q.astype(F32) + zt).astype(MXU_DTYPE))
    q_tile0 = step * n_qs
    n_chunks = (q_tile0 + n_qs - 1) // KV_CHUNK + 1
    o = _flash_chunks(jnp.concatenate(q_sel, axis=0), [r.at[0] for r in k_refs], [r.at[0] for r in v_refs], t_ref,
                      q_tile0, hs, n_qs, n_chunks, flash_scratch, max_in_producer=False)
    for h in range(hs):
        o_ref[0, :, h * SLOT:(h + 1) * SLOT] = o[h * tqb:(h + 1) * tqb].astype(o_ref.dtype)


def _moba(p, kmean, tmain, batch, seq):
    nq = seq // TQ
    nb = seq // MOBA_BLOCK
    tqb = MOBA_QB * MOBA_BLOCK
    hs = MOBA_HS
    assert seq % tqb == 0 and nb <= HEAD_DIM and MOBA_BLOCK % TQ == 0 and nq % KV_CHUNK == 0
    assert TILE_PAD >= KV_CHUNK + tqb // TQ - 2 and MOBA_HEADS % hs == 0 and NSA_HEADS % hs == 0
    r3 = lambda a: a.reshape(batch, seq, a.shape[-1])
    kv_specs = [pl.BlockSpec((1, seq, SLOT), lambda b, g, i, h=h: (b, 0, g * hs + h)) for h in range(hs)]
    q_spec = pl.BlockSpec((1, tqb, hs * SLOT), lambda b, g, i: (b, i, g))
    return pl.pallas_call(
        functools.partial(_moba_kernel, nb=nb),
        out_shape=jax.ShapeDtypeStruct((batch, seq, MOBA_HEADS * SLOT), MXU_DTYPE),
        grid=(batch, MOBA_HEADS // hs, seq // tqb),
        in_specs=[q_spec] + kv_specs + kv_specs
                 + [pl.BlockSpec((1, nb, hs * SLOT), lambda b, g, i: (b, 0, g)),
                    pl.BlockSpec((tmain.shape[0], hs, TQ, TK), lambda b, g, i: (0, NSA_HEADS // hs + g, 0, 0))],
        out_specs=q_spec,
        scratch_shapes=[pltpu.VMEM((hs, LANES, tqb), F32)] + _flash_scratch(hs * tqb),
        compiler_params=_cparams(3),
        name="moba_attention",
    )(r3(p["moba_q"]), *([r3(p["moba_k"])] * hs), *([r3(p["moba_v"])] * hs),
      kmean.reshape(batch, nb, MOBA_HEADS * SLOT), tmain)


def _dilated_kernel(q_ref, k_ref, v_ref, t_ref, o_ref, lse_ref, *, n_tiles, n_res):
    lane = lax.broadcasted_iota(jnp.int32, (DIL_WB, LANES), 1)
    first = lane < HEAD_DIM
    n_kt = min(2, n_tiles)

    for res in range(n_res):
        lanes = slice(res * LANES, (res + 1) * LANES)

        def tile(n, carry, lanes=lanes):
            kv0 = jnp.maximum(n - 1, 0)
            q_rows = pl.ds(pl.multiple_of(n * DIL_WB, DIL_WB), DIL_WB)
            kv_rows = pl.ds(pl.multiple_of(kv0 * DIL_WB, DIL_WB), n_kt * DIL_WB)
            q = q_ref[0, q_rows, lanes]
            k = k_ref[0, kv_rows, lanes]
            v = v_ref[0, kv_rows, lanes]
            o_heads, lse_heads = [], []
            for h in range(DIL_HEADS_PER_GROUP):
                qh = jnp.where(first if h == 0 else ~first, q, jnp.zeros_like(q))
                bias = jnp.concatenate([t_ref[(n - kv0) - j + (0 if j == 0 else 3 * (1 - (n - kv0))), h]
                                        for j in range(n_kt)], axis=1)
                s = _dot_nt(qh, k) + bias
                m = jnp.max(s, axis=-1, keepdims=True)
                p = jnp.exp(s - m)
                l = jnp.maximum(jnp.sum(p, axis=-1, keepdims=True), 1e-30)
                o_heads.append(_dot(p.astype(MXU_DTYPE), v) / l)
                lse_heads.append(m + jnp.log(l))
            o_ref[0, q_rows, lanes] = jnp.where(first, o_heads[0], o_heads[1])
            lse_ref[0, q_rows, lanes] = jnp.where(first, lse_heads[0], lse_heads[1])
            return carry

        lax.fori_loop(0, n_tiles, tile, 0, unroll=min(DIL_UNROLL, n_tiles))


def _dilated(q, k, v, tdil, dilation, batch, seq):
    ln = seq // dilation
    assert ln % DIL_WB == 0
    view = lambda a: a.reshape(batch, ln, dilation * LANES)
    spec = pl.BlockSpec((1, ln, dilation * LANES), lambda b: (b, 0, 0))
    out = jax.ShapeDtypeStruct((batch, ln, dilation * LANES), F32)
    o, lse = pl.pallas_call(
        functools.partial(_dilated_kernel, n_tiles=ln // DIL_WB, n_res=dilation),
        out_shape=[out, out],
        grid=(batch,),
        in_specs=[spec, spec, spec, _const_spec(tdil.shape)],
        out_specs=[spec, spec],
        compiler_params=_cparams(1),
        name=f"dilated_attention_d{dilation}",
    )(view(q), view(k), view(v), tdil)
    rows = batch * ln
    return o.reshape(rows, dilation * LANES), lse.reshape(rows, dilation * LANES)


def _mixer_out_kernel(x_ref, g_ref, ya_ref, yb_ref, o0_ref, o1_ref, o2_ref, l0_ref, l1_ref, l2_ref,
                      wmg_ref, wa_ref, wb_ref, wc_ref, wo_ref, out_ref, *stage_refs):
    x = x_ref[...]
    d = x.shape[1]
    tm = x.shape[0]
    h = _rms(x, g_ref[...]).astype(MXU_DTYPE)

    def token_major(ref, stage_ref):
        il = ref.shape[1] // LANES
        if il == 1:
            return ref[...]
        for r in range(il):
            stage_ref[pl.ds(r, tm // il, stride=il), :] = ref[:, r * LANES:(r + 1) * LANES]
        return stage_ref[...]

    o0, o1, o2, l0, l1, l2 = [token_major(r, s) for r, s in
                              zip((o0_ref, o1_ref, o2_ref, l0_ref, l1_ref, l2_ref), stage_refs)]
    mx = jnp.maximum(jnp.maximum(l0, l1), l2)
    e0, e1, e2 = jnp.exp(l0 - mx), jnp.exp(l1 - mx), jnp.exp(l2 - mx)
    y_c = (e0 * o0 + e1 * o1 + e2 * o2) / (e0 + e1 + e2)
    merged = jax.nn.sigmoid(_dot(h, wmg_ref[:, 0:d])) * _dot(ya_ref[...], wa_ref[...])
    merged += jax.nn.sigmoid(_dot(h, wmg_ref[:, d:2 * d])) * _dot(yb_ref[...], wb_ref[...])
    merged += jax.nn.sigmoid(_dot(h, wmg_ref[:, 2 * d:3 * d])) * _dot(y_c.astype(MXU_DTYPE), wc_ref[...])
    out_ref[...] = x + _dot(merged.astype(MXU_DTYPE), wo_ref[...])


def _mixer_out(x2, gain, ya, yb, dil, wmg, wa, wb, wc, wo, tm=512):
    t, d = x2.shape
    tm = min(tm, t)
    row = lambda w: pl.BlockSpec((tm, w), lambda i: (i, 0))
    (o0, l0), (o1, l1), (o2, l2) = dil
    dil_arrays = (o0, o1, o2, l0, l1, l2)
    inter = lambda a: pl.BlockSpec((tm // (a.shape[1] // LANES), a.shape[1]), lambda i: (i, 0))
    return pl.pallas_call(
        _mixer_out_kernel,
        out_shape=jax.ShapeDtypeStruct((t, d), F32),
        grid=(t // tm,),
        in_specs=[row(d), _const_spec((1, d)), row(ya.shape[1]), row(yb.shape[1])] + [inter(a) for a in dil_arrays]
                 + [_const_spec(w.shape) for w in (wmg, wa, wb, wc, wo)],
        out_specs=row(d),
        scratch_shapes=[pltpu.VMEM((tm, LANES), F32)] * len(dil_arrays),
        compiler_params=_cparams(1),
        name="mixer_out_proj",
    )(x2, gain.reshape(1, d), ya, yb, o0, o1, o2, l0, l1, l2, wmg, wa, wb, wc, wo)


def _slot_rows(w, n_slots):
    d = w.shape[1]
    w = w.reshape(n_slots, HEAD_DIM, d)
    return jnp.concatenate([w, jnp.zeros_like(w)], axis=1).reshape(n_slots * SLOT, d)


def _compress_weights(pe, w1, w2):
    g_n, hd, half = NSA_KV_HEADS, HEAD_DIM, NSA_CMP_STRIDE
    hid = w1.shape[1]
    w1r = w1.reshape(2, half, hd, hid)
    eye = jnp.eye(g_n, dtype=w1.dtype)
    w1p = jnp.einsum("plch,gk->plgckh", w1r, eye).reshape(2, half * g_n * hd, g_n * hid)
    pep = jnp.broadcast_to(pe.reshape(2, half, 1, hd), (2, half, g_n, hd)).reshape(2, half * g_n * hd)
    w2p = jnp.concatenate([w2, jnp.zeros_like(w2)], axis=1)
    return pep, w1p.astype(MXU_DTYPE), w2p.astype(MXU_DTYPE)


def _cmp_to_slc(seq, n_cmp_pad):
    n_cmp = (seq - NSA_CMP_BLOCK) // NSA_CMP_STRIDE + 1
    n_slc = seq // NSA_SEL_BLOCK
    c_start = np.arange(n_cmp_pad) * NSA_CMP_STRIDE
    s_start = np.arange(LANES) * NSA_SEL_BLOCK
    ov = ((c_start[:, None] < s_start[None, :] + NSA_SEL_BLOCK) & (c_start[:, None] + NSA_CMP_BLOCK > s_start[None, :])
          & (np.arange(n_cmp_pad)[:, None] < n_cmp) & (np.arange(LANES)[None, :] < n_slc))
    return jnp.asarray(ov, F32)


def kernel(x, rel_bias, ffn1_norm, ffn1_w_gate, ffn1_w_up, ffn1_w_down, mix_norm, w_in, nsa_pe_k, nsa_pe_v,
           nsa_phi_k1, nsa_phi_k2, nsa_phi_v1, nsa_phi_v2, w_up_a, w_up_b, w_up_c, w_o, ffn2_norm, ffn2_w_gate,
           ffn2_w_up, ffn2_w_down, final_norm):
    batch, seq, d = x.shape
    depth = w_in.shape[0]
    nq = seq // TQ
    assert seq % (TQ * DIL_PATTERNS[-1][1]) == 0 and all(w // dl == DIL_WB for w, dl in DIL_PATTERNS)
    bf = lambda a: a.astype(MXU_DTYPE)

    n_cmp = (seq - NSA_CMP_BLOCK) // NSA_CMP_STRIDE + 1
    n_cmp_pad = seq // NSA_CMP_STRIDE
    a_heads = list(range(NSA_HEADS))
    ab_heads = list(range(NSA_HEADS + MOBA_HEADS))
    assert TILE_PAD >= max(WIN_TILES, KV_CHUNK - 1)
    t_main = _bias_tiles(rel_bias, n_tiles=nq + TILE_PAD, heads=ab_heads, offset=-TILE_PAD * TQ, name="bias_causal")
    t_win = _bias_tiles(rel_bias, n_tiles=WIN_TILES + NSA_QS + TILE_PAD, heads=a_heads, offset=-TILE_PAD * TQ,
                        hi=NSA_WINDOW - 1, name="bias_window")
    t_cmp = _bias_tiles(rel_bias, n_tiles=nq, heads=a_heads, width=n_cmp_pad, col_mult=NSA_CMP_STRIDE,
                        offset=-(NSA_CMP_BLOCK - 1), n_cols=n_cmp, name="bias_compressed")
    t_dil = []
    for g, (_, dilation) in enumerate(DIL_PATTERNS):
        h0 = NSA_HEADS + MOBA_HEADS + g * DIL_HEADS_PER_GROUP
        t_dil.append(_bias_tiles(rel_bias, n_tiles=3, heads=[h0, h0 + 1], dist_mult=dilation, hi=DIL_WB,
                                 name=f"bias_dilated_{g}"))
    c2s = _cmp_to_slc(seq, n_cmp_pad)

    x2 = x.reshape(batch * seq, d)
    for l in range(depth):
        last = l == depth - 1
        x2 = _ffn(x2, ffn1_norm[l], bf(ffn1_w_gate[l]), bf(ffn1_w_up[l]), bf(ffn1_w_down[l]))

        w_attn, w_merge = _proj_weights(w_in[l])
        p = _proj(x2, mix_norm[l], w_attn, seq)
        pek, wk1, wk2 = _compress_weights(nsa_pe_k[l], nsa_phi_k1[l], nsa_phi_k2[l])
        pev, wv1, wv2 = _compress_weights(nsa_pe_v[l], nsa_phi_v1[l], nsa_phi_v2[l])
        kc, vc = _compress(p["nsa_kc"], p["nsa_vc"], pek, pev, wk1, wv1, wk2, wv2, batch, seq)
        ya = _nsa(p, kc, vc, c2s, t_cmp, t_main, t_win, batch, seq)
        yb = _moba(p, p["moba_kmean"], t_main, batch, seq)
        dil = [_dilated(p[f"dil_q{g}"], p[f"dil_k{g}"], p[f"dil_v{g}"], t_dil[g], dilation, batch, seq)
               for g, (_, dilation) in enumerate(DIL_PATTERNS)]
        x2 = _mixer_out(x2, mix_norm[l], ya.reshape(batch * seq, -1), yb.reshape(batch * seq, -1), dil,
                        w_merge, bf(_slot_rows(w_up_a[l], NSA_HEADS)),
                        bf(_slot_rows(w_up_b[l], MOBA_HEADS)), bf(w_up_c[l]), bf(w_o[l]))

        x2 = _ffn(x2, ffn2_norm[l], bf(ffn2_w_gate[l]), bf(ffn2_w_up[l]), bf(ffn2_w_down[l]),
                  final_gain=final_norm if last else None)
    return x2.reshape(batch, seq, d)
```

```python
import functools
import math

import jax
import jax.numpy as jnp
import numpy as np
from jax import lax
from jax.experimental import pallas as pl
from jax.experimental.pallas import tpu as pltpu

HEAD_DIM = 64
NSA_HEADS = 6
NSA_KV_HEADS = 2
NSA_REP = NSA_HEADS // NSA_KV_HEADS
NSA_CMP_BLOCK = 32
NSA_CMP_STRIDE = 16
NSA_SEL_BLOCK = 64
NSA_N_SEL = 16
NSA_WINDOW = 512
NSA_CMP_HIDDEN = 256
MOBA_HEADS = 4
MOBA_BLOCK = 256
MOBA_TOPK = 3
DIL_PATTERNS = ((128, 1), (512, 4), (2048, 16))
DIL_HEADS_PER_GROUP = 2
DIL_WB = 128
N_BRANCHES = 3
REL_BUCKETS = 32
REL_MAX_EXACT = 16
REL_MAX_DIST = 2048
NORM_EPS = 1e-6
FORCE_SCORE = 1e4
N_HEADS_TOTAL = NSA_HEADS + MOBA_HEADS + DIL_HEADS_PER_GROUP * len(DIL_PATTERNS)

LANES = 128
MXU_COLS = 256
SLOT = LANES
TQ = 128
TK = 128
KV_CHUNK = 4
WIN_TILES = NSA_WINDOW // TK
TILE_PAD = 10
NSA_QS = 2
MOBA_QB = 2
MOBA_HS = 2
DIL_UNROLL = 4
NEG = -1e30
M_INIT = -1e29
MXU_DTYPE = jnp.bfloat16
VMEM_LIMIT = 56 * 1024 * 1024
F32 = jnp.float32


def _cparams(n_grid, vmem=VMEM_LIMIT):
    return pltpu.CompilerParams(dimension_semantics=("arbitrary",) * n_grid, vmem_limit_bytes=vmem)


def _const_spec(shape):
    nd = len(shape)
    return pl.BlockSpec(shape, lambda *_: (0,) * nd)


def _dot(a, b):
    return jnp.dot(a, b, preferred_element_type=F32)


def _dot_nt(a, b, precision=None):
    return lax.dot_general(a, b, (((1,), (1,)), ((), ())), preferred_element_type=F32, precision=precision)


def _rms(x, gain):
    return x * lax.rsqrt(jnp.mean(x * x, axis=-1, keepdims=True) + NORM_EPS) * gain


def _ffn_kernel(x_ref, g_ref, wg_ref, wu_ref, wd_ref, *rest, ff_chunk, final_norm):
    if final_norm:
        fg_ref, o_ref, acc_ref = rest
    else:
        o_ref, acc_ref = rest
    x = x_ref[...]
    h = _rms(x, g_ref[...]).astype(MXU_DTYPE)
    d_ff = wg_ref.shape[1]
    for c in range(d_ff // ff_chunk):
        sl = slice(c * ff_chunk, (c + 1) * ff_chunk)
        a = _dot(h, wg_ref[:, sl])
        u = _dot(h, wu_ref[:, sl])
        z = (a * jax.nn.sigmoid(a) * u).astype(MXU_DTYPE)
        part = _dot(z, wd_ref[sl, :])
        if c == 0:
            acc_ref[...] = part
        else:
            acc_ref[...] += part
    y = x + 0.5 * acc_ref[...]
    if final_norm:
        y = _rms(y, fg_ref[...])
    o_ref[...] = y


def _ffn(x2, gain, wg, wu, wd, final_gain=None, tm=512):
    t, d = x2.shape
    d_ff = wg.shape[1]
    ff_chunk = 256 if d_ff % 256 == 0 else d_ff
    tm = min(tm, t)
    final = final_gain is not None
    in_specs = [pl.BlockSpec((tm, d), lambda i: (i, 0)), _const_spec((1, d)),
                _const_spec((d, d_ff)), _const_spec((d, d_ff)), _const_spec((d_ff, d))]
    args = [x2, gain.reshape(1, d), wg, wu, wd]
    if final:
        in_specs.append(_const_spec((1, d)))
        args.append(final_gain.reshape(1, d))
    return pl.pallas_call(
        functools.partial(_ffn_kernel, ff_chunk=ff_chunk, final_norm=final),
        out_shape=jax.ShapeDtypeStruct((t, d), F32),
        grid=(t // tm,),
        in_specs=in_specs,
        out_specs=pl.BlockSpec((tm, d), lambda i: (i, 0)),
        scratch_shapes=[pltpu.VMEM((tm, d), F32)],
        compiler_params=_cparams(1),
        name="ffn_swiglu",
    )(*args)


def _bias_tile_kernel(tbl_ref, o_ref, *, heads, width, row_step, col_mult, offset, dist_mult, lo, hi, n_cols):
    d = pl.program_id(0)
    i = lax.broadcasted_iota(jnp.int32, (TQ, width), 0)
    j = lax.broadcasted_iota(jnp.int32, (TQ, width), 1)
    raw = row_step * d + i - col_mult * j + offset
    valid = (raw >= lo) & (raw <= hi) & (j < n_cols)
    n = jnp.maximum(raw * dist_mult, 0)
    nf = jnp.maximum(n, REL_MAX_EXACT).astype(F32)
    large = REL_MAX_EXACT + (jnp.log(nf / REL_MAX_EXACT) / math.log(REL_MAX_DIST / REL_MAX_EXACT)
                             * (REL_BUCKETS - REL_MAX_EXACT)).astype(jnp.int32)
    large = jnp.minimum(large, REL_BUCKETS - 1)
    bucket = jnp.where(n < REL_MAX_EXACT, n, large)
    for hh, head in enumerate(heads):
        val = jnp.zeros((TQ, width), F32)
        for k in range(REL_BUCKETS):
            val = jnp.where(bucket == k, tbl_ref[k, head], val)
        o_ref[0, hh] = jnp.where(valid, val, NEG)


def _bias_tiles(rel_bias, *, n_tiles, heads, width=TK, row_step=TQ, col_mult=1, offset=0, dist_mult=1,
                lo=0, hi=2 ** 30, n_cols=None, name):
    n_cols = width if n_cols is None else n_cols
    nh = len(heads)
    return pl.pallas_call(
        functools.partial(_bias_tile_kernel, heads=tuple(heads), width=width, row_step=row_step,
                          col_mult=col_mult, offset=offset, dist_mult=dist_mult, lo=lo, hi=hi, n_cols=n_cols),
        out_shape=jax.ShapeDtypeStruct((n_tiles, nh, TQ, width), F32),
        grid=(n_tiles,),
        in_specs=[pl.BlockSpec(memory_space=pltpu.SMEM)],
        out_specs=pl.BlockSpec((1, nh, TQ, width), lambda d: (d, 0, 0, 0)),
        compiler_params=_cparams(1),
        name=name,
    )(rel_bias)


_DILATIONS = tuple(d for _, d in DIL_PATTERNS)
_PROJ_SEGS = (
    ("nsa_q", NSA_HEADS * SLOT, MXU_DTYPE, 1, NSA_HEADS),
    ("nsa_kc", LANES, F32, NSA_CMP_STRIDE, 0),
    ("nsa_vc", LANES, F32, NSA_CMP_STRIDE, 0),
    ("nsa_ks", NSA_KV_HEADS * SLOT, MXU_DTYPE, 1, NSA_KV_HEADS),
    ("nsa_vs", NSA_KV_HEADS * SLOT, MXU_DTYPE, 1, NSA_KV_HEADS),
    ("nsa_kw", NSA_KV_HEADS * SLOT, MXU_DTYPE, 1, NSA_KV_HEADS),
    ("nsa_vw", NSA_KV_HEADS * SLOT, MXU_DTYPE, 1, NSA_KV_HEADS),
    ("nsa_gate", NSA_KV_HEADS * SLOT, F32, 1, NSA_KV_HEADS),
    ("moba_q", MOBA_HEADS * SLOT, MXU_DTYPE, 1, MOBA_HEADS),
    ("moba_k", MOBA_HEADS * SLOT, MXU_DTYPE, 1, MOBA_HEADS),
    ("moba_v", MOBA_HEADS * SLOT, MXU_DTYPE, 1, MOBA_HEADS),
) + tuple((f"dil_{n}{g}", LANES, MXU_DTYPE, d, 0) for n in "qkv" for g, d in enumerate(_DILATIONS))
_PROJ_WCOLS = [s[4] * HEAD_DIM if s[4] else s[1] for s in _PROJ_SEGS]
_PROJ_OFFS = np.concatenate([[0], np.cumsum(_PROJ_WCOLS)])
_PROJ_COLS = int(_PROJ_OFFS[-1])


def _proj_weights(w_l):
    hd = HEAD_DIM
    d_model = w_l.shape[0]
    kvw = NSA_KV_HEADS * hd
    widths = (("nsa_q", NSA_HEADS * hd), ("nsa_k_cmp", kvw), ("nsa_v_cmp", kvw), ("nsa_k_sel", kvw),
              ("nsa_v_sel", kvw), ("nsa_k_win", kvw), ("nsa_v_win", kvw), ("nsa_gate", NSA_HEADS * 3),
              ("moba_q", MOBA_HEADS * hd), ("moba_k", MOBA_HEADS * hd), ("moba_v", MOBA_HEADS * hd),
              ("dil_q", 6 * hd), ("dil_k", 6 * hd), ("dil_v", 6 * hd))
    cols, o = {}, 0
    for name, w in widths:
        cols[name] = w_l[:, o:o + w]
        o += w
    qk_scale = hd ** -0.5

    def slots(w, n, real):
        w = w.reshape(d_model, n, real)
        return jnp.pad(w, ((0, 0), (0, 0), (0, hd - real))).reshape(d_model, n * hd)

    parts = {
        "nsa_q": slots(cols["nsa_q"] * qk_scale, NSA_HEADS, hd),
        "nsa_kc": cols["nsa_k_cmp"], "nsa_vc": cols["nsa_v_cmp"],
        "nsa_ks": slots(cols["nsa_k_sel"], NSA_KV_HEADS, hd), "nsa_vs": slots(cols["nsa_v_sel"], NSA_KV_HEADS, hd),
        "nsa_kw": slots(cols["nsa_k_win"], NSA_KV_HEADS, hd), "nsa_vw": slots(cols["nsa_v_win"], NSA_KV_HEADS, hd),
        "nsa_gate": slots(cols["nsa_gate"], NSA_KV_HEADS, NSA_REP * 3),
        "moba_q": slots(cols["moba_q"] * qk_scale, MOBA_HEADS, hd),
        "moba_k": slots(cols["moba_k"], MOBA_HEADS, hd), "moba_v": slots(cols["moba_v"], MOBA_HEADS, hd),
    }
    for g in range(len(DIL_PATTERNS)):
        sl = slice(g * 2 * hd, (g + 1) * 2 * hd)
        parts[f"dil_q{g}"] = cols["dil_q"][:, sl] * qk_scale
        parts[f"dil_k{g}"] = cols["dil_k"][:, sl]
        parts[f"dil_v{g}"] = cols["dil_v"][:, sl]
    w_attn = jnp.concatenate([parts[seg[0]] for seg in _PROJ_SEGS], axis=1)
    return w_attn.astype(MXU_DTYPE), w_l[:, o:].astype(MXU_DTYPE)


def _proj_kernel(x_ref, g_ref, w_ref, *refs, seq, tm):
    names = [s[0] for s in _PROJ_SEGS] + ["moba_kmean"]
    outs = dict(zip(names, refs[:len(names)]))
    stage_ref = refs[len(names)]
    h = _rms(x_ref[...], g_ref[...]).astype(MXU_DTYPE)
    pos0 = (pl.program_id(0) * tm) % seq
    lower = lax.broadcasted_iota(jnp.int32, (tm, LANES), 1) < HEAD_DIM
    group_y, group_c0 = None, 0
    for si, (name, width, dtype, inter, n_slots) in enumerate(_PROJ_SEGS):
        c0 = int(_PROJ_OFFS[si])
        if group_y is None or c0 >= group_c0 + group_y.shape[1]:
            c1 = next((int(o) for o in _PROJ_OFFS[si + 1:] if (int(o) - c0) % MXU_COLS == 0), _PROJ_COLS)
            group_y, group_c0 = _dot(h, w_ref[:, c0:c1]), c0
        y = group_y[:, c0 - group_c0:c0 - group_c0 + _PROJ_WCOLS[si]]
        if n_slots:
            pieces = []
            for pair in range(n_slots // 2):
                v = y[:, pair * LANES:(pair + 1) * LANES]
                pieces += [jnp.where(lower, v, 0.0), jnp.where(lower, pltpu.roll(v, HEAD_DIM, 1), 0.0)]
            y = jnp.concatenate(pieces, axis=1)
        if inter > 1:
            stage_ref[...] = y
            for r in range(inter):
                outs[name][:, r * LANES:(r + 1) * LANES] = stage_ref[pl.ds(r, tm // inter, stride=inter), :].astype(dtype)
            continue
        if name == "moba_k":
            nblk = tm // MOBA_BLOCK
            outs["moba_kmean"][0] = jnp.mean(y.reshape(nblk, MOBA_BLOCK, width), axis=1)
        if name in ("nsa_ks", "moba_k"):
            blk = NSA_SEL_BLOCK if name == "nsa_ks" else MOBA_BLOCK
            row = lax.broadcasted_iota(jnp.int32, (tm, width), 0)
            lane = lax.broadcasted_iota(jnp.int32, (tm, width), 1) & (SLOT - 1)
            hit = (lane - HEAD_DIM) == jnp.right_shift(pos0 + row, int(math.log2(blk)))
            y = jnp.where(hit, 1.0, y)
        if name in ("nsa_vs", "nsa_vw", "moba_v"):
            lane = lax.broadcasted_iota(jnp.int32, (tm, width), 1) & (SLOT - 1)
            y = jnp.where(lane == HEAD_DIM, 1.0, y)
        if name == "nsa_gate":
            y = jax.nn.sigmoid(y)
        outs[name][...] = y.astype(dtype)


def _proj(x2, gain, w_attn, seq, tm=512):
    t, d = x2.shape
    tm = min(tm, seq)
    assert tm % MOBA_BLOCK == 0 and seq % tm == 0
    nblk = tm // MOBA_BLOCK
    out_shape = [jax.ShapeDtypeStruct((t // il, w * il), dt) for _, w, dt, il, _ in _PROJ_SEGS]
    out_specs = [pl.BlockSpec((tm // il, w * il), lambda i: (i, 0)) for _, w, _, il, _ in _PROJ_SEGS]
    kw = MOBA_HEADS * SLOT
    out_shape.append(jax.ShapeDtypeStruct((t // tm, nblk, kw), F32))
    out_specs.append(pl.BlockSpec((1, nblk, kw), lambda i: (i, 0, 0)))
    outs = pl.pallas_call(
        functools.partial(_proj_kernel, seq=seq, tm=tm),
        out_shape=out_shape,
        grid=(t // tm,),
        in_specs=[pl.BlockSpec((tm, d), lambda i: (i, 0)), _const_spec((1, d)), _const_spec((d, _PROJ_COLS))],
        out_specs=out_specs,
        scratch_shapes=[pltpu.VMEM((tm, LANES), F32)],
        compiler_params=_cparams(1),
        name="mixer_in_proj",
    )(x2, gain.reshape(1, d), w_attn)
    res = {s[0]: o for s, o in zip(_PROJ_SEGS, outs)}
    res["moba_kmean"] = outs[-1]
    return res


def _gelu_tanh(x):
    return 0.5 * x * (1.0 + jnp.tanh(math.sqrt(2.0 / math.pi) * (x + 0.044715 * (x * x * x))))


def _compress_kernel(k_ref, v_ref, pek_ref, pev_ref, wk1_ref, wv1_ref, wk2_ref, wv2_ref, kc_ref, vc_ref):
    nr = k_ref.shape[1]
    hid_w = NSA_CMP_HIDDEN
    for x_ref, pe_ref, w1_ref, w2_ref, o_ref in ((k_ref, pek_ref, wk1_ref, wk2_ref, kc_ref),
                                                 (v_ref, pev_ref, wv1_ref, wv2_ref, vc_ref)):
        r = x_ref[0]
        lo = _dot((r + pe_ref[0:1, :]).astype(MXU_DTYPE), w1_ref[0])
        hi = _dot((r + pe_ref[1:2, :]).astype(MXU_DTYPE), w1_ref[1])
        hid = lo + pltpu.roll(hi, nr - 1, 0)
        act = _gelu_tanh(hid).astype(MXU_DTYPE)
        for g in range(NSA_KV_HEADS):
            o_ref[0, g] = _dot(act[:, g * hid_w:(g + 1) * hid_w], w2_ref[...]).astype(o_ref.dtype)


def _compress(kc_in, vc_in, pek, pev, wk1, wv1, wk2, wv2, batch, seq):
    nr = seq // NSA_CMP_STRIDE
    rw = NSA_CMP_STRIDE * LANES
    kin = kc_in.reshape(batch, nr, rw)
    vin = vc_in.reshape(batch, nr, rw)
    hw = NSA_KV_HEADS * NSA_CMP_HIDDEN
    out = jax.ShapeDtypeStruct((batch, NSA_KV_HEADS, nr, SLOT), MXU_DTYPE)
    in_blk = pl.BlockSpec((1, nr, rw), lambda b: (b, 0, 0))
    out_blk = pl.BlockSpec((1, NSA_KV_HEADS, nr, SLOT), lambda b: (b, 0, 0, 0))
    return pl.pallas_call(
        _compress_kernel,
        out_shape=[out, out],
        grid=(batch,),
        in_specs=[in_blk, in_blk, _const_spec((2, rw)), _const_spec((2, rw)),
                  _const_spec((2, rw, hw)), _const_spec((2, rw, hw)),
                  _const_spec((NSA_CMP_HIDDEN, SLOT)), _const_spec((NSA_CMP_HIDDEN, SLOT))],
        out_specs=[out_blk, out_blk],
        compiler_params=_cparams(1),
        name="nsa_compress",
    )(kin, vin, pek, pev, wk1, wv1, wk2, wv2)


def _rank_before(score):
    n, tq = score.shape
    sub_rows = 8
    assert n % sub_rows == 0
    bits = pltpu.bitcast(score, jnp.int32)
    key = bits ^ (jnp.right_shift(bits, 31) & 0x7FFFFFFF)
    keys = [key[b * sub_rows:(b + 1) * sub_rows] for b in range(n // sub_rows)]
    sub = lax.broadcasted_iota(jnp.int32, (sub_rows, tq), 0)
    adj = [k - 1 for k in keys]
    rank = [jnp.zeros((sub_rows, tq), jnp.int32) for _ in keys]
    for j in range(n):
        b, r = divmod(j, sub_rows)
        adj[b] = adj[b] + jnp.where(sub == r, 1, 0)
        row = keys[b][r:r + 1, :]
        rank = [rk + jnp.where(row > a, 1, 0) for rk, a in zip(rank, adj)]
    return jnp.concatenate(rank, axis=0)


def _bias_block(t_ref, q_tile0, k_tile0, n_stack, n_qs, n_kt):
    rows = []
    for r in range(n_stack):
        for a in range(n_qs):
            base = q_tile0 + a - k_tile0 + TILE_PAD
            rows.append(jnp.concatenate([t_ref[base - j, r] for j in range(n_kt)], axis=1))
    return jnp.concatenate(rows, axis=0)


def _flash_chunks(q, k_ref, v_ref, t_ref, q_tile0, n_stack, n_qs, n_chunks, scratch, max_in_producer=True):
    q_s, s_a, s_b, x_a, x_b, p_a, p_b, m_s, a_s, acc_s = scratch
    rows = q.shape[0]
    kw = KV_CHUNK * TK
    stat = (rows, LANES)

    def chunk_rows(c):
        return pl.ds(pl.multiple_of(c * kw, kw), kw)

    k_refs = k_ref if isinstance(k_ref, (list, tuple)) else [k_ref]
    v_refs = v_ref if isinstance(v_ref, (list, tuple)) else [v_ref]
    grp = rows // len(k_refs)
    groups = [slice(i * grp, (i + 1) * grp) for i in range(len(k_refs))]

    def logits_into(c, s_ref, x_ref):
        bias = _bias_block(t_ref, q_tile0, c * KV_CHUNK, n_stack, n_qs, KV_CHUNK)
        for rows_g, kg in zip(groups, k_refs):
            s = _dot_nt(q_s[rows_g, :], kg[chunk_rows(c), :]) + bias[rows_g]
            s_ref[rows_g, :] = s
            if max_in_producer:
                x_ref[rows_g, :] = row_max(s)

    def row_max(s):
        tiles = [s[:, j * TK:(j + 1) * TK] for j in range(KV_CHUNK)]
        return jnp.broadcast_to(jnp.max(functools.reduce(jnp.maximum, tiles), axis=-1, keepdims=True),
                                (s.shape[0], LANES))

    def pv(p_ref, c):
        for rows_g, vg in zip(groups, v_refs):
            acc_s[rows_g, :] = a_s[rows_g, :] * acc_s[rows_g, :] + _dot(p_ref[rows_g, :], vg[chunk_rows(c), :])

    def step(c, s_cur, x_cur, s_nxt, x_nxt, p_cur, p_prev, prefetch=True):
        pv(p_prev, jnp.maximum(c - 1, 0))
        if prefetch:
            logits_into(jnp.minimum(c + 1, n_chunks - 1), s_nxt, x_nxt)
        m_prev = m_s[...]
        m_new = jnp.maximum(m_prev, x_cur[...] if max_in_producer else row_max(s_cur[...]))
        a_s[...] = jnp.exp(m_prev - m_new)
        m_s[...] = m_new
        for j in range(KV_CHUNK):
            p_cur[:, j * TK:(j + 1) * TK] = jnp.exp(s_cur[:, j * TK:(j + 1) * TK] - m_new).astype(MXU_DTYPE)

    q_s[...] = q
    m_s[...] = jnp.full(stat, M_INIT, F32)
    a_s[...] = jnp.ones(stat, F32)
    acc_s[...] = jnp.zeros(stat, F32)
    p_b[...] = jnp.zeros(p_b.shape, MXU_DTYPE)
    logits_into(0, s_a, x_a)

    def body(i, carry):
        step(2 * i, s_a, x_a, s_b, x_b, p_a, p_b)
        step(2 * i + 1, s_b, x_b, s_a, x_a, p_b, p_a)
        return carry

    lax.fori_loop(0, n_chunks // 2, body, 0)
    last = n_chunks - 1

    @pl.when(last % 2 == 0)
    def _():
        step(last, s_a, x_a, s_b, x_b, p_a, p_b, prefetch=False)
        pv(p_a, last)

    @pl.when(last % 2 == 1)
    def _():
        pv(p_b, last)
    return _normalize(acc_s[...])


def _flash_scratch(rows):
    kw = KV_CHUNK * TK
    return ([pltpu.VMEM((rows, LANES), MXU_DTYPE)]
            + [pltpu.VMEM((rows, kw), F32)] * 2 + [pltpu.VMEM((rows, LANES), F32)] * 2
            + [pltpu.VMEM((rows, kw), MXU_DTYPE)] * 2 + [pltpu.VMEM((rows, LANES), F32)] * 3)


def _normalize(acc):
    return acc / jnp.maximum(acc[:, HEAD_DIM:HEAD_DIM + 1], 1e-30)


def _attend_once(q, k, v, bias):
    s = _dot_nt(q, k) + bias
    m = jnp.maximum(jnp.max(s, axis=-1, keepdims=True), M_INIT)
    p = jnp.exp(s - m)
    return _normalize(_dot(p.astype(MXU_DTYPE), v))


def _block_mask_lanes(z_ref, sel_t, tq):
    n = sel_t.shape[0]
    z_ref[...] = jnp.zeros(z_ref.shape, F32)
    z_ref[HEAD_DIM:HEAD_DIM + n, :] = jnp.where(sel_t, 0.0, NEG)
    return z_ref[...].T


def _nsa_kernel(q_ref, gate_ref, kc_ref, vc_ref, ks_ref, vs_ref, kw_ref, vw_ref, c2s_ref,
                tcmp_ref, tsel_ref, twin_ref, o_ref, z_ref, *flash_scratch, n_slc):
    step = pl.program_id(2)
    qt0 = step * NSA_QS
    rep, nqs = NSA_REP, NSA_QS
    tqb = nqs * TQ
    q_all = q_ref[0]
    q = jnp.concatenate([q_all[a * TQ:(a + 1) * TQ, r * SLOT:(r + 1) * SLOT]
                         for r in range(rep) for a in range(nqs)], axis=0)

    n_wt = WIN_TILES + nqs
    w0 = jnp.maximum(qt0 - WIN_TILES, 0)
    w_rows = pl.ds(pl.multiple_of(w0 * TK, TK), n_wt * TK)
    o_w = _attend_once(q, kw_ref[0, w_rows, :], vw_ref[0, w_rows, :],
                       _bias_block(twin_ref, qt0, w0, rep, nqs, n_wt))

    bias_c = jnp.concatenate([tcmp_ref[a, r] for r in range(rep) for a in range(nqs)], axis=0)
    sc = _dot_nt(q, kc_ref[0, 0]) + bias_c
    mc = jnp.maximum(jnp.max(sc, axis=-1, keepdims=True), M_INIT)
    pc = jnp.exp(sc - mc)
    pc = pc / jnp.maximum(jnp.sum(pc, axis=-1, keepdims=True), 1e-30)
    o_c = _dot(pc.astype(MXU_DTYPE), vc_ref[0, 0])

    p_sum = pc[0:tqb]
    for r in range(1, rep):
        p_sum = p_sum + pc[r * tqb:(r + 1) * tqb]
    p_hi = p_sum.astype(MXU_DTYPE)
    p_lo = (p_sum - p_hi.astype(F32)).astype(MXU_DTYPE)
    imp = _dot(p_hi, c2s_ref[...]) + _dot(p_lo, c2s_ref[...])
    imp_t = imp.T[0:n_slc]
    blk = lax.broadcasted_iota(jnp.int32, (n_slc, tqb), 0)
    t_pos = qt0 * TQ + lax.broadcasted_iota(jnp.int32, (n_slc, tqb), 1)
    cur = jnp.right_shift(t_pos, int(math.log2(NSA_SEL_BLOCK)))
    forced = (blk == 0) | (blk == cur) | (blk == cur - 1)
    score = jnp.where(blk > cur, NEG, jnp.where(forced, FORCE_SCORE, imp_t))
    sel_t = _rank_before(score) < min(NSA_N_SEL, n_slc)
    zt = _block_mask_lanes(z_ref, sel_t, tqb)
    q_sel = (q.astype(F32) + jnp.concatenate([zt] * rep, axis=0)).astype(MXU_DTYPE)

    n_chunks = (qt0 + nqs - 1) // KV_CHUNK + 1
    o_s = _flash_chunks(q_sel, ks_ref.at[0], vs_ref.at[0], tsel_ref, qt0, rep, nqs, n_chunks, flash_scratch)

    gates = gate_ref[0]
    for r in range(rep):
        sl = slice(r * tqb, (r + 1) * tqb)
        o_r = (gates[:, 3 * r:3 * r + 1] * o_c[sl] + gates[:, 3 * r + 1:3 * r + 2] * o_s[sl]
               + gates[:, 3 * r + 2:3 * r + 3] * o_w[sl])
        o_ref[0, :, r * SLOT:(r + 1) * SLOT] = o_r.astype(o_ref.dtype)


def _nsa(p, kc, vc, c2s, tcmp, tmain, twin, batch, seq):
    nq = seq // TQ
    n_slc = seq // NSA_SEL_BLOCK
    tqb = NSA_QS * TQ
    assert n_slc <= HEAD_DIM and nq % KV_CHUNK == 0 and nq % NSA_QS == 0 and nq >= WIN_TILES + NSA_QS
    assert TILE_PAD >= KV_CHUNK + NSA_QS - 2
    g_n, rep = NSA_KV_HEADS, NSA_REP
    n_cmp_pad = kc.shape[2]
    r3 = lambda a: a.reshape(batch, seq, a.shape[-1])
    kv_spec = pl.BlockSpec((1, seq, SLOT), lambda b, g, i: (b, 0, g))
    cmp_spec = pl.BlockSpec((1, 1, n_cmp_pad, SLOT), lambda b, g, i: (b, g, 0, 0))
    return pl.pallas_call(
        functools.partial(_nsa_kernel, n_slc=n_slc),
        out_shape=jax.ShapeDtypeStruct((batch, seq, NSA_HEADS * SLOT), MXU_DTYPE),
        grid=(batch, g_n, nq // NSA_QS),
        in_specs=[
            pl.BlockSpec((1, tqb, rep * SLOT), lambda b, g, i: (b, i, g)),
            pl.BlockSpec((1, tqb, SLOT), lambda b, g, i: (b, i, g)),
            cmp_spec, cmp_spec, kv_spec, kv_spec, kv_spec, kv_spec,
            _const_spec(c2s.shape),
            pl.BlockSpec((NSA_QS, rep, TQ, n_cmp_pad), lambda b, g, i: (i, g, 0, 0)),
            pl.BlockSpec((tmain.shape[0], rep, TQ, TK), lambda b, g, i: (0, g, 0, 0),
                         pipeline_mode=pl.Buffered(1)),
            pl.BlockSpec((twin.shape[0], rep, TQ, TK), lambda b, g, i: (0, g, 0, 0),
                         pipeline_mode=pl.Buffered(1)),
        ],
        out_specs=pl.BlockSpec((1, tqb, rep * SLOT), lambda b, g, i: (b, i, g)),
        scratch_shapes=[pltpu.VMEM((LANES, tqb), F32)] + _flash_scratch(rep * tqb),
        compiler_params=_cparams(3),
        name="nsa_attention",
    )(r3(p["nsa_q"]), r3(p["nsa_gate"]), kc, vc, r3(p["nsa_ks"]), r3(p["nsa_vs"]), r3(p["nsa_kw"]),
      r3(p["nsa_vw"]), c2s.astype(MXU_DTYPE), tcmp, tmain, twin)


def _moba_kernel(q_ref, *refs, nb):
    hs = MOBA_HS
    k_refs, v_refs = refs[0:hs], refs[hs:2 * hs]
    km_ref, t_ref, o_ref, z_ref = refs[2 * hs:2 * hs + 4]
    flash_scratch = refs[2 * hs + 4:]
    step = pl.program_id(2)
    tqb = MOBA_QB * MOBA_BLOCK
    n_qs = tqb // TQ
    n_i = lax.broadcasted_iota(jnp.int32, (nb, tqb), 0)
    own = step * MOBA_QB + jnp.right_shift(lax.broadcasted_iota(jnp.int32, (nb, tqb), 1), int(math.log2(MOBA_BLOCK)))
    past = n_i < own
    q_sel = []
    for h in range(hs):
        q = q_ref[0, :, h * SLOT:(h + 1) * SLOT]
        km = km_ref[0, :, h * SLOT:(h + 1) * SLOT]
        km_hi = km.astype(MXU_DTYPE)
        rem = km - km_hi.astype(F32)
        km_mid = rem.astype(MXU_DTYPE)
        km_lo = (rem - km_mid.astype(F32)).astype(MXU_DTYPE)
        gate_t = _dot_nt(km_hi, q) + _dot_nt(km_mid, q) + _dot_nt(km_lo, q)
        score = jnp.where(past, gate_t, NEG)
        sel_t = ((_rank_before(score) < min(MOBA_TOPK, nb - 1)) & past) | (n_i == own)
        zt = _block_mask_lanes(z_ref.at[h], sel_t, tqb)
        q_sel.append((q.astype(F32) + zt).astype(MXU_DTYPE))
    q_tile0 = step * n_qs
    n_chunks = (q_tile0 + n_qs - 1) // KV_CHUNK + 1
    o = _flash_chunks(jnp.concatenate(q_sel, axis=0), [r.at[0] for r in k_refs], [r.at[0] for r in v_refs], t_ref,
                      q_tile0, hs, n_qs, n_chunks, flash_scratch, max_in_producer=False)
    for h in range(hs):
        o_ref[0, :, h * SLOT:(h + 1) * SLOT] = o[h * tqb:(h + 1) * tqb].astype(o_ref.dtype)


def _moba(p, kmean, tmain, batch, seq):
    nq = seq // TQ
    nb = seq // MOBA_BLOCK
    tqb = MOBA_QB * MOBA_BLOCK
    hs = MOBA_HS
    assert seq % tqb == 0 and nb <= HEAD_DIM and MOBA_BLOCK % TQ == 0 and nq % KV_CHUNK == 0
    assert TILE_PAD >= KV_CHUNK + tqb // TQ - 2 and MOBA_HEADS % hs == 0 and NSA_HEADS % hs == 0
    r3 = lambda a: a.reshape(batch, seq, a.shape[-1])
    kv_specs = [pl.BlockSpec((1, seq, SLOT), lambda b, g, i, h=h: (b, 0, g * hs + h)) for h in range(hs)]
    q_spec = pl.BlockSpec((1, tqb, hs * SLOT), lambda b, g, i: (b, i, g))
    return pl.pallas_call(
        functools.partial(_moba_kernel, nb=nb),
        out_shape=jax.ShapeDtypeStruct((batch, seq, MOBA_HEADS * SLOT), MXU_DTYPE),
        grid=(batch, MOBA_HEADS // hs, seq // tqb),
        in_specs=[q_spec] + kv_specs + kv_specs
                 + [pl.BlockSpec((1, nb, hs * SLOT), lambda b, g, i: (b, 0, g)),
                    pl.BlockSpec((tmain.shape[0], hs, TQ, TK), lambda b, g, i: (0, NSA_HEADS // hs + g, 0, 0))],
        out_specs=q_spec,
        scratch_shapes=[pltpu.VMEM((hs, LANES, tqb), F32)] + _flash_scratch(hs * tqb),
        compiler_params=_cparams(3),
        name="moba_attention",
    )(r3(p["moba_q"]), *([r3(p["moba_k"])] * hs), *([r3(p["moba_v"])] * hs),
      kmean.reshape(batch, nb, MOBA_HEADS * SLOT), tmain)


def _dilated_kernel(q_ref, k_ref, v_ref, t_ref, o_ref, lse_ref, *, n_tiles, n_res):
    lane = lax.broadcasted_iota(jnp.int32, (DIL_WB, LANES), 1)
    first = lane < HEAD_DIM
    n_kt = min(2, n_tiles)

    for res in range(n_res):
        lanes = slice(res * LANES, (res + 1) * LANES)

        def tile(n, carry, lanes=lanes):
            kv0 = jnp.maximum(n - 1, 0)
            q_rows = pl.ds(pl.multiple_of(n * DIL_WB, DIL_WB), DIL_WB)
            kv_rows = pl.ds(pl.multiple_of(kv0 * DIL_WB, DIL_WB), n_kt * DIL_WB)
            q = q_ref[0, q_rows, lanes]
            k = k_ref[0, kv_rows, lanes]
            v = v_ref[0, kv_rows, lanes]
            o_heads, lse_heads = [], []
            for h in range(DIL_HEADS_PER_GROUP):
                qh = jnp.where(first if h == 0 else ~first, q, jnp.zeros_like(q))
                bias = jnp.concatenate([t_ref[(n - kv0) - j + (0 if j == 0 else 3 * (1 - (n - kv0))), h]
                                        for j in range(n_kt)], axis=1)
                s = _dot_nt(qh, k) + bias
                m = jnp.max(s, axis=-1, keepdims=True)
                p = jnp.exp(s - m)
                l = jnp.maximum(jnp.sum(p, axis=-1, keepdims=True), 1e-30)
                o_heads.append(_dot(p.astype(MXU_DTYPE), v) / l)
                lse_heads.append(m + jnp.log(l))
            o_ref[0, q_rows, lanes] = jnp.where(first, o_heads[0], o_heads[1])
            lse_ref[0, q_rows, lanes] = jnp.where(first, lse_heads[0], lse_heads[1])
            return carry

        lax.fori_loop(0, n_tiles, tile, 0, unroll=min(DIL_UNROLL, n_tiles))


def _dilated(q, k, v, tdil, dilation, batch, seq):
    ln = seq // dilation
    assert ln % DIL_WB == 0
    view = lambda a: a.reshape(batch, ln, dilation * LANES)
    spec = pl.BlockSpec((1, ln, dilation * LANES), lambda b: (b, 0, 0))
    out = jax.ShapeDtypeStruct((batch, ln, dilation * LANES), F32)
    o, lse = pl.pallas_call(
        functools.partial(_dilated_kernel, n_tiles=ln // DIL_WB, n_res=dilation),
        out_shape=[out, out],
        grid=(batch,),
        in_specs=[spec, spec, spec, _const_spec(tdil.shape)],
        out_specs=[spec, spec],
        compiler_params=_cparams(1),
        name=f"dilated_attention_d{dilation}",
    )(view(q), view(k), view(v), tdil)
    rows = batch * ln
    return o.reshape(rows, dilation * LANES), lse.reshape(rows, dilation * LANES)


def _mixer_out_kernel(x_ref, g_ref, ya_ref, yb_ref, o0_ref, o1_ref, o2_ref, l0_ref, l1_ref, l2_ref,
                      wmg_ref, wa_ref, wb_ref, wc_ref, wo_ref, out_ref, *stage_refs):
    x = x_ref[...]
    d = x.shape[1]
    tm = x.shape[0]
    h = _rms(x, g_ref[...]).astype(MXU_DTYPE)

    def token_major(ref, stage_ref):
        il = ref.shape[1] // LANES
        if il == 1:
            return ref[...]
        for r in range(il):
            stage_ref[pl.ds(r, tm // il, stride=il), :] = ref[:, r * LANES:(r + 1) * LANES]
        return stage_ref[...]

    o0, o1, o2, l0, l1, l2 = [token_major(r, s) for r, s in
                              zip((o0_ref, o1_ref, o2_ref, l0_ref, l1_ref, l2_ref), stage_refs)]
    mx = jnp.maximum(jnp.maximum(l0, l1), l2)
    e0, e1, e2 = jnp.exp(l0 - mx), jnp.exp(l1 - mx), jnp.exp(l2 - mx)
    y_c = (e0 * o0 + e1 * o1 + e2 * o2) / (e0 + e1 + e2)
    merged = jax.nn.sigmoid(_dot(h, wmg_ref[:, 0:d])) * _dot(ya_ref[...], wa_ref[...])
    merged += jax.nn.sigmoid(_dot(h, wmg_ref[:, d:2 * d])) * _dot(yb_ref[...], wb_ref[...])
    merged += jax.nn.sigmoid(_dot(h, wmg_ref[:, 2 * d:3 * d])) * _dot(y_c.astype(MXU_DTYPE), wc_ref[...])
    out_ref[...] = x + _dot(merged.astype(MXU_DTYPE), wo_ref[...])


def _mixer_out(x2, gain, ya, yb, dil, wmg, wa, wb, wc, wo, tm=512):
    t, d = x2.shape
    tm = min(tm, t)
    row = lambda w: pl.BlockSpec((tm, w), lambda i: (i, 0))
    (o0, l0), (o1, l1), (o2, l2) = dil
    dil_arrays = (o0, o1, o2, l0, l1, l2)
    inter = lambda a: pl.BlockSpec((tm // (a.shape[1] // LANES), a.shape[1]), lambda i: (i, 0))
    return pl.pallas_call(
        _mixer_out_kernel,
        out_shape=jax.ShapeDtypeStruct((t, d), F32),
        grid=(t // tm,),
        in_specs=[row(d), _const_spec((1, d)), row(ya.shape[1]), row(yb.shape[1])] + [inter(a) for a in dil_arrays]
                 + [_const_spec(w.shape) for w in (wmg, wa, wb, wc, wo)],
        out_specs=row(d),
        scratch_shapes=[pltpu.VMEM((tm, LANES), F32)] * len(dil_arrays),
        compiler_params=_cparams(1),
        name="mixer_out_proj",
    )(x2, gain.reshape(1, d), ya, yb, o0, o1, o2, l0, l1, l2, wmg, wa, wb, wc, wo)


def _slot_rows(w, n_slots):
    d = w.shape[1]
    w = w.reshape(n_slots, HEAD_DIM, d)
    return jnp.concatenate([w, jnp.zeros_like(w)], axis=1).reshape(n_slots * SLOT, d)


def _compress_weights(pe, w1, w2):
    g_n, hd, half = NSA_KV_HEADS, HEAD_DIM, NSA_CMP_STRIDE
    hid = w1.shape[1]
    w1r = w1.reshape(2, half, hd, hid)
    eye = jnp.eye(g_n, dtype=w1.dtype)
    w1p = jnp.einsum("plch,gk->plgckh", w1r, eye).reshape(2, half * g_n * hd, g_n * hid)
    pep = jnp.broadcast_to(pe.reshape(2, half, 1, hd), (2, half, g_n, hd)).reshape(2, half * g_n * hd)
    w2p = jnp.concatenate([w2, jnp.zeros_like(w2)], axis=1)
    return pep, w1p.astype(MXU_DTYPE), w2p.astype(MXU_DTYPE)


def _cmp_to_slc(seq, n_cmp_pad):
    n_cmp = (seq - NSA_CMP_BLOCK) // NSA_CMP_STRIDE + 1
    n_slc = seq // NSA_SEL_BLOCK
    c_start = np.arange(n_cmp_pad) * NSA_CMP_STRIDE
    s_start = np.arange(LANES) * NSA_SEL_BLOCK
    ov = ((c_start[:, None] < s_start[None, :] + NSA_SEL_BLOCK) & (c_start[:, None] + NSA_CMP_BLOCK > s_start[None, :])
          & (np.arange(n_cmp_pad)[:, None] < n_cmp) & (np.arange(LANES)[None, :] < n_slc))
    return jnp.asarray(ov, F32)


def kernel(x, rel_bias, ffn1_norm, ffn1_w_gate, ffn1_w_up, ffn1_w_down, mix_norm, w_in, nsa_pe_k, nsa_pe_v,
           nsa_phi_k1, nsa_phi_k2, nsa_phi_v1, nsa_phi_v2, w_up_a, w_up_b, w_up_c, w_o, ffn2_norm, ffn2_w_gate,
           ffn2_w_up, ffn2_w_down, final_norm):
    batch, seq, d = x.shape
    depth = w_in.shape[0]
    nq = seq // TQ
    assert seq % (TQ * DIL_PATTERNS[-1][1]) == 0 and all(w // dl == DIL_WB for w, dl in DIL_PATTERNS)
    bf = lambda a: a.astype(MXU_DTYPE)

    n_cmp = (seq - NSA_CMP_BLOCK) // NSA_CMP_STRIDE + 1
    n_cmp_pad = seq // NSA_CMP_STRIDE
    a_heads = list(range(NSA_HEADS))
    ab_heads = list(range(NSA_HEADS + MOBA_HEADS))
    assert TILE_PAD >= max(WIN_TILES, KV_CHUNK - 1)
    t_main = _bias_tiles(rel_bias, n_tiles=nq + TILE_PAD, heads=ab_heads, offset=-TILE_PAD * TQ, name="bias_causal")
    t_win = _bias_tiles(rel_bias, n_tiles=WIN_TILES + NSA_QS + TILE_PAD, heads=a_heads, offset=-TILE_PAD * TQ,
                        hi=NSA_WINDOW - 1, name="bias_window")
    t_cmp = _bias_tiles(rel_bias, n_tiles=nq, heads=a_heads, width=n_cmp_pad, col_mult=NSA_CMP_STRIDE,
                        offset=-(NSA_CMP_BLOCK - 1), n_cols=n_cmp, name="bias_compressed")
    t_dil = []
    for g, (_, dilation) in enumerate(DIL_PATTERNS):
        h0 = NSA_HEADS + MOBA_HEADS + g * DIL_HEADS_PER_GROUP
        t_dil.append(_bias_tiles(rel_bias, n_tiles=3, heads=[h0, h0 + 1], dist_mult=dilation, hi=DIL_WB,
                                 name=f"bias_dilated_{g}"))
    c2s = _cmp_to_slc(seq, n_cmp_pad)

    x2 = x.reshape(batch * seq, d)
    for l in range(depth):
        last = l == depth - 1
        x2 = _ffn(x2, ffn1_norm[l], bf(ffn1_w_gate[l]), bf(ffn1_w_up[l]), bf(ffn1_w_down[l]))

        w_attn, w_merge = _proj_weights(w_in[l])
        p = _proj(x2, mix_norm[l], w_attn, seq)
        pek, wk1, wk2 = _compress_weights(nsa_pe_k[l], nsa_phi_k1[l], nsa_phi_k2[l])
        pev, wv1, wv2 = _compress_weights(nsa_pe_v[l], nsa_phi_v1[l], nsa_phi_v2[l])
        kc, vc = _compress(p["nsa_kc"], p["nsa_vc"], pek, pev, wk1, wv1, wk2, wv2, batch, seq)
        ya = _nsa(p, kc, vc, c2s, t_cmp, t_main, t_win, batch, seq)
        yb = _moba(p, p["moba_kmean"], t_main, batch, seq)
        dil = [_dilated(p[f"dil_q{g}"], p[f"dil_k{g}"], p[f"dil_v{g}"], t_dil[g], dilation, batch, seq)
               for g, (_, dilation) in enumerate(DIL_PATTERNS)]
        x2 = _mixer_out(x2, mix_norm[l], ya.reshape(batch * seq, -1), yb.reshape(batch * seq, -1), dil,
                        w_merge, bf(_slot_rows(w_up_a[l], NSA_HEADS)),
                        bf(_slot_rows(w_up_b[l], MOBA_HEADS)), bf(w_up_c[l]), bf(w_o[l]))

        x2 = _ffn(x2, ffn2_norm[l], bf(ffn2_w_gate[l]), bf(ffn2_w_up[l]), bf(ffn2_w_down[l]),
                  final_gain=final_norm if last else None)
    return x2.reshape(batch, seq, d)
```

```python
import functools
import math

import jax
import jax.numpy as jnp
import numpy as np
from jax import lax
from jax.experimental import pallas as pl
from jax.experimental.pallas import tpu as pltpu

HEAD_DIM = 64
NSA_HEADS = 6
NSA_KV_HEADS = 2
NSA_REP = NSA_HEADS // NSA_KV_HEADS
NSA_CMP_BLOCK = 32
NSA_CMP_STRIDE = 16
NSA_SEL_BLOCK = 64
NSA_N_SEL = 16
NSA_WINDOW = 512
NSA_CMP_HIDDEN = 256
MOBA_HEADS = 4
MOBA_BLOCK = 256
MOBA_TOPK = 3
DIL_PATTERNS = ((128, 1), (512, 4), (2048, 16))
DIL_HEADS_PER_GROUP = 2
DIL_WB = 128
N_BRANCHES = 3
REL_BUCKETS = 32
REL_MAX_EXACT = 16
REL_MAX_DIST = 2048
NORM_EPS = 1e-6
FORCE_SCORE = 1e4
N_HEADS_TOTAL = NSA_HEADS + MOBA_HEADS + DIL_HEADS_PER_GROUP * len(DIL_PATTERNS)

LANES = 128
MXU_COLS = 256
SLOT = LANES
TQ = 128
TK = 128
KV_CHUNK = 4
WIN_TILES = NSA_WINDOW // TK
TILE_PAD = 10
NSA_QS = 2
MOBA_QB = 2
MOBA_HS = 2
DIL_UNROLL = 8
NEG = -1e30
M_INIT = -1e29
MXU_DTYPE = jnp.bfloat16
VMEM_LIMIT = 56 * 1024 * 1024
F32 = jnp.float32


def _cparams(n_grid, vmem=VMEM_LIMIT):
    return pltpu.CompilerParams(dimension_semantics=("arbitrary",) * n_grid, vmem_limit_bytes=vmem)


def _const_spec(shape):
    nd = len(shape)
    return pl.BlockSpec(shape, lambda *_: (0,) * nd, pipeline_mode=pl.Buffered(1))


def _dot(a, b):
    return jnp.dot(a, b, preferred_element_type=F32)


def _dot_nt(a, b, precision=None):
    return lax.dot_general(a, b, (((1,), (1,)), ((), ())), preferred_element_type=F32, precision=precision)


def _rms(x, gain):
    return x * lax.rsqrt(jnp.mean(x * x, axis=-1, keepdims=True) + NORM_EPS) * gain


def _ffn_kernel(x_ref, g_ref, wg_ref, wu_ref, wd_ref, *rest, ff_chunk, final_norm):
    if final_norm:
        fg_ref, o_ref, acc_ref = rest
    else:
        o_ref, acc_ref = rest
    x = x_ref[...]
    h = _rms(x, g_ref[...]).astype(MXU_DTYPE)
    d_ff = wg_ref.shape[1]
    for c in range(d_ff // ff_chunk):
        sl = slice(c * ff_chunk, (c + 1) * ff_chunk)
        a = _dot(h, wg_ref[:, sl])
        u = _dot(h, wu_ref[:, sl])
        z = (a * jax.nn.sigmoid(a) * u).astype(MXU_DTYPE)
        part = _dot(z, wd_ref[sl, :])
        if c == 0:
            acc_ref[...] = part
        else:
            acc_ref[...] += part
    y = x + 0.5 * acc_ref[...]
    if final_norm:
        y = _rms(y, fg_ref[...])
    o_ref[...] = y


def _ffn(x2, gain, wg, wu, wd, final_gain=None, tm=1024):
    t, d = x2.shape
    d_ff = wg.shape[1]
    ff_chunk = 256 if d_ff % 256 == 0 else d_ff
    tm = min(tm, t)
    final = final_gain is not None
    in_specs = [pl.BlockSpec((tm, d), lambda i: (i, 0)), _const_spec((1, d)),
                _const_spec((d, d_ff)), _const_spec((d, d_ff)), _const_spec((d_ff, d))]
    args = [x2, gain.reshape(1, d), wg, wu, wd]
    if final:
        in_specs.append(_const_spec((1, d)))
        args.append(final_gain.reshape(1, d))
    return pl.pallas_call(
        functools.partial(_ffn_kernel, ff_chunk=ff_chunk, final_norm=final),
        out_shape=jax.ShapeDtypeStruct((t, d), F32),
        grid=(t // tm,),
        in_specs=in_specs,
        out_specs=pl.BlockSpec((tm, d), lambda i: (i, 0)),
        scratch_shapes=[pltpu.VMEM((tm, d), F32)],
        compiler_params=_cparams(1),
        name="ffn_swiglu",
    )(*args)


def _bias_tile_kernel(tbl_ref, o_ref, *, heads, width, row_step, col_mult, offset, dist_mult, lo, hi, n_cols):
    d = pl.program_id(0)
    i = lax.broadcasted_iota(jnp.int32, (TQ, width), 0)
    j = lax.broadcasted_iota(jnp.int32, (TQ, width), 1)
    raw = row_step * d + i - col_mult * j + offset
    valid = (raw >= lo) & (raw <= hi) & (j < n_cols)
    n = jnp.maximum(raw * dist_mult, 0)
    nf = jnp.maximum(n, REL_MAX_EXACT).astype(F32)
    large = REL_MAX_EXACT + (jnp.log(nf / REL_MAX_EXACT) / math.log(REL_MAX_DIST / REL_MAX_EXACT)
                             * (REL_BUCKETS - REL_MAX_EXACT)).astype(jnp.int32)
    large = jnp.minimum(large, REL_BUCKETS - 1)
    bucket = jnp.where(n < REL_MAX_EXACT, n, large)
    for hh, head in enumerate(heads):
        val = jnp.zeros((TQ, width), F32)
        for k in range(REL_BUCKETS):
            val = jnp.where(bucket == k, tbl_ref[k, head], val)
        o_ref[0, hh] = jnp.where(valid, val, NEG)


def _bias_tiles(rel_bias, *, n_tiles, heads, width=TK, row_step=TQ, col_mult=1, offset=0, dist_mult=1,
                lo=0, hi=2 ** 30, n_cols=None, name):
    n_cols = width if n_cols is None else n_cols
    nh = len(heads)
    return pl.pallas_call(
        functools.partial(_bias_tile_kernel, heads=tuple(heads), width=width, row_step=row_step,
                          col_mult=col_mult, offset=offset, dist_mult=dist_mult, lo=lo, hi=hi, n_cols=n_cols),
        out_shape=jax.ShapeDtypeStruct((n_tiles, nh, TQ, width), F32),
        grid=(n_tiles,),
        in_specs=[pl.BlockSpec(memory_space=pltpu.SMEM)],
        out_specs=pl.BlockSpec((1, nh, TQ, width), lambda d: (d, 0, 0, 0)),
        compiler_params=_cparams(1),
        name=name,
    )(rel_bias)


_DILATIONS = tuple(d for _, d in DIL_PATTERNS)
_PROJ_SEGS = (
    ("nsa_q", NSA_HEADS * SLOT, MXU_DTYPE, 1, NSA_HEADS),
    ("nsa_kc", LANES, F32, NSA_CMP_STRIDE, 0),
    ("nsa_vc", LANES, F32, NSA_CMP_STRIDE, 0),
    ("nsa_ks", NSA_KV_HEADS * SLOT, MXU_DTYPE, 1, NSA_KV_HEADS),
    ("nsa_vs", NSA_KV_HEADS * SLOT, MXU_DTYPE, 1, NSA_KV_HEADS),
    ("nsa_kw", NSA_KV_HEADS * SLOT, MXU_DTYPE, 1, NSA_KV_HEADS),
    ("nsa_vw", NSA_KV_HEADS * SLOT, MXU_DTYPE, 1, NSA_KV_HEADS),
    ("nsa_gate", NSA_KV_HEADS * SLOT, F32, 1, NSA_KV_HEADS),
    ("moba_q", MOBA_HEADS * SLOT, MXU_DTYPE, 1, MOBA_HEADS),
    ("moba_k", MOBA_HEADS * SLOT, MXU_DTYPE, 1, MOBA_HEADS),
    ("moba_v", MOBA_HEADS * SLOT, MXU_DTYPE, 1, MOBA_HEADS),
) + tuple((f"dil_{n}{g}", LANES, MXU_DTYPE, d, 0) for n in "qkv" for g, d in enumerate(_DILATIONS))
_PROJ_WCOLS = [s[4] * HEAD_DIM if s[4] else s[1] for s in _PROJ_SEGS]
_PROJ_OFFS = np.concatenate([[0], np.cumsum(_PROJ_WCOLS)])
_PROJ_COLS = int(_PROJ_OFFS[-1])


def _proj_weights(w_l):
    hd = HEAD_DIM
    d_model = w_l.shape[0]
    kvw = NSA_KV_HEADS * hd
    widths = (("nsa_q", NSA_HEADS * hd), ("nsa_k_cmp", kvw), ("nsa_v_cmp", kvw), ("nsa_k_sel", kvw),
              ("nsa_v_sel", kvw), ("nsa_k_win", kvw), ("nsa_v_win", kvw), ("nsa_gate", NSA_HEADS * 3),
              ("moba_q", MOBA_HEADS * hd), ("moba_k", MOBA_HEADS * hd), ("moba_v", MOBA_HEADS * hd),
              ("dil_q", 6 * hd), ("dil_k", 6 * hd), ("dil_v", 6 * hd))
    cols, o = {}, 0
    for name, w in widths:
        cols[name] = w_l[:, o:o + w]
        o += w
    qk_scale = hd ** -0.5

    def slots(w, n, real):
        w = w.reshape(d_model, n, real)
        return jnp.pad(w, ((0, 0), (0, 0), (0, hd - real))).reshape(d_model, n * hd)

    parts = {
        "nsa_q": slots(cols["nsa_q"] * qk_scale, NSA_HEADS, hd),
        "nsa_kc": cols["nsa_k_cmp"], "nsa_vc": cols["nsa_v_cmp"],
        "nsa_ks": slots(cols["nsa_k_sel"], NSA_KV_HEADS, hd), "nsa_vs": slots(cols["nsa_v_sel"], NSA_KV_HEADS, hd),
        "nsa_kw": slots(cols["nsa_k_win"], NSA_KV_HEADS, hd), "nsa_vw": slots(cols["nsa_v_win"], NSA_KV_HEADS, hd),
        "nsa_gate": slots(cols["nsa_gate"], NSA_KV_HEADS, NSA_REP * 3),
        "moba_q": slots(cols["moba_q"] * qk_scale, MOBA_HEADS, hd),
        "moba_k": slots(cols["moba_k"], MOBA_HEADS, hd), "moba_v": slots(cols["moba_v"], MOBA_HEADS, hd),
    }
    for g in range(len(DIL_PATTERNS)):
        sl = slice(g * 2 * hd, (g + 1) * 2 * hd)
        parts[f"dil_q{g}"] = cols["dil_q"][:, sl] * qk_scale
        parts[f"dil_k{g}"] = cols["dil_k"][:, sl]
        parts[f"dil_v{g}"] = cols["dil_v"][:, sl]
    w_attn = jnp.concatenate([parts[seg[0]] for seg in _PROJ_SEGS], axis=1)
    return w_attn.astype(MXU_DTYPE), w_l[:, o:].astype(MXU_DTYPE)


def _proj_kernel(x_ref, g_ref, w_ref, *refs, seq, tm):
    names = [s[0] for s in _PROJ_SEGS] + ["moba_kmean"]
    outs = dict(zip(names, refs[:len(names)]))
    stage_ref = refs[len(names)]
    h = _rms(x_ref[...], g_ref[...]).astype(MXU_DTYPE)
    pos0 = (pl.program_id(0) * tm) % seq
    lower = lax.broadcasted_iota(jnp.int32, (tm, LANES), 1) < HEAD_DIM
    group_y, group_c0 = None, 0
    for si, (name, width, dtype, inter, n_slots) in enumerate(_PROJ_SEGS):
        c0 = int(_PROJ_OFFS[si])
        if group_y is None or c0 >= group_c0 + group_y.shape[1]:
            c1 = next((int(o) for o in _PROJ_OFFS[si + 1:] if (int(o) - c0) % MXU_COLS == 0), _PROJ_COLS)
            group_y, group_c0 = _dot(h, w_ref[:, c0:c1]), c0
        y = group_y[:, c0 - group_c0:c0 - group_c0 + _PROJ_WCOLS[si]]
        if n_slots:
            pieces = []
            for pair in range(n_slots // 2):
                v = y[:, pair * LANES:(pair + 1) * LANES]
                pieces += [jnp.where(lower, v, 0.0), jnp.where(lower, pltpu.roll(v, HEAD_DIM, 1), 0.0)]
            y = jnp.concatenate(pieces, axis=1)
        if inter > 1:
            stage_ref[...] = y
            for r in range(inter):
                outs[name][:, r * LANES:(r + 1) * LANES] = stage_ref[pl.ds(r, tm // inter, stride=inter), :].astype(dtype)
            continue
        if name == "moba_k":
            nblk = tm // MOBA_BLOCK
            outs["moba_kmean"][0] = jnp.mean(y.reshape(nblk, MOBA_BLOCK, width), axis=1)
        if name in ("nsa_ks", "moba_k"):
            blk = NSA_SEL_BLOCK if name == "nsa_ks" else MOBA_BLOCK
            row = lax.broadcasted_iota(jnp.int32, (tm, width), 0)
            lane = lax.broadcasted_iota(jnp.int32, (tm, width), 1) & (SLOT - 1)
            hit = (lane - HEAD_DIM) == jnp.right_shift(pos0 + row, int(math.log2(blk)))
            y = jnp.where(hit, 1.0, y)
        if name in ("nsa_vs", "nsa_vw", "moba_v"):
            lane = lax.broadcasted_iota(jnp.int32, (tm, width), 1) & (SLOT - 1)
            y = jnp.where(lane == HEAD_DIM, 1.0, y)
        if name == "nsa_gate":
            y = jax.nn.sigmoid(y)
        outs[name][...] = y.astype(dtype)


def _proj(x2, gain, w_attn, seq, tm=1024):
    t, d = x2.shape
    tm = min(tm, seq)
    assert tm % MOBA_BLOCK == 0 and seq % tm == 0
    nblk = tm // MOBA_BLOCK
    out_shape = [jax.ShapeDtypeStruct((t // il, w * il), dt) for _, w, dt, il, _ in _PROJ_SEGS]
    out_specs = [pl.BlockSpec((tm // il, w * il), lambda i: (i, 0)) for _, w, _, il, _ in _PROJ_SEGS]
    kw = MOBA_HEADS * SLOT
    out_shape.append(jax.ShapeDtypeStruct((t // tm, nblk, kw), F32))
    out_specs.append(pl.BlockSpec((1, nblk, kw), lambda i: (i, 0, 0)))
    outs = pl.pallas_call(
        functools.partial(_proj_kernel, seq=seq, tm=tm),
        out_shape=out_shape,
        grid=(t // tm,),
        in_specs=[pl.BlockSpec((tm, d), lambda i: (i, 0)), _const_spec((1, d)), _const_spec((d, _PROJ_COLS))],
        out_specs=out_specs,
        scratch_shapes=[pltpu.VMEM((tm, LANES), F32)],
        compiler_params=_cparams(1),
        name="mixer_in_proj",
    )(x2, gain.reshape(1, d), w_attn)
    res = {s[0]: o for s, o in zip(_PROJ_SEGS, outs)}
    res["moba_kmean"] = outs[-1]
    return res


def _gelu_tanh(x):
    return 0.5 * x * (1.0 + jnp.tanh(math.sqrt(2.0 / math.pi) * (x + 0.044715 * (x * x * x))))


def _compress_kernel(k_ref, v_ref, pek_ref, pev_ref, wk1_ref, wv1_ref, wk2_ref, wv2_ref, kc_ref, vc_ref):
    nr = k_ref.shape[1]
    hid_w = NSA_CMP_HIDDEN
    for x_ref, pe_ref, w1_ref, w2_ref, o_ref in ((k_ref, pek_ref, wk1_ref, wk2_ref, kc_ref),
                                                 (v_ref, pev_ref, wv1_ref, wv2_ref, vc_ref)):
        r = x_ref[0]
        lo = _dot((r + pe_ref[0:1, :]).astype(MXU_DTYPE), w1_ref[0])
        hi = _dot((r + pe_ref[1:2, :]).astype(MXU_DTYPE), w1_ref[1])
        hid = lo + pltpu.roll(hi, nr - 1, 0)
        act = _gelu_tanh(hid).astype(MXU_DTYPE)
        for g in range(NSA_KV_HEADS):
            o_ref[0, g] = _dot(act[:, g * hid_w:(g + 1) * hid_w], w2_ref[...]).astype(o_ref.dtype)


def _compress(kc_in, vc_in, pek, pev, wk1, wv1, wk2, wv2, batch, seq):
    nr = seq // NSA_CMP_STRIDE
    rw = NSA_CMP_STRIDE * LANES
    kin = kc_in.reshape(batch, nr, rw)
    vin = vc_in.reshape(batch, nr, rw)
    hw = NSA_KV_HEADS * NSA_CMP_HIDDEN
    out = jax.ShapeDtypeStruct((batch, NSA_KV_HEADS, nr, SLOT), MXU_DTYPE)
    in_blk = pl.BlockSpec((1, nr, rw), lambda b: (b, 0, 0))
    out_blk = pl.BlockSpec((1, NSA_KV_HEADS, nr, SLOT), lambda b: (b, 0, 0, 0))
    return pl.pallas_call(
        _compress_kernel,
        out_shape=[out, out],
        grid=(batch,),
        in_specs=[in_blk, in_blk, _const_spec((2, rw)), _const_spec((2, rw)),
                  _const_spec((2, rw, hw)), _const_spec((2, rw, hw)),
                  _const_spec((NSA_CMP_HIDDEN, SLOT)), _const_spec((NSA_CMP_HIDDEN, SLOT))],
        out_specs=[out_blk, out_blk],
        compiler_params=_cparams(1),
        name="nsa_compress",
    )(kin, vin, pek, pev, wk1, wv1, wk2, wv2)


def _rank_before(score):
    n, tq = score.shape
    sub_rows = 8
    assert n % sub_rows == 0
    bits = pltpu.bitcast(score, jnp.int32)
    key = bits ^ (jnp.right_shift(bits, 31) & 0x7FFFFFFF)
    keys = [key[b * sub_rows:(b + 1) * sub_rows] for b in range(n // sub_rows)]
    sub = lax.broadcasted_iota(jnp.int32, (sub_rows, tq), 0)
    adj = [k - 1 for k in keys]
    rank = [jnp.zeros((sub_rows, tq), jnp.int32) for _ in keys]
    for j in range(n):
        b, r = divmod(j, sub_rows)
        adj[b] = adj[b] + jnp.where(sub == r, 1, 0)
        row = keys[b][r:r + 1, :]
        rank = [rk + jnp.where(row > a, 1, 0) for rk, a in zip(rank, adj)]
    return jnp.concatenate(rank, axis=0)


def _bias_block(t_ref, q_tile0, k_tile0, n_stack, n_qs, n_kt):
    rows = []
    for r in range(n_stack):
        for a in range(n_qs):
            base = q_tile0 + a - k_tile0 + TILE_PAD
            rows.append(jnp.concatenate([t_ref[base - j, r] for j in range(n_kt)], axis=1))
    return jnp.concatenate(rows, axis=0)


def _flash_chunks(q, k_ref, v_ref, t_ref, q_tile0, n_stack, n_qs, n_chunks, scratch, max_in_producer=True):
    q_s, s_a, s_b, x_a, x_b, p_a, p_b, m_s, a_s, acc_s = scratch
    rows = q.shape[0]
    kw = KV_CHUNK * TK
    stat = (rows, LANES)

    def chunk_rows(c):
        return pl.ds(pl.multiple_of(c * kw, kw), kw)

    k_refs = k_ref if isinstance(k_ref, (list, tuple)) else [k_ref]
    v_refs = v_ref if isinstance(v_ref, (list, tuple)) else [v_ref]
    grp = rows // len(k_refs)
    groups = [slice(i * grp, (i + 1) * grp) for i in range(len(k_refs))]

    def logits_into(c, s_ref, x_ref):
        bias = _bias_block(t_ref, q_tile0, c * KV_CHUNK, n_stack, n_qs, KV_CHUNK)
        for rows_g, kg in zip(groups, k_refs):
            s = _dot_nt(q_s[rows_g, :], kg[chunk_rows(c), :]) + bias[rows_g]
            s_ref[rows_g, :] = s
            if max_in_producer:
                x_ref[rows_g, :] = row_max(s)

    def row_max(s):
        tiles = [s[:, j * TK:(j + 1) * TK] for j in range(KV_CHUNK)]
        return jnp.broadcast_to(jnp.max(functools.reduce(jnp.maximum, tiles), axis=-1, keepdims=True),
                                (s.shape[0], LANES))

    def pv(p_ref, c):
        for rows_g, vg in zip(groups, v_refs):
            acc_s[rows_g, :] = a_s[rows_g, :] * acc_s[rows_g, :] + _dot(p_ref[rows_g, :], vg[chunk_rows(c), :])

    def step(c, s_cur, x_cur, s_nxt, x_nxt, p_cur, p_prev, prefetch=True):
        pv(p_prev, jnp.maximum(c - 1, 0))
        if prefetch:
            logits_into(jnp.minimum(c + 1, n_chunks - 1), s_nxt, x_nxt)
        m_prev = m_s[...]
        m_new = jnp.maximum(m_prev, x_cur[...] if max_in_producer else row_max(s_cur[...]))
        a_s[...] = jnp.exp(m_prev - m_new)
        m_s[...] = m_new
        for j in range(KV_CHUNK):
            p_cur[:, j * TK:(j + 1) * TK] = jnp.exp(s_cur[:, j * TK:(j + 1) * TK] - m_new).astype(MXU_DTYPE)

    q_s[...] = q
    m_s[...] = jnp.full(stat, M_INIT, F32)
    a_s[...] = jnp.ones(stat, F32)
    acc_s[...] = jnp.zeros(stat, F32)
    p_b[...] = jnp.zeros(p_b.shape, MXU_DTYPE)
    logits_into(0, s_a, x_a)

    def body(i, carry):
        step(2 * i, s_a, x_a, s_b, x_b, p_a, p_b)
        step(2 * i + 1, s_b, x_b, s_a, x_a, p_b, p_a)
        return carry

    lax.fori_loop(0, n_chunks // 2, body, 0)
    last = n_chunks - 1

    @pl.when(last % 2 == 0)
    def _():
        step(last, s_a, x_a, s_b, x_b, p_a, p_b, prefetch=False)
        pv(p_a, last)

    @pl.when(last % 2 == 1)
    def _():
        pv(p_b, last)
    return _normalize(acc_s[...])


def _flash_scratch(rows):
    kw = KV_CHUNK * TK
    return ([pltpu.VMEM((rows, LANES), MXU_DTYPE)]
            + [pltpu.VMEM((rows, kw), F32)] * 2 + [pltpu.VMEM((rows, LANES), F32)] * 2
            + [pltpu.VMEM((rows, kw), MXU_DTYPE)] * 2 + [pltpu.VMEM((rows, LANES), F32)] * 3)


def _normalize(acc):
    return acc / jnp.maximum(acc[:, HEAD_DIM:HEAD_DIM + 1], 1e-30)


def _attend_once(q, k, v, bias):
    s = _dot_nt(q, k) + bias
    m = jnp.maximum(jnp.max(s, axis=-1, keepdims=True), M_INIT)
    p = jnp.exp(s - m)
    return _normalize(_dot(p.astype(MXU_DTYPE), v))


def _packed_width(n_heads):
    return LANES * ((n_heads + 1) // 2)


def _pack_heads(heads):
    rows = heads[0].shape[0]
    lower = lax.broadcasted_iota(jnp.int32, (rows, LANES), 1) < HEAD_DIM
    tiles = []
    for i in range(0, len(heads), 2):
        upper = pltpu.roll(heads[i + 1], HEAD_DIM, 1) if i + 1 < len(heads) else jnp.zeros_like(heads[i])
        tiles.append(jnp.where(lower, heads[i], upper))
    return jnp.concatenate(tiles, axis=1)


def _block_mask_lanes(z_ref, sel_t, tq):
    n = sel_t.shape[0]
    z_ref[...] = jnp.zeros(z_ref.shape, F32)
    z_ref[HEAD_DIM:HEAD_DIM + n, :] = jnp.where(sel_t, 0.0, NEG)
    return z_ref[...].T


def _nsa_kernel(q_ref, gate_ref, kc_ref, vc_ref, ks_ref, vs_ref, kw_ref, vw_ref, c2s_ref,
                tcmp_ref, tsel_ref, twin_ref, o_ref, z_ref, *flash_scratch, n_slc):
    step = pl.program_id(2)
    qt0 = step * NSA_QS
    rep, nqs = NSA_REP, NSA_QS
    tqb = nqs * TQ
    q_all = q_ref[0]
    q = jnp.concatenate([q_all[a * TQ:(a + 1) * TQ, r * SLOT:(r + 1) * SLOT]
                         for r in range(rep) for a in range(nqs)], axis=0)

    n_wt = WIN_TILES + nqs
    w0 = jnp.maximum(qt0 - WIN_TILES, 0)
    w_rows = pl.ds(pl.multiple_of(w0 * TK, TK), n_wt * TK)
    o_w = _attend_once(q, kw_ref[0, w_rows, :], vw_ref[0, w_rows, :],
                       _bias_block(twin_ref, qt0, w0, rep, nqs, n_wt))

    bias_c = jnp.concatenate([tcmp_ref[a, r] for r in range(rep) for a in range(nqs)], axis=0)
    sc = _dot_nt(q, kc_ref[0, 0]) + bias_c
    mc = jnp.maximum(jnp.max(sc, axis=-1, keepdims=True), M_INIT)
    pc = jnp.exp(sc - mc)
    pc = pc / jnp.maximum(jnp.sum(pc, axis=-1, keepdims=True), 1e-30)
    o_c = _dot(pc.astype(MXU_DTYPE), vc_ref[0, 0])

    p_sum = pc[0:tqb]
    for r in range(1, rep):
        p_sum = p_sum + pc[r * tqb:(r + 1) * tqb]
    p_hi = p_sum.astype(MXU_DTYPE)
    p_lo = (p_sum - p_hi.astype(F32)).astype(MXU_DTYPE)
    imp = _dot(p_hi, c2s_ref[...]) + _dot(p_lo, c2s_ref[...])
    imp_t = imp.T[0:n_slc]
    blk = lax.broadcasted_iota(jnp.int32, (n_slc, tqb), 0)
    t_pos = qt0 * TQ + lax.broadcasted_iota(jnp.int32, (n_slc, tqb), 1)
    cur = jnp.right_shift(t_pos, int(math.log2(NSA_SEL_BLOCK)))
    forced = (blk == 0) | (blk == cur) | (blk == cur - 1)
    score = jnp.where(blk > cur, NEG, jnp.where(forced, FORCE_SCORE, imp_t))
    sel_t = _rank_before(score) < min(NSA_N_SEL, n_slc)
    zt = _block_mask_lanes(z_ref, sel_t, tqb)
    q_sel = (q.astype(F32) + jnp.concatenate([zt] * rep, axis=0)).astype(MXU_DTYPE)

    n_chunks = (qt0 + nqs - 1) // KV_CHUNK + 1
    o_s = _flash_chunks(q_sel, ks_ref.at[0], vs_ref.at[0], tsel_ref, qt0, rep, nqs, n_chunks, flash_scratch)

    gates = gate_ref[0]
    heads = []
    for r in range(rep):
        sl = slice(r * tqb, (r + 1) * tqb)
        heads.append(gates[:, 3 * r:3 * r + 1] * o_c[sl] + gates[:, 3 * r + 1:3 * r + 2] * o_s[sl]
                     + gates[:, 3 * r + 2:3 * r + 3] * o_w[sl])
    o_ref[0] = _pack_heads(heads).astype(o_ref.dtype)


def _nsa(p, kc, vc, c2s, tcmp, tmain, twin, batch, seq):
    nq = seq // TQ
    n_slc = seq // NSA_SEL_BLOCK
    tqb = NSA_QS * TQ
    assert n_slc <= HEAD_DIM and nq % KV_CHUNK == 0 and nq % NSA_QS == 0 and nq >= WIN_TILES + NSA_QS
    assert TILE_PAD >= KV_CHUNK + NSA_QS - 2
    g_n, rep = NSA_KV_HEADS, NSA_REP
    n_cmp_pad = kc.shape[2]
    r3 = lambda a: a.reshape(batch, seq, a.shape[-1])
    kv_spec = pl.BlockSpec((1, seq, SLOT), lambda b, g, i: (b, 0, g))
    cmp_spec = pl.BlockSpec((1, 1, n_cmp_pad, SLOT), lambda b, g, i: (b, g, 0, 0))
    return pl.pallas_call(
        functools.partial(_nsa_kernel, n_slc=n_slc),
        out_shape=jax.ShapeDtypeStruct((batch, seq, g_n * _packed_width(rep)), MXU_DTYPE),
        grid=(batch, g_n, nq // NSA_QS),
        in_specs=[
            pl.BlockSpec((1, tqb, rep * SLOT), lambda b, g, i: (b, i, g)),
            pl.BlockSpec((1, tqb, SLOT), lambda b, g, i: (b, i, g)),
            cmp_spec, cmp_spec, kv_spec, kv_spec, kv_spec, kv_spec,
            _const_spec(c2s.shape),
            pl.BlockSpec((NSA_QS, rep, TQ, n_cmp_pad), lambda b, g, i: (i, g, 0, 0)),
            pl.BlockSpec((tmain.shape[0], rep, TQ, TK), lambda b, g, i: (0, g, 0, 0),
                         pipeline_mode=pl.Buffered(1)),
            pl.BlockSpec((twin.shape[0], rep, TQ, TK), lambda b, g, i: (0, g, 0, 0),
                         pipeline_mode=pl.Buffered(1)),
        ],
        out_specs=pl.BlockSpec((1, tqb, _packed_width(rep)), lambda b, g, i: (b, i, g)),
        scratch_shapes=[pltpu.VMEM((LANES, tqb), F32)] + _flash_scratch(rep * tqb),
        compiler_params=_cparams(3),
        name="nsa_attention",
    )(r3(p["nsa_q"]), r3(p["nsa_gate"]), kc, vc, r3(p["nsa_ks"]), r3(p["nsa_vs"]), r3(p["nsa_kw"]),
      r3(p["nsa_vw"]), c2s.astype(MXU_DTYPE), tcmp, tmain, twin)


def _moba_kernel(q_ref, *refs, nb):
    hs = MOBA_HS
    k_refs, v_refs = refs[0:hs], refs[hs:2 * hs]
    km_ref, t_ref, o_ref, z_ref = refs[2 * hs:2 * hs + 4]
    flash_scratch = refs[2 * hs + 4:]
    step = pl.program_id(2)
    tqb = MOBA_QB * MOBA_BLOCK
    n_qs = tqb // TQ
    n_i = lax.broadcasted_iota(jnp.int32, (nb, tqb), 0)
    own = step * MOBA_QB + jnp.right_shift(lax.broadcasted_iota(jnp.int32, (nb, tqb), 1), int(math.log2(MOBA_BLOCK)))
    past = n_i < own
    q_sel = []
    for h in range(hs):
        q = q_ref[0, :, h * SLOT:(h + 1) * SLOT]
        km = km_ref[0, :, h * SLOT:(h + 1) * SLOT]
        km_hi = km.astype(MXU_DTYPE)
        rem = km - km_hi.astype(F32)
        km_mid = rem.astype(MXU_DTYPE)
        km_lo = (rem - km_mid.astype(F32)).astype(MXU_DTYPE)
        gate_t = _dot_nt(km_hi, q) + _dot_nt(km_mid, q) + _dot_nt(km_lo, q)
        score = jnp.where(past, gate_t, NEG)
        sel_t = ((_rank_before(score) < min(MOBA_TOPK, nb - 1)) & past) | (n_i == own)
        zt = _block_mask_lanes(z_ref.at[h], sel_t, tqb)
        q_sel.append((q.astype(F32) + zt).astype(MXU_DTYPE))
    q_tile0 = step * n_qs
    n_chunks = (q_tile0 + n_qs - 1) // KV_CHUNK + 1
    o = _flash_chunks(jnp.concatenate(q_sel, axis=0), [r.at[0] for r in k_refs], [r.at[0] for r in v_refs], t_ref,
                      q_tile0, hs, n_qs, n_chunks, flash_scratch, max_in_producer=False)
    o_ref[0] = _pack_heads([o[h * tqb:(h + 1) * tqb] for h in range(hs)]).astype(o_ref.dtype)


def _moba(p, kmean, tmain, batch, seq):
    nq = seq // TQ
    nb = seq // MOBA_BLOCK
    tqb = MOBA_QB * MOBA_BLOCK
    hs = MOBA_HS
    assert seq % tqb == 0 and nb <= HEAD_DIM and MOBA_BLOCK % TQ == 0 and nq % KV_CHUNK == 0
    assert TILE_PAD >= KV_CHUNK + tqb // TQ - 2 and MOBA_HEADS % hs == 0 and NSA_HEADS % hs == 0
    r3 = lambda a: a.reshape(batch, seq, a.shape[-1])
    kv_specs = [pl.BlockSpec((1, seq, SLOT), lambda b, g, i, h=h: (b, 0, g * hs + h)) for h in range(hs)]
    q_spec = pl.BlockSpec((1, tqb, hs * SLOT), lambda b, g, i: (b, i, g))
    return pl.pallas_call(
        functools.partial(_moba_kernel, nb=nb),
        out_shape=jax.ShapeDtypeStruct((batch, seq, MOBA_HEADS // hs * _packed_width(hs)), MXU_DTYPE),
        grid=(batch, MOBA_HEADS // hs, seq // tqb),
        in_specs=[q_spec] + kv_specs + kv_specs
                 + [pl.BlockSpec((1, nb, hs * SLOT), lambda b, g, i: (b, 0, g)),
                    pl.BlockSpec((tmain.shape[0], hs, TQ, TK), lambda b, g, i: (0, NSA_HEADS // hs + g, 0, 0))],
        out_specs=pl.BlockSpec((1, tqb, _packed_width(hs)), lambda b, g, i: (b, i, g)),
        scratch_shapes=[pltpu.VMEM((hs, LANES, tqb), F32)] + _flash_scratch(hs * tqb),
        compiler_params=_cparams(3),
        name="moba_attention",
    )(r3(p["moba_q"]), *([r3(p["moba_k"])] * hs), *([r3(p["moba_v"])] * hs),
      kmean.reshape(batch, nb, MOBA_HEADS * SLOT), tmain)


def _dilated_kernel(q_ref, k_ref, v_ref, t_ref, o_ref, lse_ref, *, n_tiles, n_res):
    lane = lax.broadcasted_iota(jnp.int32, (DIL_WB, LANES), 1)
    first = lane < HEAD_DIM
    n_kt = min(2, n_tiles)

    for res in range(n_res):
        lanes = slice(res * LANES, (res + 1) * LANES)

        def tile(n, carry, lanes=lanes):
            kv0 = jnp.maximum(n - 1, 0)
            q_rows = pl.ds(pl.multiple_of(n * DIL_WB, DIL_WB), DIL_WB)
            kv_rows = pl.ds(pl.multiple_of(kv0 * DIL_WB, DIL_WB), n_kt * DIL_WB)
            q = q_ref[0, q_rows, lanes]
            k = k_ref[0, kv_rows, lanes]
            v = v_ref[0, kv_rows, lanes]
            o_heads, lse_heads = [], []
            for h in range(DIL_HEADS_PER_GROUP):
                qh = jnp.where(first if h == 0 else ~first, q, jnp.zeros_like(q))
                bias = jnp.concatenate([t_ref[(n - kv0) - j + (0 if j == 0 else 3 * (1 - (n - kv0))), h]
                                        for j in range(n_kt)], axis=1)
                s = _dot_nt(qh, k) + bias
                m = jnp.max(s, axis=-1, keepdims=True)
                p = jnp.exp(s - m)
                l = jnp.maximum(jnp.sum(p, axis=-1, keepdims=True), 1e-30)
                o_heads.append(_dot(p.astype(MXU_DTYPE), v) / l)
                lse_heads.append(m + jnp.log(l))
            o_ref[0, q_rows, lanes] = jnp.where(first, o_heads[0], o_heads[1])
            lse_ref[0, q_rows, lanes] = jnp.where(first, lse_heads[0], lse_heads[1])
            return carry

        lax.fori_loop(0, n_tiles, tile, 0, unroll=min(DIL_UNROLL, n_tiles))


def _dilated(q, k, v, tdil, dilation, batch, seq):
    ln = seq // dilation
    assert ln % DIL_WB == 0
    view = lambda a: a.reshape(batch, ln, dilation * LANES)
    spec = pl.BlockSpec((1, ln, dilation * LANES), lambda b: (b, 0, 0))
    out = jax.ShapeDtypeStruct((batch, ln, dilation * LANES), F32)
    o, lse = pl.pallas_call(
        functools.partial(_dilated_kernel, n_tiles=ln // DIL_WB, n_res=dilation),
        out_shape=[out, out],
        grid=(batch,),
        in_specs=[spec, spec, spec, _const_spec(tdil.shape)],
        out_specs=[spec, spec],
        compiler_params=_cparams(1),
        name=f"dilated_attention_d{dilation}",
    )(view(q), view(k), view(v), tdil)
    rows = batch * ln
    return o.reshape(rows, dilation * LANES), lse.reshape(rows, dilation * LANES)


def _mixer_out_kernel(x_ref, g_ref, ya_ref, yb_ref, o0_ref, o1_ref, o2_ref, l0_ref, l1_ref, l2_ref,
                      wmg_ref, wa_ref, wb_ref, wc_ref, wo_ref, out_ref, *stage_refs):
    x = x_ref[...]
    d = x.shape[1]
    tm = x.shape[0]
    h = _rms(x, g_ref[...]).astype(MXU_DTYPE)

    def token_major(ref, stage_ref):
        il = ref.shape[1] // LANES
        if il == 1:
            return ref[...]
        for r in range(il):
            stage_ref[pl.ds(r, tm // il, stride=il), :] = ref[:, r * LANES:(r + 1) * LANES]
        return stage_ref[...]

    o0, o1, o2, l0, l1, l2 = [token_major(r, s) for r, s in
                              zip((o0_ref, o1_ref, o2_ref, l0_ref, l1_ref, l2_ref), stage_refs)]
    mx = jnp.maximum(jnp.maximum(l0, l1), l2)
    e0, e1, e2 = jnp.exp(l0 - mx), jnp.exp(l1 - mx), jnp.exp(l2 - mx)
    y_c = (e0 * o0 + e1 * o1 + e2 * o2) / (e0 + e1 + e2)
    merged = jax.nn.sigmoid(_dot(h, wmg_ref[:, 0:d])) * _dot(ya_ref[...], wa_ref[...])
    merged += jax.nn.sigmoid(_dot(h, wmg_ref[:, d:2 * d])) * _dot(yb_ref[...], wb_ref[...])
    merged += jax.nn.sigmoid(_dot(h, wmg_ref[:, 2 * d:3 * d])) * _dot(y_c.astype(MXU_DTYPE), wc_ref[...])
    out_ref[...] = x + _dot(merged.astype(MXU_DTYPE), wo_ref[...])


def _mixer_out(x2, gain, ya, yb, dil, wmg, wa, wb, wc, wo, tm=1024):
    t, d = x2.shape
    tm = min(tm, t)
    row = lambda w: pl.BlockSpec((tm, w), lambda i: (i, 0))
    (o0, l0), (o1, l1), (o2, l2) = dil
    dil_arrays = (o0, o1, o2, l0, l1, l2)
    inter = lambda a: pl.BlockSpec((tm // (a.shape[1] // LANES), a.shape[1]), lambda i: (i, 0))
    return pl.pallas_call(
        _mixer_out_kernel,
        out_shape=jax.ShapeDtypeStruct((t, d), F32),
        grid=(t // tm,),
        in_specs=[row(d), _const_spec((1, d)), row(ya.shape[1]), row(yb.shape[1])] + [inter(a) for a in dil_arrays]
                 + [_const_spec(w.shape) for w in (wmg, wa, wb, wc, wo)],
        out_specs=row(d),
        scratch_shapes=[pltpu.VMEM((tm, LANES), F32)] * len(dil_arrays),
        compiler_params=_cparams(1),
        name="mixer_out_proj",
    )(x2, gain.reshape(1, d), ya, yb, o0, o1, o2, l0, l1, l2, wmg, wa, wb, wc, wo)


def _packed_rows(w, n_groups, heads_per_group):
    d = w.shape[1]
    w = w.reshape(n_groups, heads_per_group * HEAD_DIM, d)
    pad = _packed_width(heads_per_group) - heads_per_group * HEAD_DIM
    return jnp.pad(w, ((0, 0), (0, pad), (0, 0))).reshape(n_groups * _packed_width(heads_per_group), d)


def _compress_weights(pe, w1, w2):
    g_n, hd, half = NSA_KV_HEADS, HEAD_DIM, NSA_CMP_STRIDE
    hid = w1.shape[1]
    w1r = w1.reshape(2, half, hd, hid)
    eye = jnp.eye(g_n, dtype=w1.dtype)
    w1p = jnp.einsum("plch,gk->plgckh", w1r, eye).reshape(2, half * g_n * hd, g_n * hid)
    pep = jnp.broadcast_to(pe.reshape(2, half, 1, hd), (2, half, g_n, hd)).reshape(2, half * g_n * hd)
    w2p = jnp.concatenate([w2, jnp.zeros_like(w2)], axis=1)
    return pep, w1p.astype(MXU_DTYPE), w2p.astype(MXU_DTYPE)


def _cmp_to_slc(seq, n_cmp_pad):
    n_cmp = (seq - NSA_CMP_BLOCK) // NSA_CMP_STRIDE + 1
    n_slc = seq // NSA_SEL_BLOCK
    c_start = np.arange(n_cmp_pad) * NSA_CMP_STRIDE
    s_start = np.arange(LANES) * NSA_SEL_BLOCK
    ov = ((c_start[:, None] < s_start[None, :] + NSA_SEL_BLOCK) & (c_start[:, None] + NSA_CMP_BLOCK > s_start[None, :])
          & (np.arange(n_cmp_pad)[:, None] < n_cmp) & (np.arange(LANES)[None, :] < n_slc))
    return jnp.asarray(ov, F32)


def kernel(x, rel_bias, ffn1_norm, ffn1_w_gate, ffn1_w_up, ffn1_w_down, mix_norm, w_in, nsa_pe_k, nsa_pe_v,
           nsa_phi_k1, nsa_phi_k2, nsa_phi_v1, nsa_phi_v2, w_up_a, w_up_b, w_up_c, w_o, ffn2_norm, ffn2_w_gate,
           ffn2_w_up, ffn2_w_down, final_norm):
    batch, seq, d = x.shape
    depth = w_in.shape[0]
    nq = seq // TQ
    assert seq % (TQ * DIL_PATTERNS[-1][1]) == 0 and all(w // dl == DIL_WB for w, dl in DIL_PATTERNS)
    bf = lambda a: a.astype(MXU_DTYPE)

    n_cmp = (seq - NSA_CMP_BLOCK) // NSA_CMP_STRIDE + 1
    n_cmp_pad = seq // NSA_CMP_STRIDE
    a_heads = list(range(NSA_HEADS))
    ab_heads = list(range(NSA_HEADS + MOBA_HEADS))
    assert TILE_PAD >= max(WIN_TILES, KV_CHUNK - 1)
    t_main = _bias_tiles(rel_bias, n_tiles=nq + TILE_PAD, heads=ab_heads, offset=-TILE_PAD * TQ, name="bias_causal")
    t_win = _bias_tiles(rel_bias, n_tiles=WIN_TILES + NSA_QS + TILE_PAD, heads=a_heads, offset=-TILE_PAD * TQ,
                        hi=NSA_WINDOW - 1, name="bias_window")
    t_cmp = _bias_tiles(rel_bias, n_tiles=nq, heads=a_heads, width=n_cmp_pad, col_mult=NSA_CMP_STRIDE,
                        offset=-(NSA_CMP_BLOCK - 1), n_cols=n_cmp, name="bias_compressed")
    t_dil = []
    for g, (_, dilation) in enumerate(DIL_PATTERNS):
        h0 = NSA_HEADS + MOBA_HEADS + g * DIL_HEADS_PER_GROUP
        t_dil.append(_bias_tiles(rel_bias, n_tiles=3, heads=[h0, h0 + 1], dist_mult=dilation, hi=DIL_WB,
                                 name=f"bias_dilated_{g}"))
    c2s = _cmp_to_slc(seq, n_cmp_pad)

    x2 = x.reshape(batch * seq, d)
    for l in range(depth):
        last = l == depth - 1
        x2 = _ffn(x2, ffn1_norm[l], bf(ffn1_w_gate[l]), bf(ffn1_w_up[l]), bf(ffn1_w_down[l]))

        w_attn, w_merge = _proj_weights(w_in[l])
        p = _proj(x2, mix_norm[l], w_attn, seq)
        pek, wk1, wk2 = _compress_weights(nsa_pe_k[l], nsa_phi_k1[l], nsa_phi_k2[l])
        pev, wv1, wv2 = _compress_weights(nsa_pe_v[l], nsa_phi_v1[l], nsa_phi_v2[l])
        kc, vc = _compress(p["nsa_kc"], p["nsa_vc"], pek, pev, wk1, wv1, wk2, wv2, batch, seq)
        ya = _nsa(p, kc, vc, c2s, t_cmp, t_main, t_win, batch, seq)
        yb = _moba(p, p["moba_kmean"], t_main, batch, seq)
        dil = [_dilated(p[f"dil_q{g}"], p[f"dil_k{g}"], p[f"dil_v{g}"], t_dil[g], dilation, batch, seq)
               for g, (_, dilation) in enumerate(DIL_PATTERNS)]
        x2 = _mixer_out(x2, mix_norm[l], ya.reshape(batch * seq, -1), yb.reshape(batch * seq, -1), dil,
                        w_merge, bf(_packed_rows(w_up_a[l], NSA_KV_HEADS, NSA_REP)),
                        bf(_packed_rows(w_up_b[l], MOBA_HEADS // MOBA_HS, MOBA_HS)), bf(w_up_c[l]), bf(w_o[l]))

        x2 = _ffn(x2, ffn2_norm[l], bf(ffn2_w_gate[l]), bf(ffn2_w_up[l]), bf(ffn2_w_down[l]),
                  final_gain=final_norm if last else None)
    return x2.reshape(batch, seq, d)
```

```python
import functools
import math

import jax
import jax.numpy as jnp
import numpy as np
from jax import lax
from jax.experimental import pallas as pl
from jax.experimental.pallas import tpu as pltpu

HEAD_DIM = 64
NSA_HEADS = 6
NSA_KV_HEADS = 2
NSA_REP = NSA_HEADS // NSA_KV_HEADS
NSA_CMP_BLOCK = 32
NSA_CMP_STRIDE = 16
NSA_SEL_BLOCK = 64
NSA_N_SEL = 16
NSA_WINDOW = 512
NSA_CMP_HIDDEN = 256
MOBA_HEADS = 4
MOBA_BLOCK = 256
MOBA_TOPK = 3
DIL_PATTERNS = ((128, 1), (512, 4), (2048, 16))
DIL_HEADS_PER_GROUP = 2
DIL_WB = 128
N_BRANCHES = 3
REL_BUCKETS = 32
REL_MAX_EXACT = 16
REL_MAX_DIST = 2048
NORM_EPS = 1e-6
FORCE_SCORE = 1e4
N_HEADS_TOTAL = NSA_HEADS + MOBA_HEADS + DIL_HEADS_PER_GROUP * len(DIL_PATTERNS)

LANES = 128
MXU_COLS = 256
SLOT = LANES
TQ = 128
TK = 128
KV_CHUNK = 4
WIN_TILES = NSA_WINDOW // TK
TILE_PAD = 10
NSA_QS = 2
MOBA_QB = 2
MOBA_HS = 2
DIL_UNROLL = 8
NEG = -1e30
M_INIT = -1e29
MXU_DTYPE = jnp.bfloat16
VMEM_LIMIT = 56 * 1024 * 1024
F32 = jnp.float32


def _cparams(n_grid, vmem=VMEM_LIMIT):
    return pltpu.CompilerParams(dimension_semantics=("arbitrary",) * n_grid, vmem_limit_bytes=vmem)


def _const_spec(shape):
    nd = len(shape)
    return pl.BlockSpec(shape, lambda *_: (0,) * nd, pipeline_mode=pl.Buffered(1))


def _dot(a, b):
    return jnp.dot(a, b, preferred_element_type=F32)


def _dot_nt(a, b, precision=None):
    return lax.dot_general(a, b, (((1,), (1,)), ((), ())), preferred_element_type=F32, precision=precision)


def _rms(x, gain):
    return x * lax.rsqrt(jnp.mean(x * x, axis=-1, keepdims=True) + NORM_EPS) * gain


def _ffn_kernel(x_ref, g_ref, wg_ref, wu_ref, wd_ref, *rest, ff_chunk, final_norm):
    if final_norm:
        fg_ref, o_ref, acc_ref = rest
    else:
        o_ref, acc_ref = rest
    x = x_ref[...]
    h = _rms(x, g_ref[...]).astype(MXU_DTYPE)
    d_ff = wg_ref.shape[1]
    for c in range(d_ff // ff_chunk):
        sl = slice(c * ff_chunk, (c + 1) * ff_chunk)
        a = _dot(h, wg_ref[:, sl])
        u = _dot(h, wu_ref[:, sl])
        z = (a * jax.nn.sigmoid(a) * u).astype(MXU_DTYPE)
        part = _dot(z, wd_ref[sl, :])
        if c == 0:
            acc_ref[...] = part
        else:
            acc_ref[...] += part
    y = x + 0.5 * acc_ref[...]
    if final_norm:
        y = _rms(y, fg_ref[...])
    o_ref[...] = y


def _ffn(x2, gain, wg, wu, wd, final_gain=None, tm=1024):
    t, d = x2.shape
    d_ff = wg.shape[1]
    ff_chunk = 256 if d_ff % 256 == 0 else d_ff
    tm = min(tm, t)
    final = final_gain is not None
    in_specs = [pl.BlockSpec((tm, d), lambda i: (i, 0)), _const_spec((1, d)),
                _const_spec((d, d_ff)), _const_spec((d, d_ff)), _const_spec((d_ff, d))]
    args = [x2, gain.reshape(1, d), wg, wu, wd]
    if final:
        in_specs.append(_const_spec((1, d)))
        args.append(final_gain.reshape(1, d))
    return pl.pallas_call(
        functools.partial(_ffn_kernel, ff_chunk=ff_chunk, final_norm=final),
        out_shape=jax.ShapeDtypeStruct((t, d), F32),
        grid=(t // tm,),
        in_specs=in_specs,
        out_specs=pl.BlockSpec((tm, d), lambda i: (i, 0)),
        scratch_shapes=[pltpu.VMEM((tm, d), F32)],
        compiler_params=_cparams(1),
        name="ffn_swiglu",
    )(*args)


def _bias_tile_kernel(tbl_ref, o_ref, *, heads, width, row_step, col_mult, offset, dist_mult, lo, hi, n_cols):
    d = pl.program_id(0)
    raw_hi = row_step * d + (TQ - 1) + offset
    raw_lo = row_step * d - col_mult * (width - 1) + offset
    all_masked = (raw_hi < lo) | (raw_lo > hi)
    all_far = (raw_lo >= lo) & (raw_hi <= hi) & (raw_lo * dist_mult >= REL_MAX_DIST) & (n_cols == width)

    @pl.when(all_masked)
    def _():
        o_ref[...] = jnp.full(o_ref.shape, NEG, F32)

    @pl.when(all_far)
    def _():
        for hh, head in enumerate(heads):
            o_ref[0, hh] = jnp.full((TQ, width), tbl_ref[REL_BUCKETS - 1, head], F32)

    @pl.when(jnp.logical_not(all_masked | all_far))
    def _():
        i = lax.broadcasted_iota(jnp.int32, (TQ, width), 0)
        j = lax.broadcasted_iota(jnp.int32, (TQ, width), 1)
        raw = row_step * d + i - col_mult * j + offset
        valid = (raw >= lo) & (raw <= hi) & (j < n_cols)
        n = jnp.maximum(raw * dist_mult, 0)
        nf = jnp.maximum(n, REL_MAX_EXACT).astype(F32)
        large = REL_MAX_EXACT + (jnp.log(nf / REL_MAX_EXACT) / math.log(REL_MAX_DIST / REL_MAX_EXACT)
                                 * (REL_BUCKETS - REL_MAX_EXACT)).astype(jnp.int32)
        large = jnp.minimum(large, REL_BUCKETS - 1)
        bucket = jnp.where(n < REL_MAX_EXACT, n, large)
        for hh, head in enumerate(heads):
            val = jnp.zeros((TQ, width), F32)
            for k in range(REL_BUCKETS):
                val = jnp.where(bucket == k, tbl_ref[k, head], val)
            o_ref[0, hh] = jnp.where(valid, val, NEG)


def _bias_tiles(rel_bias, *, n_tiles, heads, width=TK, row_step=TQ, col_mult=1, offset=0, dist_mult=1,
                lo=0, hi=2 ** 30, n_cols=None, name):
    n_cols = width if n_cols is None else n_cols
    nh = len(heads)
    return pl.pallas_call(
        functools.partial(_bias_tile_kernel, heads=tuple(heads), width=width, row_step=row_step,
                          col_mult=col_mult, offset=offset, dist_mult=dist_mult, lo=lo, hi=hi, n_cols=n_cols),
        out_shape=jax.ShapeDtypeStruct((n_tiles, nh, TQ, width), F32),
        grid=(n_tiles,),
        in_specs=[pl.BlockSpec(memory_space=pltpu.SMEM)],
        out_specs=pl.BlockSpec((1, nh, TQ, width), lambda d: (d, 0, 0, 0)),
        compiler_params=_cparams(1),
        name=name,
    )(rel_bias)


_DILATIONS = tuple(d for _, d in DIL_PATTERNS)
_PROJ_SEGS = (
    ("nsa_q", NSA_HEADS * SLOT, MXU_DTYPE, 1, NSA_HEADS),
    ("nsa_kc", LANES, F32, NSA_CMP_STRIDE, 0),
    ("nsa_vc", LANES, F32, NSA_CMP_STRIDE, 0),
    ("nsa_ks", NSA_KV_HEADS * SLOT, MXU_DTYPE, 1, NSA_KV_HEADS),
    ("nsa_vs", NSA_KV_HEADS * SLOT, MXU_DTYPE, 1, NSA_KV_HEADS),
    ("nsa_kw", NSA_KV_HEADS * SLOT, MXU_DTYPE, 1, NSA_KV_HEADS),
    ("nsa_vw", NSA_KV_HEADS * SLOT, MXU_DTYPE, 1, NSA_KV_HEADS),
    ("nsa_gate", NSA_KV_HEADS * SLOT, F32, 1, NSA_KV_HEADS),
    ("moba_q", MOBA_HEADS * SLOT, MXU_DTYPE, 1, MOBA_HEADS),
    ("moba_k", MOBA_HEADS * SLOT, MXU_DTYPE, 1, MOBA_HEADS),
    ("moba_v", MOBA_HEADS * SLOT, MXU_DTYPE, 1, MOBA_HEADS),
) + tuple((f"dil_{n}{g}", LANES, MXU_DTYPE, d, 0) for n in "qkv" for g, d in enumerate(_DILATIONS))
_PROJ_WCOLS = [s[4] * HEAD_DIM if s[4] else s[1] for s in _PROJ_SEGS]
_PROJ_OFFS = np.concatenate([[0], np.cumsum(_PROJ_WCOLS)])
_PROJ_COLS = int(_PROJ_OFFS[-1])


def _proj_weights(w_l):
    hd = HEAD_DIM
    d_model = w_l.shape[0]
    kvw = NSA_KV_HEADS * hd
    widths = (("nsa_q", NSA_HEADS * hd), ("nsa_k_cmp", kvw), ("nsa_v_cmp", kvw), ("nsa_k_sel", kvw),
              ("nsa_v_sel", kvw), ("nsa_k_win", kvw), ("nsa_v_win", kvw), ("nsa_gate", NSA_HEADS * 3),
              ("moba_q", MOBA_HEADS * hd), ("moba_k", MOBA_HEADS * hd), ("moba_v", MOBA_HEADS * hd),
              ("dil_q", 6 * hd), ("dil_k", 6 * hd), ("dil_v", 6 * hd))
    cols, o = {}, 0
    for name, w in widths:
        cols[name] = w_l[:, o:o + w]
        o += w
    qk_scale = hd ** -0.5

    def slots(w, n, real):
        w = w.reshape(d_model, n, real)
        return jnp.pad(w, ((0, 0), (0, 0), (0, hd - real))).reshape(d_model, n * hd)

    parts = {
        "nsa_q": slots(cols["nsa_q"] * qk_scale, NSA_HEADS, hd),
        "nsa_kc": cols["nsa_k_cmp"], "nsa_vc": cols["nsa_v_cmp"],
        "nsa_ks": slots(cols["nsa_k_sel"], NSA_KV_HEADS, hd), "nsa_vs": slots(cols["nsa_v_sel"], NSA_KV_HEADS, hd),
        "nsa_kw": slots(cols["nsa_k_win"], NSA_KV_HEADS, hd), "nsa_vw": slots(cols["nsa_v_win"], NSA_KV_HEADS, hd),
        "nsa_gate": slots(cols["nsa_gate"], NSA_KV_HEADS, NSA_REP * 3),
        "moba_q": slots(cols["moba_q"] * qk_scale, MOBA_HEADS, hd),
        "moba_k": slots(cols["moba_k"], MOBA_HEADS, hd), "moba_v": slots(cols["moba_v"], MOBA_HEADS, hd),
    }
    for g in range(len(DIL_PATTERNS)):
        sl = slice(g * 2 * hd, (g + 1) * 2 * hd)
        parts[f"dil_q{g}"] = cols["dil_q"][:, sl] * qk_scale
        parts[f"dil_k{g}"] = cols["dil_k"][:, sl]
        parts[f"dil_v{g}"] = cols["dil_v"][:, sl]
    w_attn = jnp.concatenate([parts[seg[0]] for seg in _PROJ_SEGS], axis=1)
    return w_attn.astype(MXU_DTYPE), w_l[:, o:].astype(MXU_DTYPE)


def _proj_kernel(x_ref, g_ref, w_ref, *refs, seq, tm):
    names = [s[0] for s in _PROJ_SEGS] + ["moba_kmean"]
    outs = dict(zip(names, refs[:len(names)]))
    stage_ref = refs[len(names)]
    h = _rms(x_ref[...], g_ref[...]).astype(MXU_DTYPE)
    pos0 = (pl.program_id(0) * tm) % seq
    lower = lax.broadcasted_iota(jnp.int32, (tm, LANES), 1) < HEAD_DIM
    group_y, group_c0 = None, 0
    for si, (name, width, dtype, inter, n_slots) in enumerate(_PROJ_SEGS):
        c0 = int(_PROJ_OFFS[si])
        if group_y is None or c0 >= group_c0 + group_y.shape[1]:
            c1 = next((int(o) for o in _PROJ_OFFS[si + 1:] if (int(o) - c0) % MXU_COLS == 0), _PROJ_COLS)
            group_y, group_c0 = _dot(h, w_ref[:, c0:c1]), c0
        y = group_y[:, c0 - group_c0:c0 - group_c0 + _PROJ_WCOLS[si]]
        if n_slots:
            pieces = []
            for pair in range(n_slots // 2):
                v = y[:, pair * LANES:(pair + 1) * LANES]
                pieces += [jnp.where(lower, v, 0.0), jnp.where(lower, pltpu.roll(v, HEAD_DIM, 1), 0.0)]
            y = jnp.concatenate(pieces, axis=1)
        if inter > 1:
            stage_ref[...] = y
            for r in range(inter):
                outs[name][:, r * LANES:(r + 1) * LANES] = stage_ref[pl.ds(r, tm // inter, stride=inter), :].astype(dtype)
            continue
        if name == "moba_k":
            nblk = tm // MOBA_BLOCK
            outs["moba_kmean"][0] = jnp.mean(y.reshape(nblk, MOBA_BLOCK, width), axis=1)
        if name in ("nsa_ks", "moba_k"):
            blk = NSA_SEL_BLOCK if name == "nsa_ks" else MOBA_BLOCK
            row = lax.broadcasted_iota(jnp.int32, (tm, width), 0)
            lane = lax.broadcasted_iota(jnp.int32, (tm, width), 1) & (SLOT - 1)
            hit = (lane - HEAD_DIM) == jnp.right_shift(pos0 + row, int(math.log2(blk)))
            y = jnp.where(hit, 1.0, y)
        if name in ("nsa_vs", "nsa_vw", "moba_v"):
            lane = lax.broadcasted_iota(jnp.int32, (tm, width), 1) & (SLOT - 1)
            y = jnp.where(lane == HEAD_DIM, 1.0, y)
        if name == "nsa_gate":
            y = jax.nn.sigmoid(y)
        outs[name][...] = y.astype(dtype)


def _proj(x2, gain, w_attn, seq, tm=1024):
    t, d = x2.shape
    tm = min(tm, seq)
    assert tm % MOBA_BLOCK == 0 and seq % tm == 0
    nblk = tm // MOBA_BLOCK
    out_shape = [jax.ShapeDtypeStruct((t // il, w * il), dt) for _, w, dt, il, _ in _PROJ_SEGS]
    out_specs = [pl.BlockSpec((tm // il, w * il), lambda i: (i, 0)) for _, w, _, il, _ in _PROJ_SEGS]
    kw = MOBA_HEADS * SLOT
    out_shape.append(jax.ShapeDtypeStruct((t // tm, nblk, kw), F32))
    out_specs.append(pl.BlockSpec((1, nblk, kw), lambda i: (i, 0, 0)))
    outs = pl.pallas_call(
        functools.partial(_proj_kernel, seq=seq, tm=tm),
        out_shape=out_shape,
        grid=(t // tm,),
        in_specs=[pl.BlockSpec((tm, d), lambda i: (i, 0)), _const_spec((1, d)), _const_spec((d, _PROJ_COLS))],
        out_specs=out_specs,
        scratch_shapes=[pltpu.VMEM((tm, LANES), F32)],
        compiler_params=_cparams(1),
        name="mixer_in_proj",
    )(x2, gain.reshape(1, d), w_attn)
    res = {s[0]: o for s, o in zip(_PROJ_SEGS, outs)}
    res["moba_kmean"] = outs[-1]
    return res


def _gelu_tanh(x):
    return 0.5 * x * (1.0 + jnp.tanh(math.sqrt(2.0 / math.pi) * (x + 0.044715 * (x * x * x))))


def _compress_kernel(k_ref, v_ref, pek_ref, pev_ref, wk1_ref, wv1_ref, wk2_ref, wv2_ref, kc_ref, vc_ref):
    nr = k_ref.shape[1]
    hid_w = NSA_CMP_HIDDEN
    for x_ref, pe_ref, w1_ref, w2_ref, o_ref in ((k_ref, pek_ref, wk1_ref, wk2_ref, kc_ref),
                                                 (v_ref, pev_ref, wv1_ref, wv2_ref, vc_ref)):
        r = x_ref[0]
        lo = _dot((r + pe_ref[0:1, :]).astype(MXU_DTYPE), w1_ref[0])
        hi = _dot((r + pe_ref[1:2, :]).astype(MXU_DTYPE), w1_ref[1])
        hid = lo + pltpu.roll(hi, nr - 1, 0)
        act = _gelu_tanh(hid).astype(MXU_DTYPE)
        for g in range(NSA_KV_HEADS):
            o_ref[0, g] = _dot(act[:, g * hid_w:(g + 1) * hid_w], w2_ref[...]).astype(o_ref.dtype)


def _compress(kc_in, vc_in, pek, pev, wk1, wv1, wk2, wv2, batch, seq):
    nr = seq // NSA_CMP_STRIDE
    rw = NSA_CMP_STRIDE * LANES
    kin = kc_in.reshape(batch, nr, rw)
    vin = vc_in.reshape(batch, nr, rw)
    hw = NSA_KV_HEADS * NSA_CMP_HIDDEN
    out = jax.ShapeDtypeStruct((batch, NSA_KV_HEADS, nr, SLOT), MXU_DTYPE)
    in_blk = pl.BlockSpec((1, nr, rw), lambda b: (b, 0, 0))
    out_blk = pl.BlockSpec((1, NSA_KV_HEADS, nr, SLOT), lambda b: (b, 0, 0, 0))
    return pl.pallas_call(
        _compress_kernel,
        out_shape=[out, out],
        grid=(batch,),
        in_specs=[in_blk, in_blk, _const_spec((2, rw)), _const_spec((2, rw)),
                  _const_spec((2, rw, hw)), _const_spec((2, rw, hw)),
                  _const_spec((NSA_CMP_HIDDEN, SLOT)), _const_spec((NSA_CMP_HIDDEN, SLOT))],
        out_specs=[out_blk, out_blk],
        compiler_params=_cparams(1),
        name="nsa_compress",
    )(kin, vin, pek, pev, wk1, wv1, wk2, wv2)


def _rank_before(score):
    n, tq = score.shape
    sub_rows = 8
    assert n % sub_rows == 0
    bits = pltpu.bitcast(score, jnp.int32)
    key = bits ^ (jnp.right_shift(bits, 31) & 0x7FFFFFFF)
    keys = [key[b * sub_rows:(b + 1) * sub_rows] for b in range(n // sub_rows)]
    sub = lax.broadcasted_iota(jnp.int32, (sub_rows, tq), 0)
    adj = [k - 1 for k in keys]
    rank = [jnp.zeros((sub_rows, tq), jnp.int32) for _ in keys]
    for j in range(n):
        b, r = divmod(j, sub_rows)
        adj[b] = adj[b] + jnp.where(sub == r, 1, 0)
        row = keys[b][r:r + 1, :]
        rank = [rk + jnp.where(row > a, 1, 0) for rk, a in zip(rank, adj)]
    return jnp.concatenate(rank, axis=0)


def _bias_block(t_ref, q_tile0, k_tile0, n_stack, n_qs, n_kt):
    rows = []
    for r in range(n_stack):
        for a in range(n_qs):
            base = q_tile0 + a - k_tile0 + TILE_PAD
            rows.append(jnp.concatenate([t_ref[base - j, r] for j in range(n_kt)], axis=1))
    return jnp.concatenate(rows, axis=0)


def _flash_chunks(q, k_ref, v_ref, t_ref, q_tile0, n_stack, n_qs, n_chunks, scratch, max_in_producer=True):
    q_s, s_a, s_b, x_a, x_b, p_a, p_b, m_s, a_s, acc_s = scratch
    rows = q.shape[0]
    kw = KV_CHUNK * TK
    stat = (rows, LANES)

    def chunk_rows(c):
        return pl.ds(pl.multiple_of(c * kw, kw), kw)

    k_refs = k_ref if isinstance(k_ref, (list, tuple)) else [k_ref]
    v_refs = v_ref if isinstance(v_ref, (list, tuple)) else [v_ref]
    grp = rows // len(k_refs)
    groups = [slice(i * grp, (i + 1) * grp) for i in range(len(k_refs))]

    def logits_into(c, s_ref, x_ref):
        bias = _bias_block(t_ref, q_tile0, c * KV_CHUNK, n_stack, n_qs, KV_CHUNK)
        for rows_g, kg in zip(groups, k_refs):
            s = _dot_nt(q_s[rows_g, :], kg[chunk_rows(c), :]) + bias[rows_g]
            s_ref[rows_g, :] = s
            if max_in_producer:
                x_ref[rows_g, :] = row_max(s)

    def row_max(s):
        tiles = [s[:, j * TK:(j + 1) * TK] for j in range(KV_CHUNK)]
        return jnp.broadcast_to(jnp.max(functools.reduce(jnp.maximum, tiles), axis=-1, keepdims=True),
                                (s.shape[0], LANES))

    def pv(p_ref, c):
        for rows_g, vg in zip(groups, v_refs):
            acc_s[rows_g, :] = a_s[rows_g, :] * acc_s[rows_g, :] + _dot(p_ref[rows_g, :], vg[chunk_rows(c), :])

    def step(c, s_cur, x_cur, s_nxt, x_nxt, p_cur, p_prev, prefetch=True):
        pv(p_prev, jnp.maximum(c - 1, 0))
        if prefetch:
            logits_into(jnp.minimum(c + 1, n_chunks - 1), s_nxt, x_nxt)
        m_prev = m_s[...]
        m_new = jnp.maximum(m_prev, x_cur[...] if max_in_producer else row_max(s_cur[...]))
        a_s[...] = jnp.exp(m_prev - m_new)
        m_s[...] = m_new
        for j in range(KV_CHUNK):
            p_cur[:, j * TK:(j + 1) * TK] = jnp.exp(s_cur[:, j * TK:(j + 1) * TK] - m_new).astype(MXU_DTYPE)

    q_s[...] = q
    m_s[...] = jnp.full(stat, M_INIT, F32)
    a_s[...] = jnp.ones(stat, F32)
    acc_s[...] = jnp.zeros(stat, F32)
    p_b[...] = jnp.zeros(p_b.shape, MXU_DTYPE)
    logits_into(0, s_a, x_a)

    def body(i, carry):
        step(2 * i, s_a, x_a, s_b, x_b, p_a, p_b)
        step(2 * i + 1, s_b, x_b, s_a, x_a, p_b, p_a)
        return carry

    lax.fori_loop(0, n_chunks // 2, body, 0)
    last = n_chunks - 1

    @pl.when(last % 2 == 0)
    def _():
        step(last, s_a, x_a, s_b, x_b, p_a, p_b, prefetch=False)
        pv(p_a, last)

    @pl.when(last % 2 == 1)
    def _():
        pv(p_b, last)
    return _normalize(acc_s[...])


def _flash_scratch(rows):
    kw = KV_CHUNK * TK
    return ([pltpu.VMEM((rows, LANES), MXU_DTYPE)]
            + [pltpu.VMEM((rows, kw), F32)] * 2 + [pltpu.VMEM((rows, LANES), F32)] * 2
            + [pltpu.VMEM((rows, kw), MXU_DTYPE)] * 2 + [pltpu.VMEM((rows, LANES), F32)] * 3)


def _normalize(acc):
    return acc / jnp.maximum(acc[:, HEAD_DIM:HEAD_DIM + 1], 1e-30)


def _attend_once(q, k, v, bias):
    s = _dot_nt(q, k) + bias
    m = jnp.maximum(jnp.max(s, axis=-1, keepdims=True), M_INIT)
    p = jnp.exp(s - m)
    return _normalize(_dot(p.astype(MXU_DTYPE), v))


def _packed_width(n_heads):
    return LANES * ((n_heads + 1) // 2)


def _pack_heads(heads):
    rows = heads[0].shape[0]
    lower = lax.broadcasted_iota(jnp.int32, (rows, LANES), 1) < HEAD_DIM
    tiles = []
    for i in range(0, len(heads), 2):
        upper = pltpu.roll(heads[i + 1], HEAD_DIM, 1) if i + 1 < len(heads) else jnp.zeros_like(heads[i])
        tiles.append(jnp.where(lower, heads[i], upper))
    return jnp.concatenate(tiles, axis=1)


def _block_mask_lanes(z_ref, sel_t, tq):
    n = sel_t.shape[0]
    z_ref[...] = jnp.zeros(z_ref.shape, F32)
    z_ref[HEAD_DIM:HEAD_DIM + n, :] = jnp.where(sel_t, 0.0, NEG)
    return z_ref[...].T


def _nsa_kernel(q_ref, gate_ref, kc_ref, vc_ref, ks_ref, vs_ref, kw_ref, vw_ref, c2s_ref,
                tcmp_ref, tsel_ref, twin_ref, o_ref, z_ref, *flash_scratch, n_slc):
    step = pl.program_id(2)
    qt0 = step * NSA_QS
    rep, nqs = NSA_REP, NSA_QS
    tqb = nqs * TQ
    q_all = q_ref[0]
    q = jnp.concatenate([q_all[a * TQ:(a + 1) * TQ, r * SLOT:(r + 1) * SLOT]
                         for r in range(rep) for a in range(nqs)], axis=0)

    n_wt = WIN_TILES + nqs
    w0 = jnp.maximum(qt0 - WIN_TILES, 0)
    w_rows = pl.ds(pl.multiple_of(w0 * TK, TK), n_wt * TK)
    o_w = _attend_once(q, kw_ref[0, w_rows, :], vw_ref[0, w_rows, :],
                       _bias_block(twin_ref, qt0, w0, rep, nqs, n_wt))

    bias_c = jnp.concatenate([tcmp_ref[a, r] for r in range(rep) for a in range(nqs)], axis=0)
    sc = _dot_nt(q, kc_ref[0, 0]) + bias_c
    mc = jnp.maximum(jnp.max(sc, axis=-1, keepdims=True), M_INIT)
    pc = jnp.exp(sc - mc)
    pc = pc / jnp.maximum(jnp.sum(pc, axis=-1, keepdims=True), 1e-30)
    o_c = _dot(pc.astype(MXU_DTYPE), vc_ref[0, 0])

    p_sum = pc[0:tqb]
    for r in range(1, rep):
        p_sum = p_sum + pc[r * tqb:(r + 1) * tqb]
    p_hi = p_sum.astype(MXU_DTYPE)
    p_lo = (p_sum - p_hi.astype(F32)).astype(MXU_DTYPE)
    imp = _dot(p_hi, c2s_ref[...]) + _dot(p_lo, c2s_ref[...])
    imp_t = imp.T[0:n_slc]
    blk = lax.broadcasted_iota(jnp.int32, (n_slc, tqb), 0)
    t_pos = qt0 * TQ + lax.broadcasted_iota(jnp.int32, (n_slc, tqb), 1)
    cur = jnp.right_shift(t_pos, int(math.log2(NSA_SEL_BLOCK)))
    forced = (blk == 0) | (blk == cur) | (blk == cur - 1)
    score = jnp.where(blk > cur, NEG, jnp.where(forced, FORCE_SCORE, imp_t))
    sel_t = _rank_before(score) < min(NSA_N_SEL, n_slc)
    zt = _block_mask_lanes(z_ref, sel_t, tqb)
    q_sel = (q.astype(F32) + jnp.concatenate([zt] * rep, axis=0)).astype(MXU_DTYPE)

    n_chunks = (qt0 + nqs - 1) // KV_CHUNK + 1
    o_s = _flash_chunks(q_sel, ks_ref.at[0], vs_ref.at[0], tsel_ref, qt0, rep, nqs, n_chunks, flash_scratch)

    gates = gate_ref[0]
    heads = []
    for r in range(rep):
        sl = slice(r * tqb, (r + 1) * tqb)
        heads.append(gates[:, 3 * r:3 * r + 1] * o_c[sl] + gates[:, 3 * r + 1:3 * r + 2] * o_s[sl]
                     + gates[:, 3 * r + 2:3 * r + 3] * o_w[sl])
    o_ref[0] = _pack_heads(heads).astype(o_ref.dtype)


def _nsa(p, kc, vc, c2s, tcmp, tmain, twin, batch, seq):
    nq = seq // TQ
    n_slc = seq // NSA_SEL_BLOCK
    tqb = NSA_QS * TQ
    assert n_slc <= HEAD_DIM and nq % KV_CHUNK == 0 and nq % NSA_QS == 0 and nq >= WIN_TILES + NSA_QS
    assert TILE_PAD >= KV_CHUNK + NSA_QS - 2
    g_n, rep = NSA_KV_HEADS, NSA_REP
    n_cmp_pad = kc.shape[2]
    r3 = lambda a: a.reshape(batch, seq, a.shape[-1])
    kv_spec = pl.BlockSpec((1, seq, SLOT), lambda b, g, i: (b, 0, g))
    cmp_spec = pl.BlockSpec((1, 1, n_cmp_pad, SLOT), lambda b, g, i: (b, g, 0, 0))
    return pl.pallas_call(
        functools.partial(_nsa_kernel, n_slc=n_slc),
        out_shape=jax.ShapeDtypeStruct((batch, seq, g_n * _packed_width(rep)), MXU_DTYPE),
        grid=(batch, g_n, nq // NSA_QS),
        in_specs=[
            pl.BlockSpec((1, tqb, rep * SLOT), lambda b, g, i: (b, i, g)),
            pl.BlockSpec((1, tqb, SLOT), lambda b, g, i: (b, i, g)),
            cmp_spec, cmp_spec, kv_spec, kv_spec, kv_spec, kv_spec,
            _const_spec(c2s.shape),
            pl.BlockSpec((NSA_QS, rep, TQ, n_cmp_pad), lambda b, g, i: (i, g, 0, 0)),
            pl.BlockSpec((tmain.shape[0], rep, TQ, TK), lambda b, g, i: (0, g, 0, 0),
                         pipeline_mode=pl.Buffered(1)),
            pl.BlockSpec((twin.shape[0], rep, TQ, TK), lambda b, g, i: (0, g, 0, 0),
                         pipeline_mode=pl.Buffered(1)),
        ],
        out_specs=pl.BlockSpec((1, tqb, _packed_width(rep)), lambda b, g, i: (b, i, g)),
        scratch_shapes=[pltpu.VMEM((LANES, tqb), F32)] + _flash_scratch(rep * tqb),
        compiler_params=_cparams(3),
        name="nsa_attention",
    )(r3(p["nsa_q"]), r3(p["nsa_gate"]), kc, vc, r3(p["nsa_ks"]), r3(p["nsa_vs"]), r3(p["nsa_kw"]),
      r3(p["nsa_vw"]), c2s.astype(MXU_DTYPE), tcmp, tmain, twin)


def _moba_kernel(q_ref, *refs, nb):
    hs = MOBA_HS
    k_refs, v_refs = refs[0:hs], refs[hs:2 * hs]
    km_ref, t_ref, o_ref, z_ref = refs[2 * hs:2 * hs + 4]
    flash_scratch = refs[2 * hs + 4:]
    step = pl.program_id(2)
    tqb = MOBA_QB * MOBA_BLOCK
    n_qs = tqb // TQ
    n_i = lax.broadcasted_iota(jnp.int32, (nb, tqb), 0)
    own = step * MOBA_QB + jnp.right_shift(lax.broadcasted_iota(jnp.int32, (nb, tqb), 1), int(math.log2(MOBA_BLOCK)))
    past = n_i < own
    q_sel = []
    for h in range(hs):
        q = q_ref[0, :, h * SLOT:(h + 1) * SLOT]
        km = km_ref[0, :, h * SLOT:(h + 1) * SLOT]
        km_hi = km.astype(MXU_DTYPE)
        rem = km - km_hi.astype(F32)
        km_mid = rem.astype(MXU_DTYPE)
        km_lo = (rem - km_mid.astype(F32)).astype(MXU_DTYPE)
        gate_t = _dot_nt(km_hi, q) + _dot_nt(km_mid, q) + _dot_nt(km_lo, q)
        score = jnp.where(past, gate_t, NEG)
        sel_t = ((_rank_before(score) < min(MOBA_TOPK, nb - 1)) & past) | (n_i == own)
        zt = _block_mask_lanes(z_ref.at[h], sel_t, tqb)
        q_sel.append((q.astype(F32) + zt).astype(MXU_DTYPE))
    q_tile0 = step * n_qs
    n_chunks = (q_tile0 + n_qs - 1) // KV_CHUNK + 1
    o = _flash_chunks(jnp.concatenate(q_sel, axis=0), [r.at[0] for r in k_refs], [r.at[0] for r in v_refs], t_ref,
                      q_tile0, hs, n_qs, n_chunks, flash_scratch, max_in_producer=False)
    o_ref[0] = _pack_heads([o[h * tqb:(h + 1) * tqb] for h in range(hs)]).astype(o_ref.dtype)


def _moba(p, kmean, tmain, batch, seq):
    nq = seq // TQ
    nb = seq // MOBA_BLOCK
    tqb = MOBA_QB * MOBA_BLOCK
    hs = MOBA_HS
    assert seq % tqb == 0 and nb <= HEAD_DIM and MOBA_BLOCK % TQ == 0 and nq % KV_CHUNK == 0
    assert TILE_PAD >= KV_CHUNK + tqb // TQ - 2 and MOBA_HEADS % hs == 0 and NSA_HEADS % hs == 0
    r3 = lambda a: a.reshape(batch, seq, a.shape[-1])
    kv_specs = [pl.BlockSpec((1, seq, SLOT), lambda b, g, i, h=h: (b, 0, g * hs + h)) for h in range(hs)]
    q_spec = pl.BlockSpec((1, tqb, hs * SLOT), lambda b, g, i: (b, i, g))
    return pl.pallas_call(
        functools.partial(_moba_kernel, nb=nb),
        out_shape=jax.ShapeDtypeStruct((batch, seq, MOBA_HEADS // hs * _packed_width(hs)), MXU_DTYPE),
        grid=(batch, MOBA_HEADS // hs, seq // tqb),
        in_specs=[q_spec] + kv_specs + kv_specs
                 + [pl.BlockSpec((1, nb, hs * SLOT), lambda b, g, i: (b, 0, g)),
                    pl.BlockSpec((tmain.shape[0], hs, TQ, TK), lambda b, g, i: (0, NSA_HEADS // hs + g, 0, 0))],
        out_specs=pl.BlockSpec((1, tqb, _packed_width(hs)), lambda b, g, i: (b, i, g)),
        scratch_shapes=[pltpu.VMEM((hs, LANES, tqb), F32)] + _flash_scratch(hs * tqb),
        compiler_params=_cparams(3),
        name="moba_attention",
    )(r3(p["moba_q"]), *([r3(p["moba_k"])] * hs), *([r3(p["moba_v"])] * hs),
      kmean.reshape(batch, nb, MOBA_HEADS * SLOT), tmain)


def _dilated_kernel(q_ref, k_ref, v_ref, t_ref, o_ref, lse_ref, *, n_tiles, n_res):
    lane = lax.broadcasted_iota(jnp.int32, (DIL_WB, LANES), 1)
    first = lane < HEAD_DIM
    n_kt = min(2, n_tiles)

    for res in range(n_res):
        lanes = slice(res * LANES, (res + 1) * LANES)

        def tile(n, carry, lanes=lanes):
            kv0 = jnp.maximum(n - 1, 0)
            q_rows = pl.ds(pl.multiple_of(n * DIL_WB, DIL_WB), DIL_WB)
            kv_rows = pl.ds(pl.multiple_of(kv0 * DIL_WB, DIL_WB), n_kt * DIL_WB)
            q = q_ref[0, q_rows, lanes]
            k = k_ref[0, kv_rows, lanes]
            v = v_ref[0, kv_rows, lanes]
            o_heads, lse_heads = [], []
            for h in range(DIL_HEADS_PER_GROUP):
                qh = jnp.where(first if h == 0 else ~first, q, jnp.zeros_like(q))
                bias = jnp.concatenate([t_ref[(n - kv0) - j + (0 if j == 0 else 3 * (1 - (n - kv0))), h]
                                        for j in range(n_kt)], axis=1)
                s = _dot_nt(qh, k) + bias
                m = jnp.max(s, axis=-1, keepdims=True)
                p = jnp.exp(s - m)
                l = jnp.maximum(jnp.sum(p, axis=-1, keepdims=True), 1e-30)
                o_heads.append(_dot(p.astype(MXU_DTYPE), v) / l)
                lse_heads.append(m + jnp.log(l))
            o_ref[0, q_rows, lanes] = jnp.where(first, o_heads[0], o_heads[1])
            lse_ref[0, q_rows, lanes] = jnp.where(first, lse_heads[0], lse_heads[1])
            return carry

        lax.fori_loop(0, n_tiles, tile, 0, unroll=min(DIL_UNROLL, n_tiles))


def _dilated(q, k, v, tdil, dilation, batch, seq):
    ln = seq // dilation
    assert ln % DIL_WB == 0
    view = lambda a: a.reshape(batch, ln, dilation * LANES)
    spec = pl.BlockSpec((1, ln, dilation * LANES), lambda b: (b, 0, 0))
    out = jax.ShapeDtypeStruct((batch, ln, dilation * LANES), F32)
    o, lse = pl.pallas_call(
        functools.partial(_dilated_kernel, n_tiles=ln // DIL_WB, n_res=dilation),
        out_shape=[out, out],
        grid=(batch,),
        in_specs=[spec, spec, spec, _const_spec(tdil.shape)],
        out_specs=[spec, spec],
        compiler_params=_cparams(1),
        name=f"dilated_attention_d{dilation}",
    )(view(q), view(k), view(v), tdil)
    rows = batch * ln
    return o.reshape(rows, dilation * LANES), lse.reshape(rows, dilation * LANES)


def _mixer_out_kernel(x_ref, g_ref, ya_ref, yb_ref, o0_ref, o1_ref, o2_ref, l0_ref, l1_ref, l2_ref,
                      wmg_ref, wa_ref, wb_ref, wc_ref, wo_ref, out_ref, *stage_refs):
    x = x_ref[...]
    d = x.shape[1]
    tm = x.shape[0]
    h = _rms(x, g_ref[...]).astype(MXU_DTYPE)

    def token_major(ref, stage_ref):
        il = ref.shape[1] // LANES
        if il == 1:
            return ref[...]
        for r in range(il):
            stage_ref[pl.ds(r, tm // il, stride=il), :] = ref[:, r * LANES:(r + 1) * LANES]
        return stage_ref[...]

    o0, o1, o2, l0, l1, l2 = [token_major(r, s) for r, s in
                              zip((o0_ref, o1_ref, o2_ref, l0_ref, l1_ref, l2_ref), stage_refs)]
    mx = jnp.maximum(jnp.maximum(l0, l1), l2)
    e0, e1, e2 = jnp.exp(l0 - mx), jnp.exp(l1 - mx), jnp.exp(l2 - mx)
    y_c = (e0 * o0 + e1 * o1 + e2 * o2) / (e0 + e1 + e2)
    merged = jax.nn.sigmoid(_dot(h, wmg_ref[:, 0:d])) * _dot(ya_ref[...], wa_ref[...])
    merged += jax.nn.sigmoid(_dot(h, wmg_ref[:, d:2 * d])) * _dot(yb_ref[...], wb_ref[...])
    merged += jax.nn.sigmoid(_dot(h, wmg_ref[:, 2 * d:3 * d])) * _dot(y_c.astype(MXU_DTYPE), wc_ref[...])
    out_ref[...] = x + _dot(merged.astype(MXU_DTYPE), wo_ref[...])


def _mixer_out(x2, gain, ya, yb, dil, wmg, wa, wb, wc, wo, tm=1024):
    t, d = x2.shape
    tm = min(tm, t)
    row = lambda w: pl.BlockSpec((tm, w), lambda i: (i, 0))
    (o0, l0), (o1, l1), (o2, l2) = dil
    dil_arrays = (o0, o1, o2, l0, l1, l2)
    inter = lambda a: pl.BlockSpec((tm // (a.shape[1] // LANES), a.shape[1]), lambda i: (i, 0))
    return pl.pallas_call(
        _mixer_out_kernel,
        out_shape=jax.ShapeDtypeStruct((t, d), F32),
        grid=(t // tm,),
        in_specs=[row(d), _const_spec((1, d)), row(ya.shape[1]), row(yb.shape[1])] + [inter(a) for a in dil_arrays]
                 + [_const_spec(w.shape) for w in (wmg, wa, wb, wc, wo)],
        out_specs=row(d),
        scratch_shapes=[pltpu.VMEM((tm, LANES), F32)] * len(dil_arrays),
        compiler_params=_cparams(1),
        name="mixer_out_proj",
    )(x2, gain.reshape(1, d), ya, yb, o0, o1, o2, l0, l1, l2, wmg, wa, wb, wc, wo)


def _packed_rows(w, n_groups, heads_per_group):
    d = w.shape[1]
    w = w.reshape(n_groups, heads_per_group * HEAD_DIM, d)
    pad = _packed_width(heads_per_group) - heads_per_group * HEAD_DIM
    return jnp.pad(w, ((0, 0), (0, pad), (0, 0))).reshape(n_groups * _packed_width(heads_per_group), d)


def _compress_weights(pe, w1, w2):
    g_n, hd, half = NSA_KV_HEADS, HEAD_DIM, NSA_CMP_STRIDE
    hid = w1.shape[1]
    w1r = w1.reshape(2, half, hd, hid)
    zeros = jnp.zeros_like(w1r)
    per_group = [jnp.concatenate([w1r if k == g else zeros for k in range(g_n)], axis=-1) for g in range(g_n)]
    w1p = jnp.stack(per_group, axis=2).reshape(2, half * g_n * hd, g_n * hid)
    pep = jnp.broadcast_to(pe.reshape(2, half, 1, hd), (2, half, g_n, hd)).reshape(2, half * g_n * hd)
    w2p = jnp.concatenate([w2, jnp.zeros_like(w2)], axis=1)
    return pep, w1p.astype(MXU_DTYPE), w2p.astype(MXU_DTYPE)


def _cmp_to_slc(seq, n_cmp_pad):
    n_cmp = (seq - NSA_CMP_BLOCK) // NSA_CMP_STRIDE + 1
    n_slc = seq // NSA_SEL_BLOCK
    c_start = np.arange(n_cmp_pad) * NSA_CMP_STRIDE
    s_start = np.arange(LANES) * NSA_SEL_BLOCK
    ov = ((c_start[:, None] < s_start[None, :] + NSA_SEL_BLOCK) & (c_start[:, None] + NSA_CMP_BLOCK > s_start[None, :])
          & (np.arange(n_cmp_pad)[:, None] < n_cmp) & (np.arange(LANES)[None, :] < n_slc))
    return jnp.asarray(ov, F32)


def kernel(x, rel_bias, ffn1_norm, ffn1_w_gate, ffn1_w_up, ffn1_w_down, mix_norm, w_in, nsa_pe_k, nsa_pe_v,
           nsa_phi_k1, nsa_phi_k2, nsa_phi_v1, nsa_phi_v2, w_up_a, w_up_b, w_up_c, w_o, ffn2_norm, ffn2_w_gate,
           ffn2_w_up, ffn2_w_down, final_norm):
    batch, seq, d = x.shape
    depth = w_in.shape[0]
    nq = seq // TQ
    assert seq % (TQ * DIL_PATTERNS[-1][1]) == 0 and all(w // dl == DIL_WB for w, dl in DIL_PATTERNS)
    bf = lambda a: a.astype(MXU_DTYPE)

    n_cmp = (seq - NSA_CMP_BLOCK) // NSA_CMP_STRIDE + 1
    n_cmp_pad = seq // NSA_CMP_STRIDE
    a_heads = list(range(NSA_HEADS))
    ab_heads = list(range(NSA_HEADS + MOBA_HEADS))
    assert TILE_PAD >= max(WIN_TILES, KV_CHUNK - 1)
    t_main = _bias_tiles(rel_bias, n_tiles=nq + TILE_PAD, heads=ab_heads, offset=-TILE_PAD * TQ, name="bias_causal")
    t_win = _bias_tiles(rel_bias, n_tiles=WIN_TILES + NSA_QS + TILE_PAD, heads=a_heads, offset=-TILE_PAD * TQ,
                        hi=NSA_WINDOW - 1, name="bias_window")
    t_cmp = _bias_tiles(rel_bias, n_tiles=nq, heads=a_heads, width=n_cmp_pad, col_mult=NSA_CMP_STRIDE,
                        offset=-(NSA_CMP_BLOCK - 1), n_cols=n_cmp, name="bias_compressed")
    t_dil = []
    for g, (_, dilation) in enumerate(DIL_PATTERNS):
        h0 = NSA_HEADS + MOBA_HEADS + g * DIL_HEADS_PER_GROUP
        t_dil.append(_bias_tiles(rel_bias, n_tiles=3, heads=[h0, h0 + 1], dist_mult=dilation, hi=DIL_WB,
                                 name=f"bias_dilated_{g}"))
    c2s = _cmp_to_slc(seq, n_cmp_pad)

    x2 = x.reshape(batch * seq, d)
    for l in range(depth):
        last = l == depth - 1
        x2 = _ffn(x2, ffn1_norm[l], bf(ffn1_w_gate[l]), bf(ffn1_w_up[l]), bf(ffn1_w_down[l]))

        w_attn, w_merge = _proj_weights(w_in[l])
        p = _proj(x2, mix_norm[l], w_attn, seq)
        pek, wk1, wk2 = _compress_weights(nsa_pe_k[l], nsa_phi_k1[l], nsa_phi_k2[l])
        pev, wv1, wv2 = _compress_weights(nsa_pe_v[l], nsa_phi_v1[l], nsa_phi_v2[l])
        kc, vc = _compress(p["nsa_kc"], p["nsa_vc"], pek, pev, wk1, wv1, wk2, wv2, batch, seq)
        ya = _nsa(p, kc, vc, c2s, t_cmp, t_main, t_win, batch, seq)
        yb = _moba(p, p["moba_kmean"], t_main, batch, seq)
        dil = [_dilated(p[f"dil_q{g}"], p[f"dil_k{g}"], p[f"dil_v{g}"], t_dil[g], dilation, batch, seq)
               for g, (_, dilation) in enumerate(DIL_PATTERNS)]
        x2 = _mixer_out(x2, mix_norm[l], ya.reshape(batch * seq, -1), yb.reshape(batch * seq, -1), dil,
                        w_merge, bf(_packed_rows(w_up_a[l], NSA_KV_HEADS, NSA_REP)),
                        bf(_packed_rows(w_up_b[l], MOBA_HEADS // MOBA_HS, MOBA_HS)), bf(w_up_c[l]), bf(w_o[l]))

        x2 = _ffn(x2, ffn2_norm[l], bf(ffn2_w_gate[l]), bf(ffn2_w_up[l]), bf(ffn2_w_down[l]),
                  final_gain=final_norm if last else None)
    return x2.reshape(batch, seq, d)
```

```python
import functools
import math

import jax
import jax.numpy as jnp
import numpy as np
from jax import lax
from jax.experimental import pallas as pl
from jax.experimental.pallas import tpu as pltpu

HEAD_DIM = 64
NSA_HEADS = 6
NSA_KV_HEADS = 2
NSA_REP = NSA_HEADS // NSA_KV_HEADS
NSA_CMP_BLOCK = 32
NSA_CMP_STRIDE = 16
NSA_SEL_BLOCK = 64
NSA_N_SEL = 16
NSA_WINDOW = 512
NSA_CMP_HIDDEN = 256
MOBA_HEADS = 4
MOBA_BLOCK = 256
MOBA_TOPK = 3
DIL_PATTERNS = ((128, 1), (512, 4), (2048, 16))
DIL_HEADS_PER_GROUP = 2
DIL_WB = 128
REL_BUCKETS = 32
REL_MAX_EXACT = 16
REL_MAX_DIST = 2048
NORM_EPS = 1e-6
FORCE_SCORE = 1e4

LANES = 128
MXU_COLS = 256
SLOT = LANES
TQ = 128
TK = 128
KV_CHUNK = 4
WIN_TILES = NSA_WINDOW // TK
TILE_PAD = 10
NSA_QS = 2
MOBA_QB = 2
MOBA_HS = 2
DIL_UNROLL = 8
NEG = -1e30
M_INIT = -1e29
MXU_DTYPE = jnp.bfloat16
VMEM_LIMIT = 56 * 1024 * 1024
F32 = jnp.float32


def _cparams(n_grid, vmem=VMEM_LIMIT):
    return pltpu.CompilerParams(dimension_semantics=("arbitrary",) * n_grid, vmem_limit_bytes=vmem)


def _const_spec(shape):
    nd = len(shape)
    return pl.BlockSpec(shape, lambda *_: (0,) * nd, pipeline_mode=pl.Buffered(1))


def _dot(a, b):
    return jnp.dot(a, b, preferred_element_type=F32)


def _dot_nt(a, b):
    return lax.dot_general(a, b, (((1,), (1,)), ((), ())), preferred_element_type=F32)


def _rms(x, gain):
    return x * lax.rsqrt(jnp.mean(x * x, axis=-1, keepdims=True) + NORM_EPS) * gain


def _ffn_kernel(x_ref, g_ref, wg_ref, wu_ref, wd_ref, *rest, ff_chunk, final_norm):
    if final_norm:
        fg_ref, o_ref, acc_ref = rest
    else:
        o_ref, acc_ref = rest
    x = x_ref[...]
    h = _rms(x, g_ref[...]).astype(MXU_DTYPE)
    d_ff = wg_ref.shape[1]
    for c in range(d_ff // ff_chunk):
        sl = slice(c * ff_chunk, (c + 1) * ff_chunk)
        a = _dot(h, wg_ref[:, sl])
        u = _dot(h, wu_ref[:, sl])
        z = (a * jax.nn.sigmoid(a) * u).astype(MXU_DTYPE)
        part = _dot(z, wd_ref[sl, :])
        if c == 0:
            acc_ref[...] = part
        else:
            acc_ref[...] += part
    y = x + 0.5 * acc_ref[...]
    if final_norm:
        y = _rms(y, fg_ref[...])
    o_ref[...] = y


def _ffn(x2, gain, wg, wu, wd, final_gain=None, tm=1024):
    t, d = x2.shape
    d_ff = wg.shape[1]
    ff_chunk = MXU_COLS if d_ff % MXU_COLS == 0 else d_ff
    tm = min(tm, t)
    final = final_gain is not None
    in_specs = [pl.BlockSpec((tm, d), lambda i: (i, 0)), _const_spec((1, d)),
                _const_spec((d, d_ff)), _const_spec((d, d_ff)), _const_spec((d_ff, d))]
    args = [x2, gain.reshape(1, d), wg, wu, wd]
    if final:
        in_specs.append(_const_spec((1, d)))
        args.append(final_gain.reshape(1, d))
    return pl.pallas_call(
        functools.partial(_ffn_kernel, ff_chunk=ff_chunk, final_norm=final),
        out_shape=jax.ShapeDtypeStruct((t, d), F32),
        grid=(t // tm,),
        in_specs=in_specs,
        out_specs=pl.BlockSpec((tm, d), lambda i: (i, 0)),
        scratch_shapes=[pltpu.VMEM((tm, d), F32)],
        compiler_params=_cparams(1),
        name="ffn_swiglu",
    )(*args)


def _bias_tile_kernel(tbl_ref, o_ref, *, heads, width, row_step, col_mult, offset, dist_mult, lo, hi, n_cols):
    d = pl.program_id(0)
    raw_hi = row_step * d + (TQ - 1) + offset
    raw_lo = row_step * d - col_mult * (width - 1) + offset
    all_masked = (raw_hi < lo) | (raw_lo > hi)
    all_far = (raw_lo >= lo) & (raw_hi <= hi) & (raw_lo * dist_mult >= REL_MAX_DIST) & (n_cols == width)

    @pl.when(all_masked)
    def _():
        o_ref[...] = jnp.full(o_ref.shape, NEG, F32)

    @pl.when(all_far)
    def _():
        for hh, head in enumerate(heads):
            o_ref[0, hh] = jnp.full((TQ, width), tbl_ref[REL_BUCKETS - 1, head], F32)

    @pl.when(jnp.logical_not(all_masked | all_far))
    def _():
        i = lax.broadcasted_iota(jnp.int32, (TQ, width), 0)
        j = lax.broadcasted_iota(jnp.int32, (TQ, width), 1)
        raw = row_step * d + i - col_mult * j + offset
        valid = (raw >= lo) & (raw <= hi) & (j < n_cols)
        n = jnp.maximum(raw * dist_mult, 0)
        nf = jnp.maximum(n, REL_MAX_EXACT).astype(F32)
        large = REL_MAX_EXACT + (jnp.log(nf / REL_MAX_EXACT) / math.log(REL_MAX_DIST / REL_MAX_EXACT)
                                 * (REL_BUCKETS - REL_MAX_EXACT)).astype(jnp.int32)
        large = jnp.minimum(large, REL_BUCKETS - 1)
        bucket = jnp.where(n < REL_MAX_EXACT, n, large)
        for hh, head in enumerate(heads):
            val = jnp.zeros((TQ, width), F32)
            for k in range(REL_BUCKETS):
                val = jnp.where(bucket == k, tbl_ref[k, head], val)
            o_ref[0, hh] = jnp.where(valid, val, NEG)


def _bias_tiles(rel_bias, *, n_tiles, heads, width=TK, row_step=TQ, col_mult=1, offset=0, dist_mult=1,
                lo=0, hi=2 ** 30, n_cols=None, name):
    n_cols = width if n_cols is None else n_cols
    nh = len(heads)
    return pl.pallas_call(
        functools.partial(_bias_tile_kernel, heads=tuple(heads), width=width, row_step=row_step,
                          col_mult=col_mult, offset=offset, dist_mult=dist_mult, lo=lo, hi=hi, n_cols=n_cols),
        out_shape=jax.ShapeDtypeStruct((n_tiles, nh, TQ, width), F32),
        grid=(n_tiles,),
        in_specs=[pl.BlockSpec(memory_space=pltpu.SMEM)],
        out_specs=pl.BlockSpec((1, nh, TQ, width), lambda d: (d, 0, 0, 0)),
        compiler_params=_cparams(1),
        name=name,
    )(rel_bias)


_DILATIONS = tuple(d for _, d in DIL_PATTERNS)
_PROJ_SEGS = (
    ("nsa_q", NSA_HEADS * SLOT, MXU_DTYPE, 1, NSA_HEADS),
    ("nsa_kc", LANES, F32, NSA_CMP_STRIDE, 0),
    ("nsa_vc", LANES, F32, NSA_CMP_STRIDE, 0),
    ("nsa_ks", NSA_KV_HEADS * SLOT, MXU_DTYPE, 1, NSA_KV_HEADS),
    ("nsa_vs", NSA_KV_HEADS * SLOT, MXU_DTYPE, 1, NSA_KV_HEADS),
    ("nsa_kw", NSA_KV_HEADS * SLOT, MXU_DTYPE, 1, NSA_KV_HEADS),
    ("nsa_vw", NSA_KV_HEADS * SLOT, MXU_DTYPE, 1, NSA_KV_HEADS),
    ("nsa_gate", NSA_KV_HEADS * SLOT, F32, 1, NSA_KV_HEADS),
    ("moba_q", MOBA_HEADS * SLOT, MXU_DTYPE, 1, MOBA_HEADS),
    ("moba_k", MOBA_HEADS * SLOT, MXU_DTYPE, 1, MOBA_HEADS),
    ("moba_v", MOBA_HEADS * SLOT, MXU_DTYPE, 1, MOBA_HEADS),
) + tuple((f"dil_{n}{g}", LANES, MXU_DTYPE, d, 0) for n in "qkv" for g, d in enumerate(_DILATIONS))
_PROJ_WCOLS = [s[4] * HEAD_DIM if s[4] else s[1] for s in _PROJ_SEGS]
_PROJ_OFFS = np.concatenate([[0], np.cumsum(_PROJ_WCOLS)])
_PROJ_COLS = int(_PROJ_OFFS[-1])


def _proj_weights(w_l):
    hd = HEAD_DIM
    d_model = w_l.shape[0]
    kvw = NSA_KV_HEADS * hd
    widths = (("nsa_q", NSA_HEADS * hd), ("nsa_k_cmp", kvw), ("nsa_v_cmp", kvw), ("nsa_k_sel", kvw),
              ("nsa_v_sel", kvw), ("nsa_k_win", kvw), ("nsa_v_win", kvw), ("nsa_gate", NSA_HEADS * 3),
              ("moba_q", MOBA_HEADS * hd), ("moba_k", MOBA_HEADS * hd), ("moba_v", MOBA_HEADS * hd),
              ("dil_q", 6 * hd), ("dil_k", 6 * hd), ("dil_v", 6 * hd))
    cols, o = {}, 0
    for name, w in widths:
        cols[name] = w_l[:, o:o + w]
        o += w
    qk_scale = hd ** -0.5

    def slots(w, n, real):
        w = w.reshape(d_model, n, real)
        return jnp.pad(w, ((0, 0), (0, 0), (0, hd - real))).reshape(d_model, n * hd)

    parts = {
        "nsa_q": slots(cols["nsa_q"] * qk_scale, NSA_HEADS, hd),
        "nsa_kc": cols["nsa_k_cmp"], "nsa_vc": cols["nsa_v_cmp"],
        "nsa_ks": slots(cols["nsa_k_sel"], NSA_KV_HEADS, hd), "nsa_vs": slots(cols["nsa_v_sel"], NSA_KV_HEADS, hd),
        "nsa_kw": slots(cols["nsa_k_win"], NSA_KV_HEADS, hd), "nsa_vw": slots(cols["nsa_v_win"], NSA_KV_HEADS, hd),
        "nsa_gate": slots(cols["nsa_gate"], NSA_KV_HEADS, NSA_REP * 3),
        "moba_q": slots(cols["moba_q"] * qk_scale, MOBA_HEADS, hd),
        "moba_k": slots(cols["moba_k"], MOBA_HEADS, hd), "moba_v": slots(cols["moba_v"], MOBA_HEADS, hd),
    }
    for g in range(len(DIL_PATTERNS)):
        sl = slice(g * 2 * hd, (g + 1) * 2 * hd)
        parts[f"dil_q{g}"] = cols["dil_q"][:, sl] * qk_scale
        parts[f"dil_k{g}"] = cols["dil_k"][:, sl]
        parts[f"dil_v{g}"] = cols["dil_v"][:, sl]
    w_attn = jnp.concatenate([parts[seg[0]] for seg in _PROJ_SEGS], axis=1)
    return w_attn.astype(MXU_DTYPE), w_l[:, o:].astype(MXU_DTYPE)


def _proj_kernel(x_ref, g_ref, w_ref, *refs, seq, tm):
    names = [s[0] for s in _PROJ_SEGS] + ["moba_kmean"]
    outs = dict(zip(names, refs[:len(names)]))
    stage_ref = refs[len(names)]
    h = _rms(x_ref[...], g_ref[...]).astype(MXU_DTYPE)
    pos0 = (pl.program_id(0) * tm) % seq
    lower = lax.broadcasted_iota(jnp.int32, (tm, LANES), 1) < HEAD_DIM
    group_y, group_c0 = None, 0
    for si, (name, width, dtype, inter, n_slots) in enumerate(_PROJ_SEGS):
        c0 = int(_PROJ_OFFS[si])
        if group_y is None or c0 >= group_c0 + group_y.shape[1]:
            c1 = next((int(o) for o in _PROJ_OFFS[si + 1:] if (int(o) - c0) % MXU_COLS == 0), _PROJ_COLS)
            group_y, group_c0 = _dot(h, w_ref[:, c0:c1]), c0
        y = group_y[:, c0 - group_c0:c0 - group_c0 + _PROJ_WCOLS[si]]
        if n_slots:
            pieces = []
            for pair in range(n_slots // 2):
                v = y[:, pair * LANES:(pair + 1) * LANES]
                pieces += [jnp.where(lower, v, 0.0), jnp.where(lower, pltpu.roll(v, HEAD_DIM, 1), 0.0)]
            y = jnp.concatenate(pieces, axis=1)
        if inter > 1:
            stage_ref[...] = y
            for r in range(inter):
                outs[name][:, r * LANES:(r + 1) * LANES] = stage_ref[pl.ds(r, tm // inter, stride=inter), :].astype(dtype)
            continue
        if name == "moba_k":
            nblk = tm // MOBA_BLOCK
            outs["moba_kmean"][0] = jnp.mean(y.reshape(nblk, MOBA_BLOCK, width), axis=1)
        if name in ("nsa_ks", "moba_k"):
            blk = NSA_SEL_BLOCK if name == "nsa_ks" else MOBA_BLOCK
            row = lax.broadcasted_iota(jnp.int32, (tm, width), 0)
            lane = lax.broadcasted_iota(jnp.int32, (tm, width), 1) & (SLOT - 1)
            hit = (lane - HEAD_DIM) == jnp.right_shift(pos0 + row, int(math.log2(blk)))
            y = jnp.where(hit, 1.0, y)
        if name in ("nsa_vs", "nsa_vw", "moba_v"):
            lane = lax.broadcasted_iota(jnp.int32, (tm, width), 1) & (SLOT - 1)
            y = jnp.where(lane == HEAD_DIM, 1.0, y)
        if name == "nsa_gate":
            y = jax.nn.sigmoid(y)
        outs[name][...] = y.astype(dtype)


def _proj(x2, gain, w_attn, seq, tm=1024):
    t, d = x2.shape
    tm = min(tm, seq)
    assert tm % MOBA_BLOCK == 0 and seq % tm == 0
    nblk = tm // MOBA_BLOCK
    out_shape = [jax.ShapeDtypeStruct((t // il, w * il), dt) for _, w, dt, il, _ in _PROJ_SEGS]
    out_specs = [pl.BlockSpec((tm // il, w * il), lambda i: (i, 0)) for _, w, _, il, _ in _PROJ_SEGS]
    kw = MOBA_HEADS * SLOT
    out_shape.append(jax.ShapeDtypeStruct((t // tm, nblk, kw), F32))
    out_specs.append(pl.BlockSpec((1, nblk, kw), lambda i: (i, 0, 0)))
    outs = pl.pallas_call(
        functools.partial(_proj_kernel, seq=seq, tm=tm),
        out_shape=out_shape,
        grid=(t // tm,),
        in_specs=[pl.BlockSpec((tm, d), lambda i: (i, 0)), _const_spec((1, d)), _const_spec((d, _PROJ_COLS))],
        out_specs=out_specs,
        scratch_shapes=[pltpu.VMEM((tm, LANES), F32)],
        compiler_params=_cparams(1),
        name="mixer_in_proj",
    )(x2, gain.reshape(1, d), w_attn)
    res = {s[0]: o for s, o in zip(_PROJ_SEGS, outs)}
    res["moba_kmean"] = outs[-1]
    return res


def _gelu_tanh(x):
    return 0.5 * x * (1.0 + jnp.tanh(math.sqrt(2.0 / math.pi) * (x + 0.044715 * (x * x * x))))


def _compress_kernel(k_ref, v_ref, pek_ref, pev_ref, wk1_ref, wv1_ref, wk2_ref, wv2_ref, kc_ref, vc_ref):
    nr = k_ref.shape[1]
    hid_w = NSA_CMP_HIDDEN
    for x_ref, pe_ref, w1_ref, w2_ref, o_ref in ((k_ref, pek_ref, wk1_ref, wk2_ref, kc_ref),
                                                 (v_ref, pev_ref, wv1_ref, wv2_ref, vc_ref)):
        r = x_ref[0]
        lo = _dot((r + pe_ref[0:1, :]).astype(MXU_DTYPE), w1_ref[0])
        hi = _dot((r + pe_ref[1:2, :]).astype(MXU_DTYPE), w1_ref[1])
        hid = lo + pltpu.roll(hi, nr - 1, 0)
        act = _gelu_tanh(hid).astype(MXU_DTYPE)
        for g in range(NSA_KV_HEADS):
            o_ref[0, g] = _dot(act[:, g * hid_w:(g + 1) * hid_w], w2_ref[...]).astype(o_ref.dtype)


def _compress(kc_in, vc_in, pek, pev, wk1, wv1, wk2, wv2, batch, seq):
    nr = seq // NSA_CMP_STRIDE
    rw = NSA_CMP_STRIDE * LANES
    kin = kc_in.reshape(batch, nr, rw)
    vin = vc_in.reshape(batch, nr, rw)
    hw = NSA_KV_HEADS * NSA_CMP_HIDDEN
    out = jax.ShapeDtypeStruct((batch, NSA_KV_HEADS, nr, SLOT), MXU_DTYPE)
    in_blk = pl.BlockSpec((1, nr, rw), lambda b: (b, 0, 0))
    out_blk = pl.BlockSpec((1, NSA_KV_HEADS, nr, SLOT), lambda b: (b, 0, 0, 0))
    return pl.pallas_call(
        _compress_kernel,
        out_shape=[out, out],
        grid=(batch,),
        in_specs=[in_blk, in_blk, _const_spec((2, rw)), _const_spec((2, rw)),
                  _const_spec((2, rw, hw)), _const_spec((2, rw, hw)),
                  _const_spec((NSA_CMP_HIDDEN, SLOT)), _const_spec((NSA_CMP_HIDDEN, SLOT))],
        out_specs=[out_blk, out_blk],
        compiler_params=_cparams(1),
        name="nsa_compress",
    )(kin, vin, pek, pev, wk1, wv1, wk2, wv2)


def _rank_before(score):
    n, tq = score.shape
    sub_rows = 8
    assert n % sub_rows == 0
    bits = pltpu.bitcast(score, jnp.int32)
    key = bits ^ (jnp.right_shift(bits, 31) & 0x7FFFFFFF)
    keys = [key[b * sub_rows:(b + 1) * sub_rows] for b in range(n // sub_rows)]
    sub = lax.broadcasted_iota(jnp.int32, (sub_rows, tq), 0)
    adj = [k - 1 for k in keys]
    rank = [jnp.zeros((sub_rows, tq), jnp.int32) for _ in keys]
    for j in range(n):
        b, r = divmod(j, sub_rows)
        adj[b] = adj[b] + jnp.where(sub == r, 1, 0)
        row = keys[b][r:r + 1, :]
        rank = [rk + jnp.where(row > a, 1, 0) for rk, a in zip(rank, adj)]
    return jnp.concatenate(rank, axis=0)


def _bias_block(t_ref, q_tile0, k_tile0, n_stack, n_qs, n_kt):
    rows = []
    for r in range(n_stack):
        for a in range(n_qs):
            base = q_tile0 + a - k_tile0 + TILE_PAD
            rows.append(jnp.concatenate([t_ref[base - j, r] for j in range(n_kt)], axis=1))
    return jnp.concatenate(rows, axis=0)


def _flash_chunks(q, k_ref, v_ref, t_ref, q_tile0, n_stack, n_qs, n_chunks, scratch, max_in_producer=True):
    q_s, s_a, s_b, x_a, x_b, p_a, p_b, m_s, a_s, acc_s = scratch
    rows = q.shape[0]
    kw = KV_CHUNK * TK
    stat = (rows, LANES)

    def chunk_rows(c):
        return pl.ds(pl.multiple_of(c * kw, kw), kw)

    k_refs = k_ref if isinstance(k_ref, (list, tuple)) else [k_ref]
    v_refs = v_ref if isinstance(v_ref, (list, tuple)) else [v_ref]
    grp = rows // len(k_refs)
    groups = [slice(i * grp, (i + 1) * grp) for i in range(len(k_refs))]

    def logits_into(c, s_ref, x_ref):
        bias = _bias_block(t_ref, q_tile0, c * KV_CHUNK, n_stack, n_qs, KV_CHUNK)
        for rows_g, kg in zip(groups, k_refs):
            s = _dot_nt(q_s[rows_g, :], kg[chunk_rows(c), :]) + bias[rows_g]
            s_ref[rows_g, :] = s
            if max_in_producer:
                x_ref[rows_g, :] = row_max(s)

    def row_max(s):
        tiles = [s[:, j * TK:(j + 1) * TK] for j in range(KV_CHUNK)]
        return jnp.broadcast_to(jnp.max(functools.reduce(jnp.maximum, tiles), axis=-1, keepdims=True),
                                (s.shape[0], LANES))

    def pv(p_ref, c):
        for rows_g, vg in zip(groups, v_refs):
            acc_s[rows_g, :] = a_s[rows_g, :] * acc_s[rows_g, :] + _dot(p_ref[rows_g, :], vg[chunk_rows(c), :])

    def step(c, s_cur, x_cur, s_nxt, x_nxt, p_cur, p_prev, prefetch=True):
        pv(p_prev, jnp.maximum(c - 1, 0))
        if prefetch:
            logits_into(jnp.minimum(c + 1, n_chunks - 1), s_nxt, x_nxt)
        m_prev = m_s[...]
        m_new = jnp.maximum(m_prev, x_cur[...] if max_in_producer else row_max(s_cur[...]))
        a_s[...] = jnp.exp(m_prev - m_new)
        m_s[...] = m_new
        for j in range(KV_CHUNK):
            p_cur[:, j * TK:(j + 1) * TK] = jnp.exp(s_cur[:, j * TK:(j + 1) * TK] - m_new).astype(MXU_DTYPE)

    q_s[...] = q
    m_s[...] = jnp.full(stat, M_INIT, F32)
    a_s[...] = jnp.ones(stat, F32)
    acc_s[...] = jnp.zeros(stat, F32)
    p_b[...] = jnp.zeros(p_b.shape, MXU_DTYPE)
    logits_into(0, s_a, x_a)

    def body(i, carry):
        step(2 * i, s_a, x_a, s_b, x_b, p_a, p_b)
        step(2 * i + 1, s_b, x_b, s_a, x_a, p_b, p_a)
        return carry

    lax.fori_loop(0, n_chunks // 2, body, 0)
    last = n_chunks - 1

    @pl.when(last % 2 == 0)
    def _():
        step(last, s_a, x_a, s_b, x_b, p_a, p_b, prefetch=False)
        pv(p_a, last)

    @pl.when(last % 2 == 1)
    def _():
        pv(p_b, last)
    return _normalize(acc_s[...])


def _flash_scratch(rows):
    kw = KV_CHUNK * TK
    return ([pltpu.VMEM((rows, LANES), MXU_DTYPE)]
            + [pltpu.VMEM((rows, kw), F32)] * 2 + [pltpu.VMEM((rows, LANES), F32)] * 2
            + [pltpu.VMEM((rows, kw), MXU_DTYPE)] * 2 + [pltpu.VMEM((rows, LANES), F32)] * 3)


def _normalize(acc):
    return acc / jnp.maximum(acc[:, HEAD_DIM:HEAD_DIM + 1], 1e-30)


def _attend_once(q, k, v, bias):
    s = _dot_nt(q, k) + bias
    m = jnp.maximum(jnp.max(s, axis=-1, keepdims=True), M_INIT)
    p = jnp.exp(s - m)
    return _normalize(_dot(p.astype(MXU_DTYPE), v))


def _packed_width(n_heads):
    return LANES * ((n_heads + 1) // 2)


def _pack_heads(heads):
    rows = heads[0].shape[0]
    lower = lax.broadcasted_iota(jnp.int32, (rows, LANES), 1) < HEAD_DIM
    tiles = []
    for i in range(0, len(heads), 2):
        upper = pltpu.roll(heads[i + 1], HEAD_DIM, 1) if i + 1 < len(heads) else jnp.zeros_like(heads[i])
        tiles.append(jnp.where(lower, heads[i], upper))
    return jnp.concatenate(tiles, axis=1)


def _block_mask_lanes(z_ref, sel_t, tq):
    n = sel_t.shape[0]
    z_ref[...] = jnp.zeros(z_ref.shape, F32)
    z_ref[HEAD_DIM:HEAD_DIM + n, :] = jnp.where(sel_t, 0.0, NEG)
    return z_ref[...].T


def _nsa_kernel(q_ref, gate_ref, kc_ref, vc_ref, ks_ref, vs_ref, kw_ref, vw_ref, c2s_ref,
                tcmp_ref, tsel_ref, twin_ref, o_ref, z_ref, *flash_scratch, n_slc):
    step = pl.program_id(2)
    qt0 = step * NSA_QS
    rep, nqs = NSA_REP, NSA_QS
    tqb = nqs * TQ
    q_all = q_ref[0]
    q = jnp.concatenate([q_all[a * TQ:(a + 1) * TQ, r * SLOT:(r + 1) * SLOT]
                         for r in range(rep) for a in range(nqs)], axis=0)

    n_wt = WIN_TILES + nqs
    w0 = jnp.maximum(qt0 - WIN_TILES, 0)
    w_rows = pl.ds(pl.multiple_of(w0 * TK, TK), n_wt * TK)
    o_w = _attend_once(q, kw_ref[0, w_rows, :], vw_ref[0, w_rows, :],
                       _bias_block(twin_ref, qt0, w0, rep, nqs, n_wt))

    bias_c = jnp.concatenate([tcmp_ref[a, r] for r in range(rep) for a in range(nqs)], axis=0)
    sc = _dot_nt(q, kc_ref[0, 0]) + bias_c
    mc = jnp.maximum(jnp.max(sc, axis=-1, keepdims=True), M_INIT)
    pc = jnp.exp(sc - mc)
    pc = pc / jnp.maximum(jnp.sum(pc, axis=-1, keepdims=True), 1e-30)
    o_c = _dot(pc.astype(MXU_DTYPE), vc_ref[0, 0])

    p_sum = pc[0:tqb]
    for r in range(1, rep):
        p_sum = p_sum + pc[r * tqb:(r + 1) * tqb]
    p_hi = p_sum.astype(MXU_DTYPE)
    p_lo = (p_sum - p_hi.astype(F32)).astype(MXU_DTYPE)
    imp = _dot(p_hi, c2s_ref[...]) + _dot(p_lo, c2s_ref[...])
    imp_t = imp.T[0:n_slc]
    blk = lax.broadcasted_iota(jnp.int32, (n_slc, tqb), 0)
    t_pos = qt0 * TQ + lax.broadcasted_iota(jnp.int32, (n_slc, tqb), 1)
    cur = jnp.right_shift(t_pos, int(math.log2(NSA_SEL_BLOCK)))
    forced = (blk == 0) | (blk == cur) | (blk == cur - 1)
    score = jnp.where(blk > cur, NEG, jnp.where(forced, FORCE_SCORE, imp_t))
    sel_t = _rank_before(score) < min(NSA_N_SEL, n_slc)
    zt = _block_mask_lanes(z_ref, sel_t, tqb)
    q_sel = (q.astype(F32) + jnp.concatenate([zt] * rep, axis=0)).astype(MXU_DTYPE)

    n_chunks = (qt0 + nqs - 1) // KV_CHUNK + 1
    o_s = _flash_chunks(q_sel, ks_ref.at[0], vs_ref.at[0], tsel_ref, qt0, rep, nqs, n_chunks, flash_scratch)

    gates = gate_ref[0]
    heads = []
    for r in range(rep):
        sl = slice(r * tqb, (r + 1) * tqb)
        heads.append(gates[:, 3 * r:3 * r + 1] * o_c[sl] + gates[:, 3 * r + 1:3 * r + 2] * o_s[sl]
                     + gates[:, 3 * r + 2:3 * r + 3] * o_w[sl])
    o_ref[0] = _pack_heads(heads).astype(o_ref.dtype)


def _nsa(p, kc, vc, c2s, tcmp, tmain, twin, batch, seq):
    nq = seq // TQ
    n_slc = seq // NSA_SEL_BLOCK
    tqb = NSA_QS * TQ
    assert n_slc <= HEAD_DIM and nq % KV_CHUNK == 0 and nq % NSA_QS == 0 and nq >= WIN_TILES + NSA_QS
    assert TILE_PAD >= KV_CHUNK + NSA_QS - 2
    g_n, rep = NSA_KV_HEADS, NSA_REP
    n_cmp_pad = kc.shape[2]
    r3 = lambda a: a.reshape(batch, seq, a.shape[-1])
    kv_spec = pl.BlockSpec((1, seq, SLOT), lambda b, g, i: (b, 0, g))
    cmp_spec = pl.BlockSpec((1, 1, n_cmp_pad, SLOT), lambda b, g, i: (b, g, 0, 0))
    return pl.pallas_call(
        functools.partial(_nsa_kernel, n_slc=n_slc),
        out_shape=jax.ShapeDtypeStruct((batch, seq, g_n * _packed_width(rep)), MXU_DTYPE),
        grid=(batch, g_n, nq // NSA_QS),
        in_specs=[
            pl.BlockSpec((1, tqb, rep * SLOT), lambda b, g, i: (b, i, g)),
            pl.BlockSpec((1, tqb, SLOT), lambda b, g, i: (b, i, g)),
            cmp_spec, cmp_spec, kv_spec, kv_spec, kv_spec, kv_spec,
            _const_spec(c2s.shape),
            pl.BlockSpec((NSA_QS, rep, TQ, n_cmp_pad), lambda b, g, i: (i, g, 0, 0)),
            pl.BlockSpec((tmain.shape[0], rep, TQ, TK), lambda b, g, i: (0, g, 0, 0),
                         pipeline_mode=pl.Buffered(1)),
            pl.BlockSpec((twin.shape[0], rep, TQ, TK), lambda b, g, i: (0, g, 0, 0),
                         pipeline_mode=pl.Buffered(1)),
        ],
        out_specs=pl.BlockSpec((1, tqb, _packed_width(rep)), lambda b, g, i: (b, i, g)),
        scratch_shapes=[pltpu.VMEM((LANES, tqb), F32)] + _flash_scratch(rep * tqb),
        compiler_params=_cparams(3),
        name="nsa_attention",
    )(r3(p["nsa_q"]), r3(p["nsa_gate"]), kc, vc, r3(p["nsa_ks"]), r3(p["nsa_vs"]), r3(p["nsa_kw"]),
      r3(p["nsa_vw"]), c2s.astype(MXU_DTYPE), tcmp, tmain, twin)


def _moba_kernel(q_ref, *refs, nb):
    hs = MOBA_HS
    k_refs, v_refs = refs[0:hs], refs[hs:2 * hs]
    km_ref, t_ref, o_ref, z_ref = refs[2 * hs:2 * hs + 4]
    flash_scratch = refs[2 * hs + 4:]
    step = pl.program_id(2)
    tqb = MOBA_QB * MOBA_BLOCK
    n_qs = tqb // TQ
    n_i = lax.broadcasted_iota(jnp.int32, (nb, tqb), 0)
    own = step * MOBA_QB + jnp.right_shift(lax.broadcasted_iota(jnp.int32, (nb, tqb), 1), int(math.log2(MOBA_BLOCK)))
    past = n_i < own
    q_sel = []
    for h in range(hs):
        q = q_ref[0, :, h * SLOT:(h + 1) * SLOT]
        km = km_ref[0, :, h * SLOT:(h + 1) * SLOT]
        km_hi = km.astype(MXU_DTYPE)
        rem = km - km_hi.astype(F32)
        km_mid = rem.astype(MXU_DTYPE)
        km_lo = (rem - km_mid.astype(F32)).astype(MXU_DTYPE)
        gate_t = _dot_nt(km_hi, q) + _dot_nt(km_mid, q) + _dot_nt(km_lo, q)
        score = jnp.where(past, gate_t, NEG)
        sel_t = ((_rank_before(score) < min(MOBA_TOPK, nb - 1)) & past) | (n_i == own)
        zt = _block_mask_lanes(z_ref.at[h], sel_t, tqb)
        q_sel.append((q.astype(F32) + zt).astype(MXU_DTYPE))
    q_tile0 = step * n_qs
    n_chunks = (q_tile0 + n_qs - 1) // KV_CHUNK + 1
    o = _flash_chunks(jnp.concatenate(q_sel, axis=0), [r.at[0] for r in k_refs], [r.at[0] for r in v_refs], t_ref,
                      q_tile0, hs, n_qs, n_chunks, flash_scratch, max_in_producer=False)
    o_ref[0] = _pack_heads([o[h * tqb:(h + 1) * tqb] for h in range(hs)]).astype(o_ref.dtype)


def _moba(p, kmean, tmain, batch, seq):
    nq = seq // TQ
    nb = seq // MOBA_BLOCK
    tqb = MOBA_QB * MOBA_BLOCK
    hs = MOBA_HS
    assert seq % tqb == 0 and nb <= HEAD_DIM and MOBA_BLOCK % TQ == 0 and nq % KV_CHUNK == 0
    assert TILE_PAD >= KV_CHUNK + tqb // TQ - 2 and MOBA_HEADS % hs == 0 and NSA_HEADS % hs == 0
    r3 = lambda a: a.reshape(batch, seq, a.shape[-1])
    kv_specs = [pl.BlockSpec((1, seq, SLOT), lambda b, g, i, h=h: (b, 0, g * hs + h)) for h in range(hs)]
    q_spec = pl.BlockSpec((1, tqb, hs * SLOT), lambda b, g, i: (b, i, g))
    return pl.pallas_call(
        functools.partial(_moba_kernel, nb=nb),
        out_shape=jax.ShapeDtypeStruct((batch, seq, MOBA_HEADS // hs * _packed_width(hs)), MXU_DTYPE),
        grid=(batch, MOBA_HEADS // hs, seq // tqb),
        in_specs=[q_spec] + kv_specs + kv_specs
                 + [pl.BlockSpec((1, nb, hs * SLOT), lambda b, g, i: (b, 0, g)),
                    pl.BlockSpec((tmain.shape[0], hs, TQ, TK), lambda b, g, i: (0, NSA_HEADS // hs + g, 0, 0))],
        out_specs=pl.BlockSpec((1, tqb, _packed_width(hs)), lambda b, g, i: (b, i, g)),
        scratch_shapes=[pltpu.VMEM((hs, LANES, tqb), F32)] + _flash_scratch(hs * tqb),
        compiler_params=_cparams(3),
        name="moba_attention",
    )(r3(p["moba_q"]), *([r3(p["moba_k"])] * hs), *([r3(p["moba_v"])] * hs),
      kmean.reshape(batch, nb, MOBA_HEADS * SLOT), tmain)


def _dilated_kernel(q_ref, k_ref, v_ref, t_ref, o_ref, lse_ref, *, n_tiles, n_res):
    lane = lax.broadcasted_iota(jnp.int32, (DIL_WB, LANES), 1)
    first = lane < HEAD_DIM
    n_kt = min(2, n_tiles)

    for res in range(n_res):
        lanes = slice(res * LANES, (res + 1) * LANES)

        def tile(n, carry, lanes=lanes):
            kv0 = jnp.maximum(n - 1, 0)
            q_rows = pl.ds(pl.multiple_of(n * DIL_WB, DIL_WB), DIL_WB)
            kv_rows = pl.ds(pl.multiple_of(kv0 * DIL_WB, DIL_WB), n_kt * DIL_WB)
            q = q_ref[0, q_rows, lanes]
            k = k_ref[0, kv_rows, lanes]
            v = v_ref[0, kv_rows, lanes]
            o_heads, lse_heads = [], []
            for h in range(DIL_HEADS_PER_GROUP):
                qh = jnp.where(first if h == 0 else ~first, q, jnp.zeros_like(q))
                bias = jnp.concatenate([t_ref[(n - kv0) - j + (0 if j == 0 else 3 * (1 - (n - kv0))), h]
                                        for j in range(n_kt)], axis=1)
                s = _dot_nt(qh, k) + bias
                m = jnp.max(s, axis=-1, keepdims=True)
                p = jnp.exp(s - m)
                l = jnp.maximum(jnp.sum(p, axis=-1, keepdims=True), 1e-30)
                o_heads.append(_dot(p.astype(MXU_DTYPE), v) / l)
                lse_heads.append(m + jnp.log(l))
            o_ref[0, q_rows, lanes] = jnp.where(first, o_heads[0], o_heads[1])
            lse_ref[0, q_rows, lanes] = jnp.where(first, lse_heads[0], lse_heads[1])
            return carry

        lax.fori_loop(0, n_tiles, tile, 0, unroll=min(DIL_UNROLL, n_tiles))


def _dilated(q, k, v, tdil, dilation, batch, seq):
    ln = seq // dilation
    assert ln % DIL_WB == 0
    view = lambda a: a.reshape(batch, ln, dilation * LANES)
    spec = pl.BlockSpec((1, ln, dilation * LANES), lambda b: (b, 0, 0))
    out = jax.ShapeDtypeStruct((batch, ln, dilation * LANES), F32)
    o, lse = pl.pallas_call(
        functools.partial(_dilated_kernel, n_tiles=ln // DIL_WB, n_res=dilation),
        out_shape=[out, out],
        grid=(batch,),
        in_specs=[spec, spec, spec, _const_spec(tdil.shape)],
        out_specs=[spec, spec],
        compiler_params=_cparams(1),
        name=f"dilated_attention_d{dilation}",
    )(view(q), view(k), view(v), tdil)
    rows = batch * ln
    return o.reshape(rows, dilation * LANES), lse.reshape(rows, dilation * LANES)


def _mixer_out_kernel(x_ref, g_ref, ya_ref, yb_ref, o0_ref, o1_ref, o2_ref, l0_ref, l1_ref, l2_ref,
                      wmg_ref, wa_ref, wb_ref, wc_ref, wo_ref, out_ref, *stage_refs):
    x = x_ref[...]
    d = x.shape[1]
    tm = x.shape[0]
    h = _rms(x, g_ref[...]).astype(MXU_DTYPE)

    def token_major(ref, stage_ref):
        il = ref.shape[1] // LANES
        if il == 1:
            return ref[...]
        for r in range(il):
            stage_ref[pl.ds(r, tm // il, stride=il), :] = ref[:, r * LANES:(r + 1) * LANES]
        return stage_ref[...]

    o0, o1, o2, l0, l1, l2 = [token_major(r, s) for r, s in
                              zip((o0_ref, o1_ref, o2_ref, l0_ref, l1_ref, l2_ref), stage_refs)]
    mx = jnp.maximum(jnp.maximum(l0, l1), l2)
    e0, e1, e2 = jnp.exp(l0 - mx), jnp.exp(l1 - mx), jnp.exp(l2 - mx)
    y_c = (e0 * o0 + e1 * o1 + e2 * o2) / (e0 + e1 + e2)
    merged = jax.nn.sigmoid(_dot(h, wmg_ref[:, 0:d])) * _dot(ya_ref[...], wa_ref[...])
    merged += jax.nn.sigmoid(_dot(h, wmg_ref[:, d:2 * d])) * _dot(yb_ref[...], wb_ref[...])
    merged += jax.nn.sigmoid(_dot(h, wmg_ref[:, 2 * d:3 * d])) * _dot(y_c.astype(MXU_DTYPE), wc_ref[...])
    out_ref[...] = x + _dot(merged.astype(MXU_DTYPE), wo_ref[...])


def _mixer_out(x2, gain, ya, yb, dil, wmg, wa, wb, wc, wo, tm=1024):
    t, d = x2.shape
    tm = min(tm, t)
    row = lambda w: pl.BlockSpec((tm, w), lambda i: (i, 0))
    (o0, l0), (o1, l1), (o2, l2) = dil
    dil_arrays = (o0, o1, o2, l0, l1, l2)
    inter = lambda a: pl.BlockSpec((tm // (a.shape[1] // LANES), a.shape[1]), lambda i: (i, 0))
    return pl.pallas_call(
        _mixer_out_kernel,
        out_shape=jax.ShapeDtypeStruct((t, d), F32),
        grid=(t // tm,),
        in_specs=[row(d), _const_spec((1, d)), row(ya.shape[1]), row(yb.shape[1])] + [inter(a) for a in dil_arrays]
                 + [_const_spec(w.shape) for w in (wmg, wa, wb, wc, wo)],
        out_specs=row(d),
        scratch_shapes=[pltpu.VMEM((tm, LANES), F32)] * len(dil_arrays),
        compiler_params=_cparams(1),
        name="mixer_out_proj",
    )(x2, gain.reshape(1, d), ya, yb, o0, o1, o2, l0, l1, l2, wmg, wa, wb, wc, wo)


def _packed_rows(w, n_groups, heads_per_group):
    d = w.shape[1]
    w = w.reshape(n_groups, heads_per_group * HEAD_DIM, d)
    pad = _packed_width(heads_per_group) - heads_per_group * HEAD_DIM
    return jnp.pad(w, ((0, 0), (0, pad), (0, 0))).reshape(n_groups * _packed_width(heads_per_group), d)


def _compress_weights(pe, w1, w2):
    g_n, hd, half = NSA_KV_HEADS, HEAD_DIM, NSA_CMP_STRIDE
    hid = w1.shape[1]
    w1r = w1.reshape(2, half, hd, hid)
    zeros = jnp.zeros_like(w1r)
    per_group = [jnp.concatenate([w1r if k == g else zeros for k in range(g_n)], axis=-1) for g in range(g_n)]
    w1p = jnp.stack(per_group, axis=2).reshape(2, half * g_n * hd, g_n * hid)
    pep = jnp.broadcast_to(pe.reshape(2, half, 1, hd), (2, half, g_n, hd)).reshape(2, half * g_n * hd)
    w2p = jnp.concatenate([w2, jnp.zeros_like(w2)], axis=1)
    return pep, w1p.astype(MXU_DTYPE), w2p.astype(MXU_DTYPE)


def _cmp_to_slc(seq, n_cmp_pad):
    n_cmp = (seq - NSA_CMP_BLOCK) // NSA_CMP_STRIDE + 1
    n_slc = seq // NSA_SEL_BLOCK
    c_start = np.arange(n_cmp_pad) * NSA_CMP_STRIDE
    s_start = np.arange(LANES) * NSA_SEL_BLOCK
    ov = ((c_start[:, None] < s_start[None, :] + NSA_SEL_BLOCK) & (c_start[:, None] + NSA_CMP_BLOCK > s_start[None, :])
          & (np.arange(n_cmp_pad)[:, None] < n_cmp) & (np.arange(LANES)[None, :] < n_slc))
    return jnp.asarray(ov, F32)


def kernel(x, rel_bias, ffn1_norm, ffn1_w_gate, ffn1_w_up, ffn1_w_down, mix_norm, w_in, nsa_pe_k, nsa_pe_v,
           nsa_phi_k1, nsa_phi_k2, nsa_phi_v1, nsa_phi_v2, w_up_a, w_up_b, w_up_c, w_o, ffn2_norm, ffn2_w_gate,
           ffn2_w_up, ffn2_w_down, final_norm):
    batch, seq, d = x.shape
    depth = w_in.shape[0]
    nq = seq // TQ
    assert seq % (TQ * DIL_PATTERNS[-1][1]) == 0 and all(w // dl == DIL_WB for w, dl in DIL_PATTERNS)
    bf = lambda a: a.astype(MXU_DTYPE)

    n_cmp = (seq - NSA_CMP_BLOCK) // NSA_CMP_STRIDE + 1
    n_cmp_pad = seq // NSA_CMP_STRIDE
    a_heads = list(range(NSA_HEADS))
    ab_heads = list(range(NSA_HEADS + MOBA_HEADS))
    assert TILE_PAD >= max(WIN_TILES, KV_CHUNK - 1)
    t_main = _bias_tiles(rel_bias, n_tiles=nq + TILE_PAD, heads=ab_heads, offset=-TILE_PAD * TQ, name="bias_causal")
    t_win = _bias_tiles(rel_bias, n_tiles=WIN_TILES + NSA_QS + TILE_PAD, heads=a_heads, offset=-TILE_PAD * TQ,
                        hi=NSA_WINDOW - 1, name="bias_window")
    t_cmp = _bias_tiles(rel_bias, n_tiles=nq, heads=a_heads, width=n_cmp_pad, col_mult=NSA_CMP_STRIDE,
                        offset=-(NSA_CMP_BLOCK - 1), n_cols=n_cmp, name="bias_compressed")
    t_dil = []
    for g, (_, dilation) in enumerate(DIL_PATTERNS):
        h0 = NSA_HEADS + MOBA_HEADS + g * DIL_HEADS_PER_GROUP
        t_dil.append(_bias_tiles(rel_bias, n_tiles=3, heads=[h0, h0 + 1], dist_mult=dilation, hi=DIL_WB,
                                 name=f"bias_dilated_{g}"))
    c2s = _cmp_to_slc(seq, n_cmp_pad)

    x2 = x.reshape(batch * seq, d)
    for l in range(depth):
        last = l == depth - 1
        x2 = _ffn(x2, ffn1_norm[l], bf(ffn1_w_gate[l]), bf(ffn1_w_up[l]), bf(ffn1_w_down[l]))

        w_attn, w_merge = _proj_weights(w_in[l])
        p = _proj(x2, mix_norm[l], w_attn, seq)
        pek, wk1, wk2 = _compress_weights(nsa_pe_k[l], nsa_phi_k1[l], nsa_phi_k2[l])
        pev, wv1, wv2 = _compress_weights(nsa_pe_v[l], nsa_phi_v1[l], nsa_phi_v2[l])
        kc, vc = _compress(p["nsa_kc"], p["nsa_vc"], pek, pev, wk1, wv1, wk2, wv2, batch, seq)
        ya = _nsa(p, kc, vc, c2s, t_cmp, t_main, t_win, batch, seq)
        yb = _moba(p, p["moba_kmean"], t_main, batch, seq)
        dil = [_dilated(p[f"dil_q{g}"], p[f"dil_k{g}"], p[f"dil_v{g}"], t_dil[g], dilation, batch, seq)
               for g, (_, dilation) in enumerate(DIL_PATTERNS)]
        x2 = _mixer_out(x2, mix_norm[l], ya.reshape(batch * seq, -1), yb.reshape(batch * seq, -1), dil,
                        w_merge, bf(_packed_rows(w_up_a[l], NSA_KV_HEADS, NSA_REP)),
                        bf(_packed_rows(w_up_b[l], MOBA_HEADS // MOBA_HS, MOBA_HS)), bf(w_up_c[l]), bf(w_o[l]))

        x2 = _ffn(x2, ffn2_norm[l], bf(ffn2_w_gate[l]), bf(ffn2_w_up[l]), bf(ffn2_w_down[l]),
                  final_gain=final_norm if last else None)
    return x2.reshape(batch, seq, d)
```

```python
import functools
import math

import jax
import jax.numpy as jnp
import numpy as np
from jax import lax
from jax.experimental import pallas as pl
from jax.experimental.pallas import tpu as pltpu

HEAD_DIM = 64
NSA_HEADS = 6
NSA_KV_HEADS = 2
NSA_REP = NSA_HEADS // NSA_KV_HEADS
NSA_CMP_BLOCK = 32
NSA_CMP_STRIDE = 16
NSA_SEL_BLOCK = 64
NSA_N_SEL = 16
NSA_WINDOW = 512
NSA_CMP_HIDDEN = 256
MOBA_HEADS = 4
MOBA_BLOCK = 256
MOBA_TOPK = 3
DIL_PATTERNS = ((128, 1), (512, 4), (2048, 16))
DIL_HEADS_PER_GROUP = 2
DIL_WB = 128
REL_BUCKETS = 32
REL_MAX_EXACT = 16
REL_MAX_DIST = 2048
NORM_EPS = 1e-6
FORCE_SCORE = 1e4

LANES = 128
MXU_COLS = 256
SLOT = LANES
TQ = 128
TK = 128
KV_CHUNK = 4
WIN_TILES = NSA_WINDOW // TK
TILE_PAD = 10
NSA_QS = 2
MOBA_QB = 2
MOBA_HS = 2
DIL_UNROLL = 8
NEG = -1e30
M_INIT = -1e29
MXU_DTYPE = jnp.bfloat16
VMEM_LIMIT = 56 * 1024 * 1024
F32 = jnp.float32


def _cparams(n_grid, vmem=VMEM_LIMIT):
    return pltpu.CompilerParams(dimension_semantics=("arbitrary",) * n_grid, vmem_limit_bytes=vmem)


def _const_spec(shape):
    nd = len(shape)
    return pl.BlockSpec(shape, lambda *_: (0,) * nd, pipeline_mode=pl.Buffered(1))


def _dot(a, b):
    return jnp.dot(a, b, preferred_element_type=F32)


def _dot_nt(a, b):
    return lax.dot_general(a, b, (((1,), (1,)), ((), ())), preferred_element_type=F32)


def _rms(x, gain):
    return x * lax.rsqrt(jnp.mean(x * x, axis=-1, keepdims=True) + NORM_EPS) * gain


def _ffn_kernel(x_ref, g_ref, wg_ref, wu_ref, wd_ref, *rest, ff_chunk, final_norm):
    if final_norm:
        fg_ref, o_ref, acc_ref = rest
    else:
        o_ref, acc_ref = rest
    x = x_ref[...]
    h = _rms(x, g_ref[...]).astype(MXU_DTYPE)
    d_ff = wg_ref.shape[1]
    for c in range(d_ff // ff_chunk):
        sl = slice(c * ff_chunk, (c + 1) * ff_chunk)
        a = _dot(h, wg_ref[:, sl])
        u = _dot(h, wu_ref[:, sl])
        z = (a * jax.nn.sigmoid(a) * u).astype(MXU_DTYPE)
        part = _dot(z, wd_ref[sl, :])
        if c == 0:
            acc_ref[...] = part
        else:
            acc_ref[...] += part
    y = x + 0.5 * acc_ref[...]
    if final_norm:
        y = _rms(y, fg_ref[...])
    o_ref[...] = y


def _ffn(x2, gain, wg, wu, wd, final_gain=None, tm=1024):
    t, d = x2.shape
    d_ff = wg.shape[1]
    ff_chunk = MXU_COLS if d_ff % MXU_COLS == 0 else d_ff
    tm = min(tm, t)
    final = final_gain is not None
    in_specs = [pl.BlockSpec((tm, d), lambda i: (i, 0)), _const_spec((1, d)),
                _const_spec((d, d_ff)), _const_spec((d, d_ff)), _const_spec((d_ff, d))]
    args = [x2, gain.reshape(1, d), wg, wu, wd]
    if final:
        in_specs.append(_const_spec((1, d)))
        args.append(final_gain.reshape(1, d))
    return pl.pallas_call(
        functools.partial(_ffn_kernel, ff_chunk=ff_chunk, final_norm=final),
        out_shape=jax.ShapeDtypeStruct((t, d), F32),
        grid=(t // tm,),
        in_specs=in_specs,
        out_specs=pl.BlockSpec((tm, d), lambda i: (i, 0)),
        scratch_shapes=[pltpu.VMEM((tm, d), F32)],
        compiler_params=_cparams(1),
        name="ffn_swiglu",
    )(*args)


def _bias_tile_kernel(tbl_ref, o_ref, *, heads, width, row_step, col_mult, offset, dist_mult, lo, hi, n_cols):
    d = pl.program_id(0)
    raw_hi = row_step * d + (TQ - 1) + offset
    raw_lo = row_step * d - col_mult * (width - 1) + offset
    all_masked = (raw_hi < lo) | (raw_lo > hi)
    all_far = (raw_lo >= lo) & (raw_hi <= hi) & (raw_lo * dist_mult >= REL_MAX_DIST) & (n_cols == width)

    @pl.when(all_masked)
    def _():
        o_ref[...] = jnp.full(o_ref.shape, NEG, F32)

    @pl.when(all_far)
    def _():
        for hh, head in enumerate(heads):
            o_ref[0, hh] = jnp.full((TQ, width), tbl_ref[REL_BUCKETS - 1, head], F32)

    @pl.when(jnp.logical_not(all_masked | all_far))
    def _():
        i = lax.broadcasted_iota(jnp.int32, (TQ, width), 0)
        j = lax.broadcasted_iota(jnp.int32, (TQ, width), 1)
        raw = row_step * d + i - col_mult * j + offset
        valid = (raw >= lo) & (raw <= hi) & (j < n_cols)
        n = jnp.maximum(raw * dist_mult, 0)
        nf = jnp.maximum(n, REL_MAX_EXACT).astype(F32)
        large = REL_MAX_EXACT + (jnp.log(nf / REL_MAX_EXACT) / math.log(REL_MAX_DIST / REL_MAX_EXACT)
                                 * (REL_BUCKETS - REL_MAX_EXACT)).astype(jnp.int32)
        large = jnp.minimum(large, REL_BUCKETS - 1)
        bucket = jnp.where(n < REL_MAX_EXACT, n, large)
        for hh, head in enumerate(heads):
            val = jnp.zeros((TQ, width), F32)
            for k in range(REL_BUCKETS):
                val = jnp.where(bucket == k, tbl_ref[k, head], val)
            o_ref[0, hh] = jnp.where(valid, val, NEG)


def _bias_tiles(rel_bias, *, n_tiles, heads, width=TK, row_step=TQ, col_mult=1, offset=0, dist_mult=1,
                lo=0, hi=2 ** 30, n_cols=None, name):
    n_cols = width if n_cols is None else n_cols
    nh = len(heads)
    return pl.pallas_call(
        functools.partial(_bias_tile_kernel, heads=tuple(heads), width=width, row_step=row_step,
                          col_mult=col_mult, offset=offset, dist_mult=dist_mult, lo=lo, hi=hi, n_cols=n_cols),
        out_shape=jax.ShapeDtypeStruct((n_tiles, nh, TQ, width), F32),
        grid=(n_tiles,),
        in_specs=[pl.BlockSpec(memory_space=pltpu.SMEM)],
        out_specs=pl.BlockSpec((1, nh, TQ, width), lambda d: (d, 0, 0, 0)),
        compiler_params=_cparams(1),
        name=name,
    )(rel_bias)


_DILATIONS = tuple(d for _, d in DIL_PATTERNS)
_PROJ_SEGS = (
    ("nsa_q", NSA_HEADS * SLOT, MXU_DTYPE, 1, NSA_HEADS),
    ("nsa_kc", LANES, F32, NSA_CMP_STRIDE, 0),
    ("nsa_vc", LANES, F32, NSA_CMP_STRIDE, 0),
    ("nsa_ks", NSA_KV_HEADS * SLOT, MXU_DTYPE, 1, NSA_KV_HEADS),
    ("nsa_vs", NSA_KV_HEADS * SLOT, MXU_DTYPE, 1, NSA_KV_HEADS),
    ("nsa_kw", NSA_KV_HEADS * SLOT, MXU_DTYPE, 1, NSA_KV_HEADS),
    ("nsa_vw", NSA_KV_HEADS * SLOT, MXU_DTYPE, 1, NSA_KV_HEADS),
    ("nsa_gate", NSA_KV_HEADS * SLOT, F32, 1, NSA_KV_HEADS),
    ("moba_q", MOBA_HEADS * SLOT, MXU_DTYPE, 1, MOBA_HEADS),
    ("moba_k", MOBA_HEADS * SLOT, MXU_DTYPE, 1, MOBA_HEADS),
    ("moba_v", MOBA_HEADS * SLOT, MXU_DTYPE, 1, MOBA_HEADS),
) + tuple((f"dil_{n}{g}", LANES, MXU_DTYPE, d, 0) for n in "qkv" for g, d in enumerate(_DILATIONS))
_PROJ_WCOLS = [s[4] * HEAD_DIM if s[4] else s[1] for s in _PROJ_SEGS]
_PROJ_OFFS = np.concatenate([[0], np.cumsum(_PROJ_WCOLS)])
_PROJ_COLS = int(_PROJ_OFFS[-1])


def _proj_weights(w_l):
    hd = HEAD_DIM
    d_model = w_l.shape[0]
    kvw = NSA_KV_HEADS * hd
    widths = (("nsa_q", NSA_HEADS * hd), ("nsa_k_cmp", kvw), ("nsa_v_cmp", kvw), ("nsa_k_sel", kvw),
              ("nsa_v_sel", kvw), ("nsa_k_win", kvw), ("nsa_v_win", kvw), ("nsa_gate", NSA_HEADS * 3),
              ("moba_q", MOBA_HEADS * hd), ("moba_k", MOBA_HEADS * hd), ("moba_v", MOBA_HEADS * hd),
              ("dil_q", 6 * hd), ("dil_k", 6 * hd), ("dil_v", 6 * hd))
    cols, o = {}, 0
    for name, w in widths:
        cols[name] = w_l[:, o:o + w]
        o += w
    qk_scale = hd ** -0.5

    def slots(w, n, real):
        w = w.reshape(d_model, n, real)
        return jnp.pad(w, ((0, 0), (0, 0), (0, hd - real))).reshape(d_model, n * hd)

    parts = {
        "nsa_q": slots(cols["nsa_q"] * qk_scale, NSA_HEADS, hd),
        "nsa_kc": cols["nsa_k_cmp"], "nsa_vc": cols["nsa_v_cmp"],
        "nsa_ks": slots(cols["nsa_k_sel"], NSA_KV_HEADS, hd), "nsa_vs": slots(cols["nsa_v_sel"], NSA_KV_HEADS, hd),
        "nsa_kw": slots(cols["nsa_k_win"], NSA_KV_HEADS, hd), "nsa_vw": slots(cols["nsa_v_win"], NSA_KV_HEADS, hd),
        "nsa_gate": slots(cols["nsa_gate"], NSA_KV_HEADS, NSA_REP * 3),
        "moba_q": slots(cols["moba_q"] * qk_scale, MOBA_HEADS, hd),
        "moba_k": slots(cols["moba_k"], MOBA_HEADS, hd), "moba_v": slots(cols["moba_v"], MOBA_HEADS, hd),
    }
    for g in range(len(DIL_PATTERNS)):
        sl = slice(g * 2 * hd, (g + 1) * 2 * hd)
        parts[f"dil_q{g}"] = cols["dil_q"][:, sl] * qk_scale
        parts[f"dil_k{g}"] = cols["dil_k"][:, sl]
        parts[f"dil_v{g}"] = cols["dil_v"][:, sl]
    w_attn = jnp.concatenate([parts[seg[0]] for seg in _PROJ_SEGS], axis=1)
    return w_attn.astype(MXU_DTYPE), w_l[:, o:].astype(MXU_DTYPE)


def _proj_kernel(x_ref, g_ref, w_ref, *refs, seq, tm):
    names = [s[0] for s in _PROJ_SEGS] + ["moba_kmean"]
    outs = dict(zip(names, refs[:len(names)]))
    stage_ref = refs[len(names)]
    h = _rms(x_ref[...], g_ref[...]).astype(MXU_DTYPE)
    pos0 = (pl.program_id(0) * tm) % seq
    lower = lax.broadcasted_iota(jnp.int32, (tm, LANES), 1) < HEAD_DIM
    group_y, group_c0 = None, 0
    for si, (name, width, dtype, inter, n_slots) in enumerate(_PROJ_SEGS):
        c0 = int(_PROJ_OFFS[si])
        if group_y is None or c0 >= group_c0 + group_y.shape[1]:
            c1 = next((int(o) for o in _PROJ_OFFS[si + 1:] if (int(o) - c0) % MXU_COLS == 0), _PROJ_COLS)
            group_y, group_c0 = _dot(h, w_ref[:, c0:c1]), c0
        y = group_y[:, c0 - group_c0:c0 - group_c0 + _PROJ_WCOLS[si]]
        if n_slots:
            pieces = []
            for pair in range(n_slots // 2):
                v = y[:, pair * LANES:(pair + 1) * LANES]
                pieces += [jnp.where(lower, v, 0.0), jnp.where(lower, pltpu.roll(v, HEAD_DIM, 1), 0.0)]
            y = jnp.concatenate(pieces, axis=1)
        if inter > 1:
            stage_ref[...] = y
            for r in range(inter):
                outs[name][:, r * LANES:(r + 1) * LANES] = stage_ref[pl.ds(r, tm // inter, stride=inter), :].astype(dtype)
            continue
        if name == "moba_k":
            nblk = tm // MOBA_BLOCK
            outs["moba_kmean"][0] = jnp.mean(y.reshape(nblk, MOBA_BLOCK, width), axis=1)
        if name in ("nsa_ks", "moba_k"):
            blk = NSA_SEL_BLOCK if name == "nsa_ks" else MOBA_BLOCK
            row = lax.broadcasted_iota(jnp.int32, (tm, width), 0)
            lane = lax.broadcasted_iota(jnp.int32, (tm, width), 1) & (SLOT - 1)
            hit = (lane - HEAD_DIM) == jnp.right_shift(pos0 + row, int(math.log2(blk)))
            y = jnp.where(hit, 1.0, y)
        if name in ("nsa_vs", "nsa_vw", "moba_v"):
            lane = lax.broadcasted_iota(jnp.int32, (tm, width), 1) & (SLOT - 1)
            y = jnp.where(lane == HEAD_DIM, 1.0, y)
        if name == "nsa_gate":
            y = jax.nn.sigmoid(y)
        outs[name][...] = y.astype(dtype)


def _proj(x2, gain, w_attn, seq, tm=1024):
    t, d = x2.shape
    tm = min(tm, seq)
    assert tm % MOBA_BLOCK == 0 and seq % tm == 0
    nblk = tm // MOBA_BLOCK
    out_shape = [jax.ShapeDtypeStruct((t // il, w * il), dt) for _, w, dt, il, _ in _PROJ_SEGS]
    out_specs = [pl.BlockSpec((tm // il, w * il), lambda i: (i, 0)) for _, w, _, il, _ in _PROJ_SEGS]
    kw = MOBA_HEADS * SLOT
    out_shape.append(jax.ShapeDtypeStruct((t // tm, nblk, kw), F32))
    out_specs.append(pl.BlockSpec((1, nblk, kw), lambda i: (i, 0, 0)))
    outs = pl.pallas_call(
        functools.partial(_proj_kernel, seq=seq, tm=tm),
        out_shape=out_shape,
        grid=(t // tm,),
        in_specs=[pl.BlockSpec((tm, d), lambda i: (i, 0)), _const_spec((1, d)), _const_spec((d, _PROJ_COLS))],
        out_specs=out_specs,
        scratch_shapes=[pltpu.VMEM((tm, LANES), F32)],
        compiler_params=_cparams(1),
        name="mixer_in_proj",
    )(x2, gain.reshape(1, d), w_attn)
    res = {s[0]: o for s, o in zip(_PROJ_SEGS, outs)}
    res["moba_kmean"] = outs[-1]
    return res


def _gelu_tanh(x):
    return 0.5 * x * (1.0 + jnp.tanh(math.sqrt(2.0 / math.pi) * (x + 0.044715 * (x * x * x))))


def _compress_kernel(k_ref, v_ref, pek_ref, pev_ref, wk1_ref, wv1_ref, wk2_ref, wv2_ref, kc_ref, vc_ref):
    nr = k_ref.shape[1]
    hid_w = NSA_CMP_HIDDEN
    for x_ref, pe_ref, w1_ref, w2_ref, o_ref in ((k_ref, pek_ref, wk1_ref, wk2_ref, kc_ref),
                                                 (v_ref, pev_ref, wv1_ref, wv2_ref, vc_ref)):
        r = x_ref[0]
        lo = _dot((r + pe_ref[0:1, :]).astype(MXU_DTYPE), w1_ref[0])
        hi = _dot((r + pe_ref[1:2, :]).astype(MXU_DTYPE), w1_ref[1])
        hid = lo + pltpu.roll(hi, nr - 1, 0)
        act = _gelu_tanh(hid).astype(MXU_DTYPE)
        for g in range(NSA_KV_HEADS):
            o_ref[0, g] = _dot(act[:, g * hid_w:(g + 1) * hid_w], w2_ref[...]).astype(o_ref.dtype)


def _compress(kc_in, vc_in, pek, pev, wk1, wv1, wk2, wv2, batch, seq):
    nr = seq // NSA_CMP_STRIDE
    rw = NSA_CMP_STRIDE * LANES
    kin = kc_in.reshape(batch, nr, rw)
    vin = vc_in.reshape(batch, nr, rw)
    hw = NSA_KV_HEADS * NSA_CMP_HIDDEN
    out = jax.ShapeDtypeStruct((batch, NSA_KV_HEADS, nr, SLOT), MXU_DTYPE)
    in_blk = pl.BlockSpec((1, nr, rw), lambda b: (b, 0, 0))
    out_blk = pl.BlockSpec((1, NSA_KV_HEADS, nr, SLOT), lambda b: (b, 0, 0, 0))
    return pl.pallas_call(
        _compress_kernel,
        out_shape=[out, out],
        grid=(batch,),
        in_specs=[in_blk, in_blk, _const_spec((2, rw)), _const_spec((2, rw)),
                  _const_spec((2, rw, hw)), _const_spec((2, rw, hw)),
                  _const_spec((NSA_CMP_HIDDEN, SLOT)), _const_spec((NSA_CMP_HIDDEN, SLOT))],
        out_specs=[out_blk, out_blk],
        compiler_params=_cparams(1),
        name="nsa_compress",
    )(kin, vin, pek, pev, wk1, wv1, wk2, wv2)


def _rank_before(score):
    n, tq = score.shape
    sub_rows = 8
    assert n % sub_rows == 0
    bits = pltpu.bitcast(score, jnp.int32)
    key = bits ^ (jnp.right_shift(bits, 31) & 0x7FFFFFFF)
    keys = [key[b * sub_rows:(b + 1) * sub_rows] for b in range(n // sub_rows)]
    sub = lax.broadcasted_iota(jnp.int32, (sub_rows, tq), 0)
    adj = [k - 1 for k in keys]
    rank = [jnp.zeros((sub_rows, tq), jnp.int32) for _ in keys]
    for j in range(n):
        b, r = divmod(j, sub_rows)
        adj[b] = adj[b] + jnp.where(sub == r, 1, 0)
        row = keys[b][r:r + 1, :]
        rank = [rk + jnp.where(row > a, 1, 0) for rk, a in zip(rank, adj)]
    return jnp.concatenate(rank, axis=0)


def _bias_block(t_ref, q_tile0, k_tile0, n_stack, n_qs, n_kt):
    rows = []
    for r in range(n_stack):
        for a in range(n_qs):
            base = q_tile0 + a - k_tile0 + TILE_PAD
            rows.append(jnp.concatenate([t_ref[base - j, r] for j in range(n_kt)], axis=1))
    return jnp.concatenate(rows, axis=0)


def _flash_chunks(q, k_ref, v_ref, t_ref, q_tile0, n_stack, n_qs, n_chunks, scratch):
    q_s, s_a, s_b, p_a, p_b, m_s, a_s, acc_s = scratch
    rows = q.shape[0]
    kw = KV_CHUNK * TK
    stat = (rows, LANES)

    def chunk_rows(c):
        return pl.ds(pl.multiple_of(c * kw, kw), kw)

    k_refs = k_ref if isinstance(k_ref, (list, tuple)) else [k_ref]
    v_refs = v_ref if isinstance(v_ref, (list, tuple)) else [v_ref]
    grp = rows // len(k_refs)
    groups = [slice(i * grp, (i + 1) * grp) for i in range(len(k_refs))]

    def logits_into(c, s_ref):
        bias = _bias_block(t_ref, q_tile0, c * KV_CHUNK, n_stack, n_qs, KV_CHUNK)
        for rows_g, kg in zip(groups, k_refs):
            s_ref[rows_g, :] = _dot_nt(q_s[rows_g, :], kg[chunk_rows(c), :]) + bias[rows_g]

    def row_max(s):
        tiles = [s[:, j * TK:(j + 1) * TK] for j in range(KV_CHUNK)]
        return jnp.broadcast_to(jnp.max(functools.reduce(jnp.maximum, tiles), axis=-1, keepdims=True),
                                (s.shape[0], LANES))

    def pv(p_ref, c):
        for rows_g, vg in zip(groups, v_refs):
            acc_s[rows_g, :] = a_s[rows_g, :] * acc_s[rows_g, :] + _dot(p_ref[rows_g, :], vg[chunk_rows(c), :])

    def step(c, s_cur, s_nxt, p_cur, p_prev, prefetch=True):
        pv(p_prev, jnp.maximum(c - 1, 0))
        if prefetch:
            logits_into(jnp.minimum(c + 1, n_chunks - 1), s_nxt)
        m_prev = m_s[...]
        m_new = jnp.maximum(m_prev, row_max(s_cur[...]))
        a_s[...] = jnp.exp(m_prev - m_new)
        m_s[...] = m_new
        for j in range(KV_CHUNK):
            p_cur[:, j * TK:(j + 1) * TK] = jnp.exp(s_cur[:, j * TK:(j + 1) * TK] - m_new).astype(MXU_DTYPE)

    q_s[...] = q
    m_s[...] = jnp.full(stat, M_INIT, F32)
    a_s[...] = jnp.ones(stat, F32)
    acc_s[...] = jnp.zeros(stat, F32)
    p_b[...] = jnp.zeros(p_b.shape, MXU_DTYPE)
    logits_into(0, s_a)

    def body(i, carry):
        step(2 * i, s_a, s_b, p_a, p_b)
        step(2 * i + 1, s_b, s_a, p_b, p_a)
        return carry

    lax.fori_loop(0, n_chunks // 2, body, 0)
    last = n_chunks - 1

    @pl.when(last % 2 == 0)
    def _():
        step(last, s_a, s_b, p_a, p_b, prefetch=False)
        pv(p_a, last)

    @pl.when(last % 2 == 1)
    def _():
        pv(p_b, last)
    return _normalize(acc_s[...])


def _flash_scratch(rows):
    kw = KV_CHUNK * TK
    return ([pltpu.VMEM((rows, LANES), MXU_DTYPE)]
            + [pltpu.VMEM((rows, kw), F32)] * 2
            + [pltpu.VMEM((rows, kw), MXU_DTYPE)] * 2 + [pltpu.VMEM((rows, LANES), F32)] * 3)


def _normalize(acc):
    return acc / jnp.maximum(acc[:, HEAD_DIM:HEAD_DIM + 1], 1e-30)


def _attend_once(q, k, v, bias):
    s = _dot_nt(q, k) + bias
    m = jnp.maximum(jnp.max(s, axis=-1, keepdims=True), M_INIT)
    p = jnp.exp(s - m)
    return _normalize(_dot(p.astype(MXU_DTYPE), v))


def _packed_width(n_heads):
    return LANES * ((n_heads + 1) // 2)


def _pack_heads(heads):
    rows = heads[0].shape[0]
    lower = lax.broadcasted_iota(jnp.int32, (rows, LANES), 1) < HEAD_DIM
    tiles = []
    for i in range(0, len(heads), 2):
        upper = pltpu.roll(heads[i + 1], HEAD_DIM, 1) if i + 1 < len(heads) else jnp.zeros_like(heads[i])
        tiles.append(jnp.where(lower, heads[i], upper))
    return jnp.concatenate(tiles, axis=1)


def _block_mask_lanes(z_ref, sel_t, tq):
    n = sel_t.shape[0]
    z_ref[...] = jnp.zeros(z_ref.shape, F32)
    z_ref[HEAD_DIM:HEAD_DIM + n, :] = jnp.where(sel_t, 0.0, NEG)
    return z_ref[...].T


def _nsa_kernel(q_ref, gate_ref, kc_ref, vc_ref, ks_ref, vs_ref, kw_ref, vw_ref, c2s_ref,
                tcmp_ref, tsel_ref, twin_ref, o_ref, z_ref, *flash_scratch, n_slc):
    step = pl.program_id(2)
    qt0 = step * NSA_QS
    rep, nqs = NSA_REP, NSA_QS
    tqb = nqs * TQ
    q_all = q_ref[0]
    q = jnp.concatenate([q_all[a * TQ:(a + 1) * TQ, r * SLOT:(r + 1) * SLOT]
                         for r in range(rep) for a in range(nqs)], axis=0)

    n_wt = WIN_TILES + nqs
    w0 = jnp.maximum(qt0 - WIN_TILES, 0)
    w_rows = pl.ds(pl.multiple_of(w0 * TK, TK), n_wt * TK)
    o_w = _attend_once(q, kw_ref[0, w_rows, :], vw_ref[0, w_rows, :],
                       _bias_block(twin_ref, qt0, w0, rep, nqs, n_wt))

    bias_c = jnp.concatenate([tcmp_ref[a, r] for r in range(rep) for a in range(nqs)], axis=0)
    sc = _dot_nt(q, kc_ref[0, 0]) + bias_c
    mc = jnp.maximum(jnp.max(sc, axis=-1, keepdims=True), M_INIT)
    pc = jnp.exp(sc - mc)
    pc = pc / jnp.maximum(jnp.sum(pc, axis=-1, keepdims=True), 1e-30)
    o_c = _dot(pc.astype(MXU_DTYPE), vc_ref[0, 0])

    p_sum = pc[0:tqb]
    for r in range(1, rep):
        p_sum = p_sum + pc[r * tqb:(r + 1) * tqb]
    p_hi = p_sum.astype(MXU_DTYPE)
    p_lo = (p_sum - p_hi.astype(F32)).astype(MXU_DTYPE)
    imp = _dot(p_hi, c2s_ref[...]) + _dot(p_lo, c2s_ref[...])
    imp_t = imp.T[0:n_slc]
    blk = lax.broadcasted_iota(jnp.int32, (n_slc, tqb), 0)
    t_pos = qt0 * TQ + lax.broadcasted_iota(jnp.int32, (n_slc, tqb), 1)
    cur = jnp.right_shift(t_pos, int(math.log2(NSA_SEL_BLOCK)))
    forced = (blk == 0) | (blk == cur) | (blk == cur - 1)
    score = jnp.where(blk > cur, NEG, jnp.where(forced, FORCE_SCORE, imp_t))
    sel_t = _rank_before(score) < min(NSA_N_SEL, n_slc)
    zt = _block_mask_lanes(z_ref, sel_t, tqb)
    q_sel = (q.astype(F32) + jnp.concatenate([zt] * rep, axis=0)).astype(MXU_DTYPE)

    n_chunks = (qt0 + nqs - 1) // KV_CHUNK + 1
    o_s = _flash_chunks(q_sel, ks_ref.at[0], vs_ref.at[0], tsel_ref, qt0, rep, nqs, n_chunks, flash_scratch)

    gates = gate_ref[0]
    heads = []
    for r in range(rep):
        sl = slice(r * tqb, (r + 1) * tqb)
        heads.append(gates[:, 3 * r:3 * r + 1] * o_c[sl] + gates[:, 3 * r + 1:3 * r + 2] * o_s[sl]
                     + gates[:, 3 * r + 2:3 * r + 3] * o_w[sl])
    o_ref[0] = _pack_heads(heads).astype(o_ref.dtype)


def _nsa(p, kc, vc, c2s, tcmp, tmain, twin, batch, seq):
    nq = seq // TQ
    n_slc = seq // NSA_SEL_BLOCK
    tqb = NSA_QS * TQ
    assert n_slc <= HEAD_DIM and nq % KV_CHUNK == 0 and nq % NSA_QS == 0 and nq >= WIN_TILES + NSA_QS
    assert TILE_PAD >= KV_CHUNK + NSA_QS - 2
    g_n, rep = NSA_KV_HEADS, NSA_REP
    n_cmp_pad = kc.shape[2]
    r3 = lambda a: a.reshape(batch, seq, a.shape[-1])
    kv_spec = pl.BlockSpec((1, seq, SLOT), lambda b, g, i: (b, 0, g))
    cmp_spec = pl.BlockSpec((1, 1, n_cmp_pad, SLOT), lambda b, g, i: (b, g, 0, 0))
    return pl.pallas_call(
        functools.partial(_nsa_kernel, n_slc=n_slc),
        out_shape=jax.ShapeDtypeStruct((batch, seq, g_n * _packed_width(rep)), MXU_DTYPE),
        grid=(batch, g_n, nq // NSA_QS),
        in_specs=[
            pl.BlockSpec((1, tqb, rep * SLOT), lambda b, g, i: (b, i, g)),
            pl.BlockSpec((1, tqb, SLOT), lambda b, g, i: (b, i, g)),
            cmp_spec, cmp_spec, kv_spec, kv_spec, kv_spec, kv_spec,
            _const_spec(c2s.shape),
            pl.BlockSpec((NSA_QS, rep, TQ, n_cmp_pad), lambda b, g, i: (i, g, 0, 0)),
            pl.BlockSpec((tmain.shape[0], rep, TQ, TK), lambda b, g, i: (0, g, 0, 0),
                         pipeline_mode=pl.Buffered(1)),
            pl.BlockSpec((twin.shape[0], rep, TQ, TK), lambda b, g, i: (0, g, 0, 0),
                         pipeline_mode=pl.Buffered(1)),
        ],
        out_specs=pl.BlockSpec((1, tqb, _packed_width(rep)), lambda b, g, i: (b, i, g)),
        scratch_shapes=[pltpu.VMEM((LANES, tqb), F32)] + _flash_scratch(rep * tqb),
        compiler_params=_cparams(3),
        name="nsa_attention",
    )(r3(p["nsa_q"]), r3(p["nsa_gate"]), kc, vc, r3(p["nsa_ks"]), r3(p["nsa_vs"]), r3(p["nsa_kw"]),
      r3(p["nsa_vw"]), c2s.astype(MXU_DTYPE), tcmp, tmain, twin)


def _moba_kernel(q_ref, *refs, nb):
    hs = MOBA_HS
    k_refs, v_refs = refs[0:hs], refs[hs:2 * hs]
    km_ref, t_ref, o_ref, z_ref = refs[2 * hs:2 * hs + 4]
    flash_scratch = refs[2 * hs + 4:]
    step = pl.program_id(2)
    tqb = MOBA_QB * MOBA_BLOCK
    n_qs = tqb // TQ
    n_i = lax.broadcasted_iota(jnp.int32, (nb, tqb), 0)
    own = step * MOBA_QB + jnp.right_shift(lax.broadcasted_iota(jnp.int32, (nb, tqb), 1), int(math.log2(MOBA_BLOCK)))
    past = n_i < own
    q_sel = []
    for h in range(hs):
        q = q_ref[0, :, h * SLOT:(h + 1) * SLOT]
        km = km_ref[0, :, h * SLOT:(h + 1) * SLOT]
        km_hi = km.astype(MXU_DTYPE)
        rem = km - km_hi.astype(F32)
        km_mid = rem.astype(MXU_DTYPE)
        km_lo = (rem - km_mid.astype(F32)).astype(MXU_DTYPE)
        gate_t = _dot_nt(km_hi, q) + _dot_nt(km_mid, q) + _dot_nt(km_lo, q)
        score = jnp.where(past, gate_t, NEG)
        sel_t = ((_rank_before(score) < min(MOBA_TOPK, nb - 1)) & past) | (n_i == own)
        zt = _block_mask_lanes(z_ref.at[h], sel_t, tqb)
        q_sel.append((q.astype(F32) + zt).astype(MXU_DTYPE))
    q_tile0 = step * n_qs
    n_chunks = (q_tile0 + n_qs - 1) // KV_CHUNK + 1
    o = _flash_chunks(jnp.concatenate(q_sel, axis=0), [r.at[0] for r in k_refs], [r.at[0] for r in v_refs], t_ref,
                      q_tile0, hs, n_qs, n_chunks, flash_scratch)
    o_ref[0] = _pack_heads([o[h * tqb:(h + 1) * tqb] for h in range(hs)]).astype(o_ref.dtype)


def _moba(p, kmean, tmain, batch, seq):
    nq = seq // TQ
    nb = seq // MOBA_BLOCK
    tqb = MOBA_QB * MOBA_BLOCK
    hs = MOBA_HS
    assert seq % tqb == 0 and nb <= HEAD_DIM and MOBA_BLOCK % TQ == 0 and nq % KV_CHUNK == 0
    assert TILE_PAD >= KV_CHUNK + tqb // TQ - 2 and MOBA_HEADS % hs == 0 and NSA_HEADS % hs == 0
    r3 = lambda a: a.reshape(batch, seq, a.shape[-1])
    kv_specs = [pl.BlockSpec((1, seq, SLOT), lambda b, g, i, h=h: (b, 0, g * hs + h)) for h in range(hs)]
    q_spec = pl.BlockSpec((1, tqb, hs * SLOT), lambda b, g, i: (b, i, g))
    return pl.pallas_call(
        functools.partial(_moba_kernel, nb=nb),
        out_shape=jax.ShapeDtypeStruct((batch, seq, MOBA_HEADS // hs * _packed_width(hs)), MXU_DTYPE),
        grid=(batch, MOBA_HEADS // hs, seq // tqb),
        in_specs=[q_spec] + kv_specs + kv_specs
                 + [pl.BlockSpec((1, nb, hs * SLOT), lambda b, g, i: (b, 0, g)),
                    pl.BlockSpec((tmain.shape[0], hs, TQ, TK), lambda b, g, i: (0, NSA_HEADS // hs + g, 0, 0))],
        out_specs=pl.BlockSpec((1, tqb, _packed_width(hs)), lambda b, g, i: (b, i, g)),
        scratch_shapes=[pltpu.VMEM((hs, LANES, tqb), F32)] + _flash_scratch(hs * tqb),
        compiler_params=_cparams(3),
        name="moba_attention",
    )(r3(p["moba_q"]), *([r3(p["moba_k"])] * hs), *([r3(p["moba_v"])] * hs),
      kmean.reshape(batch, nb, MOBA_HEADS * SLOT), tmain)


def _dilated_kernel(q_ref, k_ref, v_ref, t_ref, o_ref, lse_ref, *, n_tiles, n_res):
    lane = lax.broadcasted_iota(jnp.int32, (DIL_WB, LANES), 1)
    first = lane < HEAD_DIM
    n_kt = min(2, n_tiles)

    for res in range(n_res):
        lanes = slice(res * LANES, (res + 1) * LANES)

        def tile(n, carry, lanes=lanes):
            kv0 = jnp.maximum(n - 1, 0)
            q_rows = pl.ds(pl.multiple_of(n * DIL_WB, DIL_WB), DIL_WB)
            kv_rows = pl.ds(pl.multiple_of(kv0 * DIL_WB, DIL_WB), n_kt * DIL_WB)
            q = q_ref[0, q_rows, lanes]
            k = k_ref[0, kv_rows, lanes]
            v = v_ref[0, kv_rows, lanes]
            o_heads, lse_heads = [], []
            for h in range(DIL_HEADS_PER_GROUP):
                qh = jnp.where(first if h == 0 else ~first, q, jnp.zeros_like(q))
                bias = jnp.concatenate([t_ref[(n - kv0) - j + (0 if j == 0 else 3 * (1 - (n - kv0))), h]
                                        for j in range(n_kt)], axis=1)
                s = _dot_nt(qh, k) + bias
                m = jnp.max(s, axis=-1, keepdims=True)
                p = jnp.exp(s - m)
                l = jnp.maximum(jnp.sum(p, axis=-1, keepdims=True), 1e-30)
                o_heads.append(_dot(p.astype(MXU_DTYPE), v) / l)
                lse_heads.append(m + jnp.log(l))
            o_ref[0, q_rows, lanes] = jnp.where(first, o_heads[0], o_heads[1])
            lse_ref[0, q_rows, lanes] = jnp.where(first, lse_heads[0], lse_heads[1])
            return carry

        lax.fori_loop(0, n_tiles, tile, 0, unroll=min(DIL_UNROLL, n_tiles))


def _dilated(q, k, v, tdil, dilation, batch, seq):
    ln = seq // dilation
    assert ln % DIL_WB == 0
    view = lambda a: a.reshape(batch, ln, dilation * LANES)
    spec = pl.BlockSpec((1, ln, dilation * LANES), lambda b: (b, 0, 0))
    out = jax.ShapeDtypeStruct((batch, ln, dilation * LANES), F32)
    o, lse = pl.pallas_call(
        functools.partial(_dilated_kernel, n_tiles=ln // DIL_WB, n_res=dilation),
        out_shape=[out, out],
        grid=(batch,),
        in_specs=[spec, spec, spec, _const_spec(tdil.shape)],
        out_specs=[spec, spec],
        compiler_params=_cparams(1),
        name=f"dilated_attention_d{dilation}",
    )(view(q), view(k), view(v), tdil)
    rows = batch * ln
    return o.reshape(rows, dilation * LANES), lse.reshape(rows, dilation * LANES)


def _mixer_out_kernel(x_ref, g_ref, ya_ref, yb_ref, o0_ref, o1_ref, o2_ref, l0_ref, l1_ref, l2_ref,
                      wmg_ref, wa_ref, wb_ref, wc_ref, wo_ref, out_ref, *stage_refs):
    x = x_ref[...]
    d = x.shape[1]
    tm = x.shape[0]
    h = _rms(x, g_ref[...]).astype(MXU_DTYPE)

    def token_major(ref, stage_ref):
        il = ref.shape[1] // LANES
        if il == 1:
            return ref[...]
        for r in range(il):
            stage_ref[pl.ds(r, tm // il, stride=il), :] = ref[:, r * LANES:(r + 1) * LANES]
        return stage_ref[...]

    o0, o1, o2, l0, l1, l2 = [token_major(r, s) for r, s in
                              zip((o0_ref, o1_ref, o2_ref, l0_ref, l1_ref, l2_ref), stage_refs)]
    mx = jnp.maximum(jnp.maximum(l0, l1), l2)
    e0, e1, e2 = jnp.exp(l0 - mx), jnp.exp(l1 - mx), jnp.exp(l2 - mx)
    y_c = (e0 * o0 + e1 * o1 + e2 * o2) / (e0 + e1 + e2)
    merged = jax.nn.sigmoid(_dot(h, wmg_ref[:, 0:d])) * _dot(ya_ref[...], wa_ref[...])
    merged += jax.nn.sigmoid(_dot(h, wmg_ref[:, d:2 * d])) * _dot(yb_ref[...], wb_ref[...])
    merged += jax.nn.sigmoid(_dot(h, wmg_ref[:, 2 * d:3 * d])) * _dot(y_c.astype(MXU_DTYPE), wc_ref[...])
    out_ref[...] = x + _dot(merged.astype(MXU_DTYPE), wo_ref[...])


def _mixer_out(x2, gain, ya, yb, dil, wmg, wa, wb, wc, wo, tm=1024):
    t, d = x2.shape
    tm = min(tm, t)
    row = lambda w: pl.BlockSpec((tm, w), lambda i: (i, 0))
    (o0, l0), (o1, l1), (o2, l2) = dil
    dil_arrays = (o0, o1, o2, l0, l1, l2)
    inter = lambda a: pl.BlockSpec((tm // (a.shape[1] // LANES), a.shape[1]), lambda i: (i, 0))
    return pl.pallas_call(
        _mixer_out_kernel,
        out_shape=jax.ShapeDtypeStruct((t, d), F32),
        grid=(t // tm,),
        in_specs=[row(d), _const_spec((1, d)), row(ya.shape[1]), row(yb.shape[1])] + [inter(a) for a in dil_arrays]
                 + [_const_spec(w.shape) for w in (wmg, wa, wb, wc, wo)],
        out_specs=row(d),
        scratch_shapes=[pltpu.VMEM((tm, LANES), F32)] * len(dil_arrays),
        compiler_params=_cparams(1),
        name="mixer_out_proj",
    )(x2, gain.reshape(1, d), ya, yb, o0, o1, o2, l0, l1, l2, wmg, wa, wb, wc, wo)


def _packed_rows(w, n_groups, heads_per_group):
    d = w.shape[1]
    w = w.reshape(n_groups, heads_per_group * HEAD_DIM, d)
    pad = _packed_width(heads_per_group) - heads_per_group * HEAD_DIM
    return jnp.pad(w, ((0, 0), (0, pad), (0, 0))).reshape(n_groups * _packed_width(heads_per_group), d)


def _compress_weights(pe, w1, w2):
    g_n, hd, half = NSA_KV_HEADS, HEAD_DIM, NSA_CMP_STRIDE
    hid = w1.shape[1]
    w1r = w1.reshape(2, half, hd, hid)
    zeros = jnp.zeros_like(w1r)
    per_group = [jnp.concatenate([w1r if k == g else zeros for k in range(g_n)], axis=-1) for g in range(g_n)]
    w1p = jnp.stack(per_group, axis=2).reshape(2, half * g_n * hd, g_n * hid)
    pep = jnp.broadcast_to(pe.reshape(2, half, 1, hd), (2, half, g_n, hd)).reshape(2, half * g_n * hd)
    w2p = jnp.concatenate([w2, jnp.zeros_like(w2)], axis=1)
    return pep, w1p.astype(MXU_DTYPE), w2p.astype(MXU_DTYPE)


def _cmp_to_slc(seq, n_cmp_pad):
    n_cmp = (seq - NSA_CMP_BLOCK) // NSA_CMP_STRIDE + 1
    n_slc = seq // NSA_SEL_BLOCK
    c_start = np.arange(n_cmp_pad) * NSA_CMP_STRIDE
    s_start = np.arange(LANES) * NSA_SEL_BLOCK
    ov = ((c_start[:, None] < s_start[None, :] + NSA_SEL_BLOCK) & (c_start[:, None] + NSA_CMP_BLOCK > s_start[None, :])
          & (np.arange(n_cmp_pad)[:, None] < n_cmp) & (np.arange(LANES)[None, :] < n_slc))
    return jnp.asarray(ov, F32)


def kernel(x, rel_bias, ffn1_norm, ffn1_w_gate, ffn1_w_up, ffn1_w_down, mix_norm, w_in, nsa_pe_k, nsa_pe_v,
           nsa_phi_k1, nsa_phi_k2, nsa_phi_v1, nsa_phi_v2, w_up_a, w_up_b, w_up_c, w_o, ffn2_norm, ffn2_w_gate,
           ffn2_w_up, ffn2_w_down, final_norm):
    batch, seq, d = x.shape
    depth = w_in.shape[0]
    nq = seq // TQ
    assert seq % (TQ * DIL_PATTERNS[-1][1]) == 0 and all(w // dl == DIL_WB for w, dl in DIL_PATTERNS)
    bf = lambda a: a.astype(MXU_DTYPE)

    n_cmp = (seq - NSA_CMP_BLOCK) // NSA_CMP_STRIDE + 1
    n_cmp_pad = seq // NSA_CMP_STRIDE
    a_heads = list(range(NSA_HEADS))
    ab_heads = list(range(NSA_HEADS + MOBA_HEADS))
    assert TILE_PAD >= max(WIN_TILES, KV_CHUNK - 1)
    t_main = _bias_tiles(rel_bias, n_tiles=nq + TILE_PAD, heads=ab_heads, offset=-TILE_PAD * TQ, name="bias_causal")
    t_win = _bias_tiles(rel_bias, n_tiles=WIN_TILES + NSA_QS + TILE_PAD, heads=a_heads, offset=-TILE_PAD * TQ,
                        hi=NSA_WINDOW - 1, name="bias_window")
    t_cmp = _bias_tiles(rel_bias, n_tiles=nq, heads=a_heads, width=n_cmp_pad, col_mult=NSA_CMP_STRIDE,
                        offset=-(NSA_CMP_BLOCK - 1), n_cols=n_cmp, name="bias_compressed")
    t_dil = []
    for g, (_, dilation) in enumerate(DIL_PATTERNS):
        h0 = NSA_HEADS + MOBA_HEADS + g * DIL_HEADS_PER_GROUP
        t_dil.append(_bias_tiles(rel_bias, n_tiles=3, heads=[h0, h0 + 1], dist_mult=dilation, hi=DIL_WB,
                                 name=f"bias_dilated_{g}"))
    c2s = _cmp_to_slc(seq, n_cmp_pad)

    x2 = x.reshape(batch * seq, d)
    for l in range(depth):
        last = l == depth - 1
        x2 = _ffn(x2, ffn1_norm[l], bf(ffn1_w_gate[l]), bf(ffn1_w_up[l]), bf(ffn1_w_down[l]))

        w_attn, w_merge = _proj_weights(w_in[l])
        p = _proj(x2, mix_norm[l], w_attn, seq)
        pek, wk1, wk2 = _compress_weights(nsa_pe_k[l], nsa_phi_k1[l], nsa_phi_k2[l])
        pev, wv1, wv2 = _compress_weights(nsa_pe_v[l], nsa_phi_v1[l], nsa_phi_v2[l])
        kc, vc = _compress(p["nsa_kc"], p["nsa_vc"], pek, pev, wk1, wv1, wk2, wv2, batch, seq)
        ya = _nsa(p, kc, vc, c2s, t_cmp, t_main, t_win, batch, seq)
        yb = _moba(p, p["moba_kmean"], t_main, batch, seq)
        dil = [_dilated(p[f"dil_q{g}"], p[f"dil_k{g}"], p[f"dil_v{g}"], t_dil[g], dilation, batch, seq)
               for g, (_, dilation) in enumerate(DIL_PATTERNS)]
        x2 = _mixer_out(x2, mix_norm[l], ya.reshape(batch * seq, -1), yb.reshape(batch * seq, -1), dil,
                        w_merge, bf(_packed_rows(w_up_a[l], NSA_KV_HEADS, NSA_REP)),
                        bf(_packed_rows(w_up_b[l], MOBA_HEADS // MOBA_HS, MOBA_HS)), bf(w_up_c[l]), bf(w_o[l]))

        x2 = _ffn(x2, ffn2_norm[l], bf(ffn2_w_gate[l]), bf(ffn2_w_up[l]), bf(ffn2_w_down[l]),
                  final_gain=final_norm if last else None)
    return x2.reshape(batch, seq, d)
```

```python
import functools
import math

import jax
import jax.numpy as jnp
import numpy as np
from jax import lax
from jax.experimental import pallas as pl
from jax.experimental.pallas import tpu as pltpu

HEAD_DIM = 64
NSA_HEADS = 6
NSA_KV_HEADS = 2
NSA_REP = NSA_HEADS // NSA_KV_HEADS
NSA_CMP_BLOCK = 32
NSA_CMP_STRIDE = 16
NSA_SEL_BLOCK = 64
NSA_N_SEL = 16
NSA_WINDOW = 512
NSA_CMP_HIDDEN = 256
MOBA_HEADS = 4
MOBA_BLOCK = 256
MOBA_TOPK = 3
DIL_PATTERNS = ((128, 1), (512, 4), (2048, 16))
DIL_HEADS_PER_GROUP = 2
DIL_WB = 128
REL_BUCKETS = 32
REL_MAX_EXACT = 16
REL_MAX_DIST = 2048
NORM_EPS = 1e-6
FORCE_SCORE = 1e4

LANES = 128
MXU_COLS = 256
SLOT = LANES
TQ = 128
TK = 128
KV_CHUNK = 4
WIN_TILES = NSA_WINDOW // TK
TILE_PAD = 10
NSA_QS = 2
MOBA_QB = 2
MOBA_HS = 2
DIL_UNROLL = 8
NEG = -1e30
M_INIT = -1e29
MXU_DTYPE = jnp.bfloat16
VMEM_LIMIT = 56 * 1024 * 1024
F32 = jnp.float32


def _cparams(n_grid, vmem=VMEM_LIMIT):
    return pltpu.CompilerParams(dimension_semantics=("arbitrary",) * n_grid, vmem_limit_bytes=vmem)


def _const_spec(shape):
    nd = len(shape)
    return pl.BlockSpec(shape, lambda *_: (0,) * nd, pipeline_mode=pl.Buffered(1))


def _dot(a, b):
    return jnp.dot(a, b, preferred_element_type=F32)


def _dot_nt(a, b):
    return lax.dot_general(a, b, (((1,), (1,)), ((), ())), preferred_element_type=F32)


def _rms(x, gain):
    return x * lax.rsqrt(jnp.mean(x * x, axis=-1, keepdims=True) + NORM_EPS) * gain


def _ffn_kernel(x_ref, g_ref, wg_ref, wu_ref, wd_ref, *rest, ff_chunk, final_norm):
    if final_norm:
        fg_ref, o_ref, acc_ref = rest
    else:
        o_ref, acc_ref = rest
    x = x_ref[...]
    h = _rms(x, g_ref[...]).astype(MXU_DTYPE)
    d_ff = wg_ref.shape[1]
    for c in range(d_ff // ff_chunk):
        sl = slice(c * ff_chunk, (c + 1) * ff_chunk)
        a = _dot(h, wg_ref[:, sl])
        u = _dot(h, wu_ref[:, sl])
        z = (a * jax.nn.sigmoid(a) * u).astype(MXU_DTYPE)
        part = _dot(z, wd_ref[sl, :])
        if c == 0:
            acc_ref[...] = part
        else:
            acc_ref[...] += part
    y = x + 0.5 * acc_ref[...]
    if final_norm:
        y = _rms(y, fg_ref[...])
    o_ref[...] = y


def _ffn(x2, gain, wg, wu, wd, final_gain=None, tm=1024):
    t, d = x2.shape
    d_ff = wg.shape[1]
    ff_chunk = MXU_COLS if d_ff % MXU_COLS == 0 else d_ff
    tm = min(tm, t)
    final = final_gain is not None
    in_specs = [pl.BlockSpec((tm, d), lambda i: (i, 0)), _const_spec((1, d)),
                _const_spec((d, d_ff)), _const_spec((d, d_ff)), _const_spec((d_ff, d))]
    args = [x2, gain.reshape(1, d), wg, wu, wd]
    if final:
        in_specs.append(_const_spec((1, d)))
        args.append(final_gain.reshape(1, d))
    return pl.pallas_call(
        functools.partial(_ffn_kernel, ff_chunk=ff_chunk, final_norm=final),
        out_shape=jax.ShapeDtypeStruct((t, d), F32),
        grid=(t // tm,),
        in_specs=in_specs,
        out_specs=pl.BlockSpec((tm, d), lambda i: (i, 0)),
        scratch_shapes=[pltpu.VMEM((tm, d), F32)],
        compiler_params=_cparams(1),
        name="ffn_swiglu",
    )(*args)


def _bias_tile_kernel(tbl_ref, o_ref, *, heads, width, row_step, col_mult, offset, dist_mult, lo, hi, n_cols):
    d = pl.program_id(0)
    raw_hi = row_step * d + (TQ - 1) + offset
    raw_lo = row_step * d - col_mult * (width - 1) + offset
    all_masked = (raw_hi < lo) | (raw_lo > hi)
    all_far = (raw_lo >= lo) & (raw_hi <= hi) & (raw_lo * dist_mult >= REL_MAX_DIST) & (n_cols == width)

    @pl.when(all_masked)
    def _():
        o_ref[...] = jnp.full(o_ref.shape, NEG, F32)

    @pl.when(all_far)
    def _():
        for hh, head in enumerate(heads):
            o_ref[0, hh] = jnp.full((TQ, width), tbl_ref[REL_BUCKETS - 1, head], F32)

    @pl.when(jnp.logical_not(all_masked | all_far))
    def _():
        i = lax.broadcasted_iota(jnp.int32, (TQ, width), 0)
        j = lax.broadcasted_iota(jnp.int32, (TQ, width), 1)
        raw = row_step * d + i - col_mult * j + offset
        valid = (raw >= lo) & (raw <= hi) & (j < n_cols)
        n = jnp.maximum(raw * dist_mult, 0)
        nf = jnp.maximum(n, REL_MAX_EXACT).astype(F32)
        large = REL_MAX_EXACT + (jnp.log(nf / REL_MAX_EXACT) / math.log(REL_MAX_DIST / REL_MAX_EXACT)
                                 * (REL_BUCKETS - REL_MAX_EXACT)).astype(jnp.int32)
        large = jnp.minimum(large, REL_BUCKETS - 1)
        bucket = jnp.where(n < REL_MAX_EXACT, n, large)
        for hh, head in enumerate(heads):
            val = jnp.zeros((TQ, width), F32)
            for k in range(REL_BUCKETS):
                val = jnp.where(bucket == k, tbl_ref[k, head], val)
            o_ref[0, hh] = jnp.where(valid, val, NEG)


def _bias_tiles(rel_bias, *, n_tiles, heads, width=TK, row_step=TQ, col_mult=1, offset=0, dist_mult=1,
                lo=0, hi=2 ** 30, n_cols=None, name):
    n_cols = width if n_cols is None else n_cols
    nh = len(heads)
    return pl.pallas_call(
        functools.partial(_bias_tile_kernel, heads=tuple(heads), width=width, row_step=row_step,
                          col_mult=col_mult, offset=offset, dist_mult=dist_mult, lo=lo, hi=hi, n_cols=n_cols),
        out_shape=jax.ShapeDtypeStruct((n_tiles, nh, TQ, width), F32),
        grid=(n_tiles,),
        in_specs=[pl.BlockSpec(memory_space=pltpu.SMEM)],
        out_specs=pl.BlockSpec((1, nh, TQ, width), lambda d: (d, 0, 0, 0)),
        compiler_params=_cparams(1),
        name=name,
    )(rel_bias)


_DILATIONS = tuple(d for _, d in DIL_PATTERNS)
_PROJ_SEGS = (
    ("nsa_q", NSA_HEADS * SLOT, MXU_DTYPE, 1, NSA_HEADS),
    ("nsa_kc", LANES, F32, NSA_CMP_STRIDE, 0),
    ("nsa_vc", LANES, F32, NSA_CMP_STRIDE, 0),
    ("nsa_ks", NSA_KV_HEADS * SLOT, MXU_DTYPE, 1, NSA_KV_HEADS),
    ("nsa_vs", NSA_KV_HEADS * SLOT, MXU_DTYPE, 1, NSA_KV_HEADS),
    ("nsa_kw", NSA_KV_HEADS * SLOT, MXU_DTYPE, 1, NSA_KV_HEADS),
    ("nsa_vw", NSA_KV_HEADS * SLOT, MXU_DTYPE, 1, NSA_KV_HEADS),
    ("nsa_gate", NSA_KV_HEADS * SLOT, F32, 1, NSA_KV_HEADS),
    ("moba_q", MOBA_HEADS * SLOT, MXU_DTYPE, 1, MOBA_HEADS),
    ("moba_k", MOBA_HEADS * SLOT, MXU_DTYPE, 1, MOBA_HEADS),
    ("moba_v", MOBA_HEADS * SLOT, MXU_DTYPE, 1, MOBA_HEADS),
) + tuple((f"dil_{n}{g}", LANES, MXU_DTYPE, d, 0) for n in "qkv" for g, d in enumerate(_DILATIONS))
_PROJ_WCOLS = [s[4] * HEAD_DIM if s[4] else s[1] for s in _PROJ_SEGS]
_PROJ_OFFS = np.concatenate([[0], np.cumsum(_PROJ_WCOLS)])
_PROJ_COLS = int(_PROJ_OFFS[-1])


def _proj_weights(w_l):
    hd = HEAD_DIM
    d_model = w_l.shape[0]
    kvw = NSA_KV_HEADS * hd
    widths = (("nsa_q", NSA_HEADS * hd), ("nsa_k_cmp", kvw), ("nsa_v_cmp", kvw), ("nsa_k_sel", kvw),
              ("nsa_v_sel", kvw), ("nsa_k_win", kvw), ("nsa_v_win", kvw), ("nsa_gate", NSA_HEADS * 3),
              ("moba_q", MOBA_HEADS * hd), ("moba_k", MOBA_HEADS * hd), ("moba_v", MOBA_HEADS * hd),
              ("dil_q", 6 * hd), ("dil_k", 6 * hd), ("dil_v", 6 * hd))
    cols, o = {}, 0
    for name, w in widths:
        cols[name] = w_l[:, o:o + w]
        o += w
    qk_scale = hd ** -0.5

    def slots(w, n, real):
        w = w.reshape(d_model, n, real)
        return jnp.pad(w, ((0, 0), (0, 0), (0, hd - real))).reshape(d_model, n * hd)

    parts = {
        "nsa_q": slots(cols["nsa_q"] * qk_scale, NSA_HEADS, hd),
        "nsa_kc": cols["nsa_k_cmp"], "nsa_vc": cols["nsa_v_cmp"],
        "nsa_ks": slots(cols["nsa_k_sel"], NSA_KV_HEADS, hd), "nsa_vs": slots(cols["nsa_v_sel"], NSA_KV_HEADS, hd),
        "nsa_kw": slots(cols["nsa_k_win"], NSA_KV_HEADS, hd), "nsa_vw": slots(cols["nsa_v_win"], NSA_KV_HEADS, hd),
        "nsa_gate": slots(cols["nsa_gate"], NSA_KV_HEADS, NSA_REP * 3),
        "moba_q": slots(cols["moba_q"] * qk_scale, MOBA_HEADS, hd),
        "moba_k": slots(cols["moba_k"], MOBA_HEADS, hd), "moba_v": slots(cols["moba_v"], MOBA_HEADS, hd),
    }
    for g in range(len(DIL_PATTERNS)):
        sl = slice(g * 2 * hd, (g + 1) * 2 * hd)
        parts[f"dil_q{g}"] = cols["dil_q"][:, sl] * qk_scale
        parts[f"dil_k{g}"] = cols["dil_k"][:, sl]
        parts[f"dil_v{g}"] = cols["dil_v"][:, sl]
    w_attn = jnp.concatenate([parts[seg[0]] for seg in _PROJ_SEGS], axis=1)
    return w_attn.astype(MXU_DTYPE), w_l[:, o:].astype(MXU_DTYPE)


def _proj_kernel(x_ref, g_ref, w_ref, *refs, seq, tm):
    names = [s[0] for s in _PROJ_SEGS] + ["moba_kmean"]
    outs = dict(zip(names, refs[:len(names)]))
    stage_ref = refs[len(names)]
    h = _rms(x_ref[...], g_ref[...]).astype(MXU_DTYPE)
    pos0 = (pl.program_id(0) * tm) % seq
    lower = lax.broadcasted_iota(jnp.int32, (tm, LANES), 1) < HEAD_DIM
    group_y, group_c0 = None, 0
    for si, (name, width, dtype, inter, n_slots) in enumerate(_PROJ_SEGS):
        c0 = int(_PROJ_OFFS[si])
        if group_y is None or c0 >= group_c0 + group_y.shape[1]:
            c1 = next((int(o) for o in _PROJ_OFFS[si + 1:] if (int(o) - c0) % MXU_COLS == 0), _PROJ_COLS)
            group_y, group_c0 = _dot(h, w_ref[:, c0:c1]), c0
        y = group_y[:, c0 - group_c0:c0 - group_c0 + _PROJ_WCOLS[si]]
        if n_slots:
            pieces = []
            for pair in range(n_slots // 2):
                v = y[:, pair * LANES:(pair + 1) * LANES]
                pieces += [jnp.where(lower, v, 0.0), jnp.where(lower, pltpu.roll(v, HEAD_DIM, 1), 0.0)]
            y = jnp.concatenate(pieces, axis=1)
        if inter > 1:
            stage_ref[...] = y
            for r in range(inter):
                outs[name][:, r * LANES:(r + 1) * LANES] = stage_ref[pl.ds(r, tm // inter, stride=inter), :].astype(dtype)
            continue
        if name == "moba_k":
            nblk = tm // MOBA_BLOCK
            outs["moba_kmean"][0] = jnp.mean(y.reshape(nblk, MOBA_BLOCK, width), axis=1)
        if name in ("nsa_ks", "moba_k"):
            blk = NSA_SEL_BLOCK if name == "nsa_ks" else MOBA_BLOCK
            row = lax.broadcasted_iota(jnp.int32, (tm, width), 0)
            lane = lax.broadcasted_iota(jnp.int32, (tm, width), 1) & (SLOT - 1)
            hit = (lane - HEAD_DIM) == jnp.right_shift(pos0 + row, int(math.log2(blk)))
            y = jnp.where(hit, 1.0, y)
        if name in ("nsa_vs", "nsa_vw", "moba_v"):
            lane = lax.broadcasted_iota(jnp.int32, (tm, width), 1) & (SLOT - 1)
            y = jnp.where(lane == HEAD_DIM, 1.0, y)
        if name == "nsa_gate":
            y = jax.nn.sigmoid(y)
        outs[name][...] = y.astype(dtype)


def _proj(x2, gain, w_attn, seq, tm=1024):
    t, d = x2.shape
    tm = min(tm, seq)
    assert tm % MOBA_BLOCK == 0 and seq % tm == 0
    nblk = tm // MOBA_BLOCK
    out_shape = [jax.ShapeDtypeStruct((t // il, w * il), dt) for _, w, dt, il, _ in _PROJ_SEGS]
    out_specs = [pl.BlockSpec((tm // il, w * il), lambda i: (i, 0)) for _, w, _, il, _ in _PROJ_SEGS]
    kw = MOBA_HEADS * SLOT
    out_shape.append(jax.ShapeDtypeStruct((t // tm, nblk, kw), F32))
    out_specs.append(pl.BlockSpec((1, nblk, kw), lambda i: (i, 0, 0)))
    outs = pl.pallas_call(
        functools.partial(_proj_kernel, seq=seq, tm=tm),
        out_shape=out_shape,
        grid=(t // tm,),
        in_specs=[pl.BlockSpec((tm, d), lambda i: (i, 0)), _const_spec((1, d)), _const_spec((d, _PROJ_COLS))],
        out_specs=out_specs,
        scratch_shapes=[pltpu.VMEM((tm, LANES), F32)],
        compiler_params=_cparams(1),
        name="mixer_in_proj",
    )(x2, gain.reshape(1, d), w_attn)
    res = {s[0]: o for s, o in zip(_PROJ_SEGS, outs)}
    res["moba_kmean"] = outs[-1]
    return res


def _gelu_tanh(x):
    return 0.5 * x * (1.0 + jnp.tanh(math.sqrt(2.0 / math.pi) * (x + 0.044715 * (x * x * x))))


def _compress_kernel(k_ref, v_ref, pek_ref, pev_ref, wk1_ref, wv1_ref, wk2_ref, wv2_ref, kc_ref, vc_ref):
    nr = k_ref.shape[1]
    hid_w = NSA_CMP_HIDDEN
    for x_ref, pe_ref, w1_ref, w2_ref, o_ref in ((k_ref, pek_ref, wk1_ref, wk2_ref, kc_ref),
                                                 (v_ref, pev_ref, wv1_ref, wv2_ref, vc_ref)):
        r = x_ref[0]
        lo = _dot((r + pe_ref[0:1, :]).astype(MXU_DTYPE), w1_ref[0])
        hi = _dot((r + pe_ref[1:2, :]).astype(MXU_DTYPE), w1_ref[1])
        hid = lo + pltpu.roll(hi, nr - 1, 0)
        act = _gelu_tanh(hid).astype(MXU_DTYPE)
        for g in range(NSA_KV_HEADS):
            o_ref[0, g] = _dot(act[:, g * hid_w:(g + 1) * hid_w], w2_ref[...]).astype(o_ref.dtype)


def _compress(kc_in, vc_in, pek, pev, wk1, wv1, wk2, wv2, batch, seq):
    nr = seq // NSA_CMP_STRIDE
    rw = NSA_CMP_STRIDE * LANES
    kin = kc_in.reshape(batch, nr, rw)
    vin = vc_in.reshape(batch, nr, rw)
    hw = NSA_KV_HEADS * NSA_CMP_HIDDEN
    out = jax.ShapeDtypeStruct((batch, NSA_KV_HEADS, nr, SLOT), MXU_DTYPE)
    in_blk = pl.BlockSpec((1, nr, rw), lambda b: (b, 0, 0))
    out_blk = pl.BlockSpec((1, NSA_KV_HEADS, nr, SLOT), lambda b: (b, 0, 0, 0))
    return pl.pallas_call(
        _compress_kernel,
        out_shape=[out, out],
        grid=(batch,),
        in_specs=[in_blk, in_blk, _const_spec((2, rw)), _const_spec((2, rw)),
                  _const_spec((2, rw, hw)), _const_spec((2, rw, hw)),
                  _const_spec((NSA_CMP_HIDDEN, SLOT)), _const_spec((NSA_CMP_HIDDEN, SLOT))],
        out_specs=[out_blk, out_blk],
        compiler_params=_cparams(1),
        name="nsa_compress",
    )(kin, vin, pek, pev, wk1, wv1, wk2, wv2)


def _rank_before(score):
    n, tq = score.shape
    sub_rows = 8
    assert n % sub_rows == 0
    bits = pltpu.bitcast(score, jnp.int32)
    key = bits ^ (jnp.right_shift(bits, 31) & 0x7FFFFFFF)
    keys = [key[b * sub_rows:(b + 1) * sub_rows] for b in range(n // sub_rows)]
    sub = lax.broadcasted_iota(jnp.int32, (sub_rows, tq), 0)
    adj = [k - 1 for k in keys]
    rank = [jnp.zeros((sub_rows, tq), jnp.int32) for _ in keys]
    for j in range(n):
        b, r = divmod(j, sub_rows)
        adj[b] = adj[b] + jnp.where(sub == r, 1, 0)
        row = keys[b][r:r + 1, :]
        rank = [rk + jnp.where(row > a, 1, 0) for rk, a in zip(rank, adj)]
    return jnp.concatenate(rank, axis=0)


def _bias_block(t_ref, q_tile0, k_tile0, n_stack, n_qs, n_kt):
    rows = []
    for r in range(n_stack):
        for a in range(n_qs):
            base = q_tile0 + a - k_tile0 + TILE_PAD
            rows.append(jnp.concatenate([t_ref[base - j, r] for j in range(n_kt)], axis=1))
    return jnp.concatenate(rows, axis=0)


def _flash_chunks(q, k_ref, v_ref, t_ref, q_tile0, n_stack, n_qs, n_chunks, scratch):
    q_s, s_a, s_b, p_a, p_b, m_s, a_s, acc_s = scratch
    rows = q.shape[0]
    kw = KV_CHUNK * TK
    stat = (rows, LANES)

    def chunk_rows(c):
        return pl.ds(pl.multiple_of(c * kw, kw), kw)

    k_refs = k_ref if isinstance(k_ref, (list, tuple)) else [k_ref]
    v_refs = v_ref if isinstance(v_ref, (list, tuple)) else [v_ref]
    grp = rows // len(k_refs)
    groups = [slice(i * grp, (i + 1) * grp) for i in range(len(k_refs))]

    def logits_into(c, s_ref):
        bias = _bias_block(t_ref, q_tile0, c * KV_CHUNK, n_stack, n_qs, KV_CHUNK)
        for rows_g, kg in zip(groups, k_refs):
            s_ref[rows_g, :] = _dot_nt(q_s[rows_g, :], kg[chunk_rows(c), :]) + bias[rows_g]

    def row_max(s):
        tiles = [s[:, j * TK:(j + 1) * TK] for j in range(KV_CHUNK)]
        return jnp.broadcast_to(jnp.max(functools.reduce(jnp.maximum, tiles), axis=-1, keepdims=True),
                                (s.shape[0], LANES))

    def pv(p_ref, c):
        for rows_g, vg in zip(groups, v_refs):
            acc_s[rows_g, :] = a_s[rows_g, :] * acc_s[rows_g, :] + _dot(p_ref[rows_g, :], vg[chunk_rows(c), :])

    def step(c, s_cur, s_nxt, p_cur, p_prev, prefetch=True):
        pv(p_prev, jnp.maximum(c - 1, 0))
        if prefetch:
            logits_into(jnp.minimum(c + 1, n_chunks - 1), s_nxt)
        m_prev = m_s[...]
        m_new = jnp.maximum(m_prev, row_max(s_cur[...]))
        a_s[...] = jnp.exp(m_prev - m_new)
        m_s[...] = m_new
        for j in range(KV_CHUNK):
            p_cur[:, j * TK:(j + 1) * TK] = jnp.exp(s_cur[:, j * TK:(j + 1) * TK] - m_new).astype(MXU_DTYPE)

    q_s[...] = q
    m_s[...] = jnp.full(stat, M_INIT, F32)
    a_s[...] = jnp.ones(stat, F32)
    acc_s[...] = jnp.zeros(stat, F32)
    p_b[...] = jnp.zeros(p_b.shape, MXU_DTYPE)
    logits_into(0, s_a)

    def body(i, carry):
        step(2 * i, s_a, s_b, p_a, p_b)
        step(2 * i + 1, s_b, s_a, p_b, p_a)
        return carry

    lax.fori_loop(0, n_chunks // 2, body, 0)
    last = n_chunks - 1

    @pl.when(last % 2 == 0)
    def _():
        step(last, s_a, s_b, p_a, p_b, prefetch=False)
        pv(p_a, last)

    @pl.when(last % 2 == 1)
    def _():
        pv(p_b, last)
    return _normalize(acc_s[...])


def _flash_scratch(rows):
    kw = KV_CHUNK * TK
    return ([pltpu.VMEM((rows, LANES), MXU_DTYPE)]
            + [pltpu.VMEM((rows, kw), F32)] * 2
            + [pltpu.VMEM((rows, kw), MXU_DTYPE)] * 2 + [pltpu.VMEM((rows, LANES), F32)] * 3)


def _normalize(acc):
    return acc / jnp.maximum(acc[:, HEAD_DIM:HEAD_DIM + 1], 1e-30)


def _attend_once(q, k, v, bias):
    s = _dot_nt(q, k) + bias
    m = jnp.maximum(jnp.max(s, axis=-1, keepdims=True), M_INIT)
    p = jnp.exp(s - m)
    return _normalize(_dot(p.astype(MXU_DTYPE), v))


def _packed_width(n_heads):
    return LANES * ((n_heads + 1) // 2)


def _pack_heads(heads):
    rows = heads[0].shape[0]
    lower = lax.broadcasted_iota(jnp.int32, (rows, LANES), 1) < HEAD_DIM
    tiles = []
    for i in range(0, len(heads), 2):
        upper = pltpu.roll(heads[i + 1], HEAD_DIM, 1) if i + 1 < len(heads) else jnp.zeros_like(heads[i])
        tiles.append(jnp.where(lower, heads[i], upper))
    return jnp.concatenate(tiles, axis=1)


def _block_mask_lanes(z_ref, sel_t, tq):
    n = sel_t.shape[0]
    z_ref[...] = jnp.zeros(z_ref.shape, F32)
    z_ref[HEAD_DIM:HEAD_DIM + n, :] = jnp.where(sel_t, 0.0, NEG)
    return z_ref[...].T


def _nsa_kernel(q_ref, gate_ref, kc_ref, vc_ref, ks_ref, vs_ref, kw_ref, vw_ref, c2s_ref,
                tcmp_ref, tsel_ref, twin_ref, o_ref, z_ref, *flash_scratch, n_slc):
    step = pl.program_id(2)
    qt0 = step * NSA_QS
    rep, nqs = NSA_REP, NSA_QS
    tqb = nqs * TQ
    q_all = q_ref[0]
    q = jnp.concatenate([q_all[a * TQ:(a + 1) * TQ, r * SLOT:(r + 1) * SLOT]
                         for r in range(rep) for a in range(nqs)], axis=0)

    n_wt = WIN_TILES + nqs
    w0 = jnp.maximum(qt0 - WIN_TILES, 0)
    w_rows = pl.ds(pl.multiple_of(w0 * TK, TK), n_wt * TK)
    o_w = _attend_once(q, kw_ref[0, w_rows, :], vw_ref[0, w_rows, :],
                       _bias_block(twin_ref, qt0, w0, rep, nqs, n_wt))

    o_c_tiles, sel_tiles = [], []
    for a in range(nqs):
        q_a = jnp.concatenate([q[(r * nqs + a) * TQ:(r * nqs + a + 1) * TQ] for r in range(rep)], axis=0)
        bias_c = jnp.concatenate([tcmp_ref[a, r] for r in range(rep)], axis=0)
        sc = _dot_nt(q_a, kc_ref[0, 0]) + bias_c
        mc = jnp.maximum(jnp.max(sc, axis=-1, keepdims=True), M_INIT)
        pc = jnp.exp(sc - mc)
        pc = pc / jnp.maximum(jnp.sum(pc, axis=-1, keepdims=True), 1e-30)
        o_c_tiles.append(_dot(pc.astype(MXU_DTYPE), vc_ref[0, 0]))

        p_sum = pc[0:TQ]
        for r in range(1, rep):
            p_sum = p_sum + pc[r * TQ:(r + 1) * TQ]
        p_hi = p_sum.astype(MXU_DTYPE)
        p_lo = (p_sum - p_hi.astype(F32)).astype(MXU_DTYPE)
        imp = _dot(p_hi, c2s_ref[...]) + _dot(p_lo, c2s_ref[...])
        imp_t = imp.T[0:n_slc]
        blk = lax.broadcasted_iota(jnp.int32, (n_slc, TQ), 0)
        t_pos = (qt0 + a) * TQ + lax.broadcasted_iota(jnp.int32, (n_slc, TQ), 1)
        cur = jnp.right_shift(t_pos, int(math.log2(NSA_SEL_BLOCK)))
        forced = (blk == 0) | (blk == cur) | (blk == cur - 1)
        score = jnp.where(blk > cur, NEG, jnp.where(forced, FORCE_SCORE, imp_t))
        sel_tiles.append(jnp.where(_rank_before(score) < min(NSA_N_SEL, n_slc), 0.0, NEG))
    o_c = jnp.concatenate([o_c_tiles[a][r * TQ:(r + 1) * TQ] for r in range(rep) for a in range(nqs)], axis=0)
    z_ref[...] = jnp.zeros(z_ref.shape, F32)
    z_ref[HEAD_DIM:HEAD_DIM + n_slc, :] = jnp.concatenate(sel_tiles, axis=1)
    zt = z_ref[...].T
    q_sel = (q.astype(F32) + jnp.concatenate([zt] * rep, axis=0)).astype(MXU_DTYPE)

    n_chunks = (qt0 + nqs - 1) // KV_CHUNK + 1
    o_s = _flash_chunks(q_sel, ks_ref.at[0], vs_ref.at[0], tsel_ref, qt0, rep, nqs, n_chunks, flash_scratch)

    gates = gate_ref[0]
    heads = []
    for r in range(rep):
        sl = slice(r * tqb, (r + 1) * tqb)
        heads.append(gates[:, 3 * r:3 * r + 1] * o_c[sl] + gates[:, 3 * r + 1:3 * r + 2] * o_s[sl]
                     + gates[:, 3 * r + 2:3 * r + 3] * o_w[sl])
    o_ref[0] = _pack_heads(heads).astype(o_ref.dtype)


def _nsa(p, kc, vc, c2s, tcmp, tmain, twin, batch, seq):
    nq = seq // TQ
    n_slc = seq // NSA_SEL_BLOCK
    tqb = NSA_QS * TQ
    assert n_slc <= HEAD_DIM and nq % KV_CHUNK == 0 and nq % NSA_QS == 0 and nq >= WIN_TILES + NSA_QS
    assert TILE_PAD >= KV_CHUNK + NSA_QS - 2
    g_n, rep = NSA_KV_HEADS, NSA_REP
    n_cmp_pad = kc.shape[2]
    r3 = lambda a: a.reshape(batch, seq, a.shape[-1])
    kv_spec = pl.BlockSpec((1, seq, SLOT), lambda b, g, i: (b, 0, g))
    cmp_spec = pl.BlockSpec((1, 1, n_cmp_pad, SLOT), lambda b, g, i: (b, g, 0, 0))
    return pl.pallas_call(
        functools.partial(_nsa_kernel, n_slc=n_slc),
        out_shape=jax.ShapeDtypeStruct((batch, seq, g_n * _packed_width(rep)), MXU_DTYPE),
        grid=(batch, g_n, nq // NSA_QS),
        in_specs=[
            pl.BlockSpec((1, tqb, rep * SLOT), lambda b, g, i: (b, i, g)),
            pl.BlockSpec((1, tqb, SLOT), lambda b, g, i: (b, i, g)),
            cmp_spec, cmp_spec, kv_spec, kv_spec, kv_spec, kv_spec,
            _const_spec(c2s.shape),
            pl.BlockSpec((NSA_QS, rep, TQ, n_cmp_pad), lambda b, g, i: (i, g, 0, 0)),
            pl.BlockSpec((tmain.shape[0], rep, TQ, TK), lambda b, g, i: (0, g, 0, 0),
                         pipeline_mode=pl.Buffered(1)),
            pl.BlockSpec((twin.shape[0], rep, TQ, TK), lambda b, g, i: (0, g, 0, 0),
                         pipeline_mode=pl.Buffered(1)),
        ],
        out_specs=pl.BlockSpec((1, tqb, _packed_width(rep)), lambda b, g, i: (b, i, g)),
        scratch_shapes=[pltpu.VMEM((LANES, tqb), F32)] + _flash_scratch(rep * tqb),
        compiler_params=_cparams(3),
        name="nsa_attention",
    )(r3(p["nsa_q"]), r3(p["nsa_gate"]), kc, vc, r3(p["nsa_ks"]), r3(p["nsa_vs"]), r3(p["nsa_kw"]),
      r3(p["nsa_vw"]), c2s.astype(MXU_DTYPE), tcmp, tmain, twin)


def _moba_kernel(q_ref, *refs, nb):
    hs = MOBA_HS
    k_refs, v_refs = refs[0:hs], refs[hs:2 * hs]
    km_ref, t_ref, o_ref, z_ref = refs[2 * hs:2 * hs + 4]
    flash_scratch = refs[2 * hs + 4:]
    step = pl.program_id(2)
    tqb = MOBA_QB * MOBA_BLOCK
    n_qs = tqb // TQ
    n_i = lax.broadcasted_iota(jnp.int32, (nb, tqb), 0)
    own = step * MOBA_QB + jnp.right_shift(lax.broadcasted_iota(jnp.int32, (nb, tqb), 1), int(math.log2(MOBA_BLOCK)))
    past = n_i < own
    q_sel = []
    for h in range(hs):
        q = q_ref[0, :, h * SLOT:(h + 1) * SLOT]
        km = km_ref[0, :, h * SLOT:(h + 1) * SLOT]
        km_hi = km.astype(MXU_DTYPE)
        rem = km - km_hi.astype(F32)
        km_mid = rem.astype(MXU_DTYPE)
        km_lo = (rem - km_mid.astype(F32)).astype(MXU_DTYPE)
        gate_t = _dot_nt(km_hi, q) + _dot_nt(km_mid, q) + _dot_nt(km_lo, q)
        score = jnp.where(past, gate_t, NEG)
        sel_t = ((_rank_before(score) < min(MOBA_TOPK, nb - 1)) & past) | (n_i == own)
        zt = _block_mask_lanes(z_ref.at[h], sel_t, tqb)
        q_sel.append((q.astype(F32) + zt).astype(MXU_DTYPE))
    q_tile0 = step * n_qs
    n_chunks = (q_tile0 + n_qs - 1) // KV_CHUNK + 1
    o = _flash_chunks(jnp.concatenate(q_sel, axis=0), [r.at[0] for r in k_refs], [r.at[0] for r in v_refs], t_ref,
                      q_tile0, hs, n_qs, n_chunks, flash_scratch)
    o_ref[0] = _pack_heads([o[h * tqb:(h + 1) * tqb] for h in range(hs)]).astype(o_ref.dtype)


def _moba(p, kmean, tmain, batch, seq):
    nq = seq // TQ
    nb = seq // MOBA_BLOCK
    tqb = MOBA_QB * MOBA_BLOCK
    hs = MOBA_HS
    assert seq % tqb == 0 and nb <= HEAD_DIM and MOBA_BLOCK % TQ == 0 and nq % KV_CHUNK == 0
    assert TILE_PAD >= KV_CHUNK + tqb // TQ - 2 and MOBA_HEADS % hs == 0 and NSA_HEADS % hs == 0
    r3 = lambda a: a.reshape(batch, seq, a.shape[-1])
    kv_specs = [pl.BlockSpec((1, seq, SLOT), lambda b, g, i, h=h: (b, 0, g * hs + h)) for h in range(hs)]
    q_spec = pl.BlockSpec((1, tqb, hs * SLOT), lambda b, g, i: (b, i, g))
    return pl.pallas_call(
        functools.partial(_moba_kernel, nb=nb),
        out_shape=jax.ShapeDtypeStruct((batch, seq, MOBA_HEADS // hs * _packed_width(hs)), MXU_DTYPE),
        grid=(batch, MOBA_HEADS // hs, seq // tqb),
        in_specs=[q_spec] + kv_specs + kv_specs
                 + [pl.BlockSpec((1, nb, hs * SLOT), lambda b, g, i: (b, 0, g)),
                    pl.BlockSpec((tmain.shape[0], hs, TQ, TK), lambda b, g, i: (0, NSA_HEADS // hs + g, 0, 0))],
        out_specs=pl.BlockSpec((1, tqb, _packed_width(hs)), lambda b, g, i: (b, i, g)),
        scratch_shapes=[pltpu.VMEM((hs, LANES, tqb), F32)] + _flash_scratch(hs * tqb),
        compiler_params=_cparams(3),
        name="moba_attention",
    )(r3(p["moba_q"]), *([r3(p["moba_k"])] * hs), *([r3(p["moba_v"])] * hs),
      kmean.reshape(batch, nb, MOBA_HEADS * SLOT), tmain)


def _dilated_kernel(q_ref, k_ref, v_ref, t_ref, o_ref, lse_ref, *, n_tiles, n_res):
    lane = lax.broadcasted_iota(jnp.int32, (DIL_WB, LANES), 1)
    first = lane < HEAD_DIM
    n_kt = min(2, n_tiles)

    for res in range(n_res):
        lanes = slice(res * LANES, (res + 1) * LANES)

        def tile(n, carry, lanes=lanes):
            kv0 = jnp.maximum(n - 1, 0)
            q_rows = pl.ds(pl.multiple_of(n * DIL_WB, DIL_WB), DIL_WB)
            kv_rows = pl.ds(pl.multiple_of(kv0 * DIL_WB, DIL_WB), n_kt * DIL_WB)
            q = q_ref[0, q_rows, lanes]
            k = k_ref[0, kv_rows, lanes]
            v = v_ref[0, kv_rows, lanes]
            o_heads, lse_heads = [], []
            for h in range(DIL_HEADS_PER_GROUP):
                qh = jnp.where(first if h == 0 else ~first, q, jnp.zeros_like(q))
                bias = jnp.concatenate([t_ref[(n - kv0) - j + (0 if j == 0 else 3 * (1 - (n - kv0))), h]
                                        for j in range(n_kt)], axis=1)
                s = _dot_nt(qh, k) + bias
                m = jnp.max(s, axis=-1, keepdims=True)
                p = jnp.exp(s - m)
                l = jnp.maximum(jnp.sum(p, axis=-1, keepdims=True), 1e-30)
                o_heads.append(_dot(p.astype(MXU_DTYPE), v) / l)
                lse_heads.append(m + jnp.log(l))
            o_ref[0, q_rows, lanes] = jnp.where(first, o_heads[0], o_heads[1])
            lse_ref[0, q_rows, lanes] = jnp.where(first, lse_heads[0], lse_heads[1])
            return carry

        lax.fori_loop(0, n_tiles, tile, 0, unroll=min(DIL_UNROLL, n_tiles))


def _dilated(q, k, v, tdil, dilation, batch, seq):
    ln = seq // dilation
    assert ln % DIL_WB == 0
    view = lambda a: a.reshape(batch, ln, dilation * LANES)
    spec = pl.BlockSpec((1, ln, dilation * LANES), lambda b: (b, 0, 0))
    out = jax.ShapeDtypeStruct((batch, ln, dilation * LANES), F32)
    o, lse = pl.pallas_call(
        functools.partial(_dilated_kernel, n_tiles=ln // DIL_WB, n_res=dilation),
        out_shape=[out, out],
        grid=(batch,),
        in_specs=[spec, spec, spec, _const_spec(tdil.shape)],
        out_specs=[spec, spec],
        compiler_params=_cparams(1),
        name=f"dilated_attention_d{dilation}",
    )(view(q), view(k), view(v), tdil)
    rows = batch * ln
    return o.reshape(rows, dilation * LANES), lse.reshape(rows, dilation * LANES)


def _mixer_out_kernel(x_ref, g_ref, ya_ref, yb_ref, o0_ref, o1_ref, o2_ref, l0_ref, l1_ref, l2_ref,
                      wmg_ref, wa_ref, wb_ref, wc_ref, wo_ref, out_ref, *stage_refs):
    x = x_ref[...]
    d = x.shape[1]
    tm = x.shape[0]
    h = _rms(x, g_ref[...]).astype(MXU_DTYPE)

    def token_major(ref, stage_ref):
        il = ref.shape[1] // LANES
        if il == 1:
            return ref[...]
        for r in range(il):
            stage_ref[pl.ds(r, tm // il, stride=il), :] = ref[:, r * LANES:(r + 1) * LANES]
        return stage_ref[...]

    o0, o1, o2, l0, l1, l2 = [token_major(r, s) for r, s in
                              zip((o0_ref, o1_ref, o2_ref, l0_ref, l1_ref, l2_ref), stage_refs)]
    mx = jnp.maximum(jnp.maximum(l0, l1), l2)
    e0, e1, e2 = jnp.exp(l0 - mx), jnp.exp(l1 - mx), jnp.exp(l2 - mx)
    y_c = (e0 * o0 + e1 * o1 + e2 * o2) / (e0 + e1 + e2)
    merged = jax.nn.sigmoid(_dot(h, wmg_ref[:, 0:d])) * _dot(ya_ref[...], wa_ref[...])
    merged += jax.nn.sigmoid(_dot(h, wmg_ref[:, d:2 * d])) * _dot(yb_ref[...], wb_ref[...])
    merged += jax.nn.sigmoid(_dot(h, wmg_ref[:, 2 * d:3 * d])) * _dot(y_c.astype(MXU_DTYPE), wc_ref[...])
    out_ref[...] = x + _dot(merged.astype(MXU_DTYPE), wo_ref[...])


def _mixer_out(x2, gain, ya, yb, dil, wmg, wa, wb, wc, wo, tm=1024):
    t, d = x2.shape
    tm = min(tm, t)
    row = lambda w: pl.BlockSpec((tm, w), lambda i: (i, 0))
    (o0, l0), (o1, l1), (o2, l2) = dil
    dil_arrays = (o0, o1, o2, l0, l1, l2)
    inter = lambda a: pl.BlockSpec((tm // (a.shape[1] // LANES), a.shape[1]), lambda i: (i, 0))
    return pl.pallas_call(
        _mixer_out_kernel,
        out_shape=jax.ShapeDtypeStruct((t, d), F32),
        grid=(t // tm,),
        in_specs=[row(d), _const_spec((1, d)), row(ya.shape[1]), row(yb.shape[1])] + [inter(a) for a in dil_arrays]
                 + [_const_spec(w.shape) for w in (wmg, wa, wb, wc, wo)],
        out_specs=row(d),
        scratch_shapes=[pltpu.VMEM((tm, LANES), F32)] * len(dil_arrays),
        compiler_params=_cparams(1),
        name="mixer_out_proj",
    )(x2, gain.reshape(1, d), ya, yb, o0, o1, o2, l0, l1, l2, wmg, wa, wb, wc, wo)


def _packed_rows(w, n_groups, heads_per_group):
    d = w.shape[1]
    w = w.reshape(n_groups, heads_per_group * HEAD_DIM, d)
    pad = _packed_width(heads_per_group) - heads_per_group * HEAD_DIM
    return jnp.pad(w, ((0, 0), (0, pad), (0, 0))).reshape(n_groups * _packed_width(heads_per_group), d)


def _compress_weights(pe, w1, w2):
    g_n, hd, half = NSA_KV_HEADS, HEAD_DIM, NSA_CMP_STRIDE
    hid = w1.shape[1]
    w1r = w1.reshape(2, half, hd, hid)
    zeros = jnp.zeros_like(w1r)
    per_group = [jnp.concatenate([w1r if k == g else zeros for k in range(g_n)], axis=-1) for g in range(g_n)]
    w1p = jnp.stack(per_group, axis=2).reshape(2, half * g_n * hd, g_n * hid)
    pep = jnp.broadcast_to(pe.reshape(2, half, 1, hd), (2, half, g_n, hd)).reshape(2, half * g_n * hd)
    w2p = jnp.concatenate([w2, jnp.zeros_like(w2)], axis=1)
    return pep, w1p.astype(MXU_DTYPE), w2p.astype(MXU_DTYPE)


def _cmp_to_slc(seq, n_cmp_pad):
    n_cmp = (seq - NSA_CMP_BLOCK) // NSA_CMP_STRIDE + 1
    n_slc = seq // NSA_SEL_BLOCK
    c_start = np.arange(n_cmp_pad) * NSA_CMP_STRIDE
    s_start = np.arange(LANES) * NSA_SEL_BLOCK
    ov = ((c_start[:, None] < s_start[None, :] + NSA_SEL_BLOCK) & (c_start[:, None] + NSA_CMP_BLOCK > s_start[None, :])
          & (np.arange(n_cmp_pad)[:, None] < n_cmp) & (np.arange(LANES)[None, :] < n_slc))
    return jnp.asarray(ov, F32)


def kernel(x, rel_bias, ffn1_norm, ffn1_w_gate, ffn1_w_up, ffn1_w_down, mix_norm, w_in, nsa_pe_k, nsa_pe_v,
           nsa_phi_k1, nsa_phi_k2, nsa_phi_v1, nsa_phi_v2, w_up_a, w_up_b, w_up_c, w_o, ffn2_norm, ffn2_w_gate,
           ffn2_w_up, ffn2_w_down, final_norm):
    batch, seq, d = x.shape
    depth = w_in.shape[0]
    nq = seq // TQ
    assert seq % (TQ * DIL_PATTERNS[-1][1]) == 0 and all(w // dl == DIL_WB for w, dl in DIL_PATTERNS)
    bf = lambda a: a.astype(MXU_DTYPE)

    n_cmp = (seq - NSA_CMP_BLOCK) // NSA_CMP_STRIDE + 1
    n_cmp_pad = seq // NSA_CMP_STRIDE
    a_heads = list(range(NSA_HEADS))
    ab_heads = list(range(NSA_HEADS + MOBA_HEADS))
    assert TILE_PAD >= max(WIN_TILES, KV_CHUNK - 1)
    t_main = _bias_tiles(rel_bias, n_tiles=nq + TILE_PAD, heads=ab_heads, offset=-TILE_PAD * TQ, name="bias_causal")
    t_win = _bias_tiles(rel_bias, n_tiles=WIN_TILES + NSA_QS + TILE_PAD, heads=a_heads, offset=-TILE_PAD * TQ,
                        hi=NSA_WINDOW - 1, name="bias_window")
    t_cmp = _bias_tiles(rel_bias, n_tiles=nq, heads=a_heads, width=n_cmp_pad, col_mult=NSA_CMP_STRIDE,
                        offset=-(NSA_CMP_BLOCK - 1), n_cols=n_cmp, name="bias_compressed")
    t_dil = []
    for g, (_, dilation) in enumerate(DIL_PATTERNS):
        h0 = NSA_HEADS + MOBA_HEADS + g * DIL_HEADS_PER_GROUP
        t_dil.append(_bias_tiles(rel_bias, n_tiles=3, heads=[h0, h0 + 1], dist_mult=dilation, hi=DIL_WB,
                                 name=f"bias_dilated_{g}"))
    c2s = _cmp_to_slc(seq, n_cmp_pad)

    x2 = x.reshape(batch * seq, d)
    for l in range(depth):
        last = l == depth - 1
        x2 = _ffn(x2, ffn1_norm[l], bf(ffn1_w_gate[l]), bf(ffn1_w_up[l]), bf(ffn1_w_down[l]))

        w_attn, w_merge = _proj_weights(w_in[l])
        p = _proj(x2, mix_norm[l], w_attn, seq)
        pek, wk1, wk2 = _compress_weights(nsa_pe_k[l], nsa_phi_k1[l], nsa_phi_k2[l])
        pev, wv1, wv2 = _compress_weights(nsa_pe_v[l], nsa_phi_v1[l], nsa_phi_v2[l])
        kc, vc = _compress(p["nsa_kc"], p["nsa_vc"], pek, pev, wk1, wv1, wk2, wv2, batch, seq)
        ya = _nsa(p, kc, vc, c2s, t_cmp, t_main, t_win, batch, seq)
        yb = _moba(p, p["moba_kmean"], t_main, batch, seq)
        dil = [_dilated(p[f"dil_q{g}"], p[f"dil_k{g}"], p[f"dil_v{g}"], t_dil[g], dilation, batch, seq)
               for g, (_, dilation) in enumerate(DIL_PATTERNS)]
        x2 = _mixer_out(x2, mix_norm[l], ya.reshape(batch * seq, -1), yb.reshape(batch * seq, -1), dil,
                        w_merge, bf(_packed_rows(w_up_a[l], NSA_KV_HEADS, NSA_REP)),
                        bf(_packed_rows(w_up_b[l], MOBA_HEADS // MOBA_HS, MOBA_HS)), bf(w_up_c[l]), bf(w_o[l]))

        x2 = _ffn(x2, ffn2_norm[l], bf(ffn2_w_gate[l]), bf(ffn2_w_up[l]), bf(ffn2_w_down[l]),
                  final_gain=final_norm if last else None)
    return x2.reshape(batch, seq, d)
```
